```python
import jax, jax.numpy as jnp
from jax import lax
import numpy as np

D_MODEL = 2048
BATCH = 8
SEQ = 8192
DEPTH = 4

MEM_LEN = 256
CONV_DIM = D_MODEL // 2
CONV_KERNEL = 31
SCONV_DIM = D_MODEL // 2
SCONV_KERNEL = 3
XATTN_DIM = D_MODEL // 2
XATTN_HEADS = 4
XATTN_HEAD_DIM = XATTN_DIM // XATTN_HEADS
N_BRANCH = 3
D_FF = 4 * D_MODEL
EPS = 1e-6

IN_SIZES = (CONV_DIM, CONV_DIM, SCONV_DIM, SCONV_DIM, SCONV_DIM, XATTN_DIM, N_BRANCH * D_MODEL)
IN_DIM = int(sum(IN_SIZES))
IN_SPLITS = tuple(int(v) for v in np.cumsum(IN_SIZES)[:-1])

kernel_name = "hybrid_conformer_shortconv_memxattn_block"


def rms_norm(x, g):
    xf = x.astype(jnp.float32)
    y = xf * lax.rsqrt(jnp.mean(xf * xf, axis=-1, keepdims=True) + EPS)
    return (y * g.astype(jnp.float32)).astype(x.dtype)


def layer_norm(x, g, b):
    xf = x.astype(jnp.float32)
    mu = jnp.mean(xf, axis=-1, keepdims=True)
    var = jnp.mean(jnp.square(xf - mu), axis=-1, keepdims=True)
    y = (xf - mu) * lax.rsqrt(var + EPS)
    return (y * g.astype(jnp.float32) + b.astype(jnp.float32)).astype(x.dtype)


def causal_depthwise_conv(u, w):
    k, c = w.shape
    return lax.conv_general_dilated(
        u, w[:, None, :].astype(u.dtype),
        window_strides=(1,), padding=[(k - 1, 0)],
        dimension_numbers=("NWC", "WIO", "NWC"),
        feature_group_count=c)


def memory_cross_attention(q, mem_n, w_kv):
    b, s, _ = q.shape
    m = mem_n.shape[1]
    kv = mem_n @ w_kv
    k, v = jnp.split(kv, 2, axis=-1)
    qh = q.reshape(b, s, XATTN_HEADS, XATTN_HEAD_DIM)
    kh = k.reshape(b, m, XATTN_HEADS, XATTN_HEAD_DIM)
    vh = v.reshape(b, m, XATTN_HEADS, XATTN_HEAD_DIM)
    scale = XATTN_HEAD_DIM ** -0.5
    scores = jnp.einsum("bshd,bmhd->bhsm", qh, kh).astype(jnp.float32) * scale
    probs = jax.nn.softmax(scores, axis=-1).astype(v.dtype)
    o = jnp.einsum("bhsm,bmhd->bshd", probs, vh)
    return o.reshape(b, s, XATTN_DIM)


def hybrid_layer(x, mem, g_mix_pre, w_in, conv_a_w, conv_a_b, ln_a_g, ln_a_b, w_a_out,
                 conv_b_w, w_b_out, g_mem, w_kv, w_x_out, w_o, g_mix_post,
                 g_mlp_pre, w_up, w_down, g_mlp_post):
    b, s, d = x.shape
    h = rms_norm(x, g_mix_pre)
    proj = h @ w_in
    a_val, a_gate, sb, sc, sx, q, gates = jnp.split(proj, IN_SPLITS, axis=-1)

    a = a_val * jax.nn.sigmoid(a_gate)
    a = causal_depthwise_conv(a, conv_a_w) + conv_a_b
    a = jax.nn.silu(layer_norm(a, ln_a_g, ln_a_b))
    y_a = a @ w_a_out

    u = causal_depthwise_conv(sc * sx, conv_b_w)
    y_b = (sb * u) @ w_b_out

    mem_n = rms_norm(mem, g_mem)
    y_x = memory_cross_attention(q, mem_n, w_kv) @ w_x_out

    g = jax.nn.sigmoid(gates).reshape(b, s, N_BRANCH, d)
    merged = g[:, :, 0] * y_a + g[:, :, 1] * y_b + g[:, :, 2] * y_x
    x = x + rms_norm(merged @ w_o, g_mix_post)

    h = rms_norm(x, g_mlp_pre)
    f = jnp.square(jax.nn.relu(h @ w_up)) @ w_down
    x = x + rms_norm(f, g_mlp_post)
    return x


def _fwd_setup_inputs(seed: int = 0) -> dict:
    key = jax.random.key(seed)
    ks = jax.random.split(key, 24)
    f32 = jnp.float32

    def nrm(k, shape, scale):
        return jax.random.normal(k, shape, f32) * scale

    def gain(k, shape):
        return 1.0 + 0.02 * jax.random.normal(k, shape, f32)

    L, D = DEPTH, D_MODEL
    return {
        "x": nrm(ks[0], (BATCH, SEQ, D), 1.0),
        "mem": nrm(ks[1], (BATCH, MEM_LEN, D), 1.0),
        "g_mix_pre": gain(ks[2], (L, D)),
        "w_in": nrm(ks[3], (L, D, IN_DIM), D ** -0.5),
        "conv_a_w": nrm(ks[4], (L, CONV_KERNEL, CONV_DIM), CONV_KERNEL ** -0.5),
        "conv_a_b": nrm(ks[5], (L, CONV_DIM), 0.02),
        "ln_a_g": gain(ks[6], (L, CONV_DIM)),
        "ln_a_b": nrm(ks[7], (L, CONV_DIM), 0.02),
        "w_a_out": nrm(ks[8], (L, CONV_DIM, D), CONV_DIM ** -0.5),
        "conv_b_w": nrm(ks[9], (L, SCONV_KERNEL, SCONV_DIM), SCONV_KERNEL ** -0.5),
        "w_b_out": nrm(ks[10], (L, SCONV_DIM, D), SCONV_DIM ** -0.5),
        "g_mem": gain(ks[11], (L, D)),
        "w_kv": nrm(ks[12], (L, D, 2 * XATTN_DIM), D ** -0.5),
        "w_x_out": nrm(ks[13], (L, XATTN_DIM, D), XATTN_DIM ** -0.5),
        "w_o": nrm(ks[14], (L, D, D), D ** -0.5),
        "g_mix_post": gain(ks[15], (L, D)),
        "g_mlp_pre": gain(ks[16], (L, D)),
        "w_up": nrm(ks[17], (L, D, D_FF), D ** -0.5),
        "w_down": nrm(ks[18], (L, D_FF, D), D_FF ** -0.5),
        "g_mlp_post": gain(ks[19], (L, D)),
    }


def _fwd_reference(x, mem, g_mix_pre, w_in, conv_a_w, conv_a_b, ln_a_g, ln_a_b, w_a_out,
              conv_b_w, w_b_out, g_mem, w_kv, w_x_out, w_o, g_mix_post,
              g_mlp_pre, w_up, w_down, g_mlp_post):
    for l in range(DEPTH):
        x = hybrid_layer(x, mem, g_mix_pre[l], w_in[l], conv_a_w[l], conv_a_b[l],
                         ln_a_g[l], ln_a_b[l], w_a_out[l], conv_b_w[l], w_b_out[l],
                         g_mem[l], w_kv[l], w_x_out[l], w_o[l], g_mix_post[l],
                         g_mlp_pre[l], w_up[l], w_down[l], g_mlp_post[l])
    return x


import jax as _jax
import jax.numpy as _jnp

TWIN_FORMAT = 'train_step'
FWD_PARAMS = ['x', 'mem', 'g_mix_pre', 'w_in', 'conv_a_w', 'conv_a_b', 'ln_a_g', 'ln_a_b', 'w_a_out', 'conv_b_w', 'w_b_out', 'g_mem', 'w_kv', 'w_x_out', 'w_o', 'g_mix_post', 'g_mlp_pre', 'w_up', 'w_down', 'g_mlp_post']
TWIN_WEIGHTS = ['g_mix_pre', 'w_in', 'conv_a_w', 'conv_a_b', 'ln_a_g', 'ln_a_b', 'w_a_out', 'conv_b_w', 'w_b_out', 'g_mem', 'w_kv', 'w_x_out', 'w_o', 'g_mix_post', 'g_mlp_pre', 'w_up', 'w_down', 'g_mlp_post']
TWIN_DIFF_INPUT = 'x'
TWIN_INPUTS = ['x', 'mem', 'g_mix_pre', 'w_in', 'conv_a_w', 'conv_a_b', 'ln_a_g', 'ln_a_b', 'w_a_out', 'conv_b_w', 'w_b_out', 'g_mem', 'w_kv', 'w_x_out', 'w_o', 'g_mix_post', 'g_mlp_pre', 'w_up', 'w_down', 'g_mlp_post', 'loss_target', 'm_g_mix_pre', 'm_w_in', 'm_conv_a_w', 'm_conv_a_b', 'm_ln_a_g', 'm_ln_a_b', 'm_w_a_out', 'm_conv_b_w', 'm_w_b_out', 'm_g_mem', 'm_w_kv', 'm_w_x_out', 'm_w_o', 'm_g_mix_post', 'm_g_mlp_pre', 'm_w_up', 'm_w_down', 'm_g_mlp_post', 'v_g_mix_pre', 'v_w_in', 'v_conv_a_w', 'v_conv_a_b', 'v_ln_a_g', 'v_ln_a_b', 'v_w_a_out', 'v_conv_b_w', 'v_w_b_out', 'v_g_mem', 'v_w_kv', 'v_w_x_out', 'v_w_o', 'v_g_mix_post', 'v_g_mlp_pre', 'v_w_up', 'v_w_down', 'v_g_mlp_post']
TWIN_OUTPUTS = ['loss', 'grad_x', 'grad_g_mix_pre', 'grad_w_in', 'grad_conv_a_w', 'grad_conv_a_b', 'grad_ln_a_g', 'grad_ln_a_b', 'grad_w_a_out', 'grad_conv_b_w', 'grad_w_b_out', 'grad_g_mem', 'grad_w_kv', 'grad_w_x_out', 'grad_w_o', 'grad_g_mix_post', 'grad_g_mlp_pre', 'grad_w_up', 'grad_w_down', 'grad_g_mlp_post', 'delta_g_mix_pre', 'delta_w_in', 'delta_conv_a_w', 'delta_conv_a_b', 'delta_ln_a_g', 'delta_ln_a_b', 'delta_w_a_out', 'delta_conv_b_w', 'delta_w_b_out', 'delta_g_mem', 'delta_w_kv', 'delta_w_x_out', 'delta_w_o', 'delta_g_mix_post', 'delta_g_mlp_pre', 'delta_w_up', 'delta_w_down', 'delta_g_mlp_post', 'new_m_g_mix_pre', 'new_m_w_in', 'new_m_conv_a_w', 'new_m_conv_a_b', 'new_m_ln_a_g', 'new_m_ln_a_b', 'new_m_w_a_out', 'new_m_conv_b_w', 'new_m_w_b_out', 'new_m_g_mem', 'new_m_w_kv', 'new_m_w_x_out', 'new_m_w_o', 'new_m_g_mix_post', 'new_m_g_mlp_pre', 'new_m_w_up', 'new_m_w_down', 'new_m_g_mlp_post', 'new_v_g_mix_pre', 'new_v_w_in', 'new_v_conv_a_w', 'new_v_conv_a_b', 'new_v_ln_a_g', 'new_v_ln_a_b', 'new_v_w_a_out', 'new_v_conv_b_w', 'new_v_w_b_out', 'new_v_g_mem', 'new_v_w_kv', 'new_v_w_x_out', 'new_v_w_o', 'new_v_g_mix_post', 'new_v_g_mlp_pre', 'new_v_w_up', 'new_v_w_down', 'new_v_g_mlp_post']
TWIN_LEAF_KINDS = {'loss': 'loss', 'grad_x': 'grad_x', 'grad_g_mix_pre': 'grad_w', 'grad_w_in': 'grad_w', 'grad_conv_a_w': 'grad_w', 'grad_conv_a_b': 'grad_w', 'grad_ln_a_g': 'grad_w', 'grad_ln_a_b': 'grad_w', 'grad_w_a_out': 'grad_w', 'grad_conv_b_w': 'grad_w', 'grad_w_b_out': 'grad_w', 'grad_g_mem': 'grad_w', 'grad_w_kv': 'grad_w', 'grad_w_x_out': 'grad_w', 'grad_w_o': 'grad_w', 'grad_g_mix_post': 'grad_w', 'grad_g_mlp_pre': 'grad_w', 'grad_w_up': 'grad_w', 'grad_w_down': 'grad_w', 'grad_g_mlp_post': 'grad_w', 'delta_g_mix_pre': 'delta_w', 'delta_w_in': 'delta_w', 'delta_conv_a_w': 'delta_w', 'delta_conv_a_b': 'delta_w', 'delta_ln_a_g': 'delta_w', 'delta_ln_a_b': 'delta_w', 'delta_w_a_out': 'delta_w', 'delta_conv_b_w': 'delta_w', 'delta_w_b_out': 'delta_w', 'delta_g_mem': 'delta_w', 'delta_w_kv': 'delta_w', 'delta_w_x_out': 'delta_w', 'delta_w_o': 'delta_w', 'delta_g_mix_post': 'delta_w', 'delta_g_mlp_pre': 'delta_w', 'delta_w_up': 'delta_w', 'delta_w_down': 'delta_w', 'delta_g_mlp_post': 'delta_w', 'new_m_g_mix_pre': 'new_m', 'new_m_w_in': 'new_m', 'new_m_conv_a_w': 'new_m', 'new_m_conv_a_b': 'new_m', 'new_m_ln_a_g': 'new_m', 'new_m_ln_a_b': 'new_m', 'new_m_w_a_out': 'new_m', 'new_m_conv_b_w': 'new_m', 'new_m_w_b_out': 'new_m', 'new_m_g_mem': 'new_m', 'new_m_w_kv': 'new_m', 'new_m_w_x_out': 'new_m', 'new_m_w_o': 'new_m', 'new_m_g_mix_post': 'new_m', 'new_m_g_mlp_pre': 'new_m', 'new_m_w_up': 'new_m', 'new_m_w_down': 'new_m', 'new_m_g_mlp_post': 'new_m', 'new_v_g_mix_pre': 'new_v', 'new_v_w_in': 'new_v', 'new_v_conv_a_w': 'new_v', 'new_v_conv_a_b': 'new_v', 'new_v_ln_a_g': 'new_v', 'new_v_ln_a_b': 'new_v', 'new_v_w_a_out': 'new_v', 'new_v_conv_b_w': 'new_v', 'new_v_w_b_out': 'new_v', 'new_v_g_mem': 'new_v', 'new_v_w_kv': 'new_v', 'new_v_w_x_out': 'new_v', 'new_v_w_o': 'new_v', 'new_v_g_mix_post': 'new_v', 'new_v_g_mlp_pre': 'new_v', 'new_v_w_up': 'new_v', 'new_v_w_down': 'new_v', 'new_v_g_mlp_post': 'new_v'}


def _forward(args):
    return _fwd_reference(*[args[k] for k in FWD_PARAMS])


def _output_shape():
    def fwd():
        inp = _fwd_setup_inputs(0)
        return _fwd_reference(*[inp[k] for k in FWD_PARAMS])
    out = _jax.eval_shape(fwd)
    return out.shape, out.dtype

N_MICROBATCH = 1
ADAM_LR = 0.001
ADAM_B1 = 0.9
ADAM_B2 = 0.999
ADAM_EPS = 1e-08
ADAM_WD = 0.01
ADAM_STEP = 10
PER_EXAMPLE_BATCH_AXIS = {'x': 0, 'mem': 0, 'loss_target': 0}
SHARED_INPUTS = []
_WEIGHT_DTYPES = {'g_mix_pre': _jnp.float32, 'w_in': _jnp.float32, 'conv_a_w': _jnp.float32, 'conv_a_b': _jnp.float32, 'ln_a_g': _jnp.float32, 'ln_a_b': _jnp.float32, 'w_a_out': _jnp.float32, 'conv_b_w': _jnp.float32, 'w_b_out': _jnp.float32, 'g_mem': _jnp.float32, 'w_kv': _jnp.float32, 'w_x_out': _jnp.float32, 'w_o': _jnp.float32, 'g_mix_post': _jnp.float32, 'g_mlp_pre': _jnp.float32, 'w_up': _jnp.float32, 'w_down': _jnp.float32, 'g_mlp_post': _jnp.float32}
MOMENT_SCALE = {'g_mix_pre': 2.497460e+00, 'w_in': 1.027634e+00, 'conv_a_w': 4.040682e+00, 'conv_a_b': 3.204411e+01, 'ln_a_g': 1.349420e+01, 'ln_a_b': 1.900791e+01, 'w_a_out': 6.200406e+00, 'conv_b_w': 1.117657e+00, 'w_b_out': 8.766828e-01, 'g_mem': 5.623034e-01, 'w_kv': 5.499446e-01, 'w_x_out': 5.526396e-01, 'w_o': 5.703554e+00, 'g_mix_post': 3.241887e+01, 'g_mlp_pre': 4.434452e+00, 'w_up': 2.141815e+00, 'w_down': 1.293620e+01, 'g_mlp_post': 3.557441e+01}


def _to_microbatches(a, axis):
    t = _jnp.moveaxis(a, axis, 0)
    t = t.reshape((N_MICROBATCH, t.shape[0] // N_MICROBATCH) + t.shape[1:])
    return _jnp.moveaxis(t, 1, axis + 1)


def setup_inputs(seed: int = 0) -> dict:
    inp = _fwd_setup_inputs(seed)
    key = _jax.random.fold_in(_jax.random.key(seed), 7919)
    shape, _ = _output_shape()
    out = dict(inp)
    out["loss_target"] = _jax.random.normal(_jax.random.fold_in(key, 0), shape, _jnp.float32)
    for i, name in enumerate(TWIN_WEIGHTS):
        w = inp[name].astype(_jnp.float32)
        if MOMENT_SCALE is None:
            s = _jnp.sqrt(_jnp.mean(_jnp.square(w)) + 1e-30)
        else:
            s = MOMENT_SCALE[name]
        km, kv = _jax.random.split(_jax.random.fold_in(key, i + 1))
        out[name] = w
        out["m_" + name] = s * _jax.random.normal(km, w.shape, _jnp.float32)
        out["v_" + name] = (s * s) * _jax.random.uniform(kv, w.shape, _jnp.float32, 0.5, 1.5)
    if N_MICROBATCH > 1:
        for name, axis in PER_EXAMPLE_BATCH_AXIS.items():
            out[name] = _to_microbatches(out[name], axis)
    return {'x': out['x'], 'mem': out['mem'], 'g_mix_pre': out['g_mix_pre'], 'w_in': out['w_in'], 'conv_a_w': out['conv_a_w'], 'conv_a_b': out['conv_a_b'], 'ln_a_g': out['ln_a_g'], 'ln_a_b': out['ln_a_b'], 'w_a_out': out['w_a_out'], 'conv_b_w': out['conv_b_w'], 'w_b_out': out['w_b_out'], 'g_mem': out['g_mem'], 'w_kv': out['w_kv'], 'w_x_out': out['w_x_out'], 'w_o': out['w_o'], 'g_mix_post': out['g_mix_post'], 'g_mlp_pre': out['g_mlp_pre'], 'w_up': out['w_up'], 'w_down': out['w_down'], 'g_mlp_post': out['g_mlp_post'], 'loss_target': out['loss_target'], 'm_g_mix_pre': out['m_g_mix_pre'], 'm_w_in': out['m_w_in'], 'm_conv_a_w': out['m_conv_a_w'], 'm_conv_a_b': out['m_conv_a_b'], 'm_ln_a_g': out['m_ln_a_g'], 'm_ln_a_b': out['m_ln_a_b'], 'm_w_a_out': out['m_w_a_out'], 'm_conv_b_w': out['m_conv_b_w'], 'm_w_b_out': out['m_w_b_out'], 'm_g_mem': out['m_g_mem'], 'm_w_kv': out['m_w_kv'], 'm_w_x_out': out['m_w_x_out'], 'm_w_o': out['m_w_o'], 'm_g_mix_post': out['m_g_mix_post'], 'm_g_mlp_pre': out['m_g_mlp_pre'], 'm_w_up': out['m_w_up'], 'm_w_down': out['m_w_down'], 'm_g_mlp_post': out['m_g_mlp_post'], 'v_g_mix_pre': out['v_g_mix_pre'], 'v_w_in': out['v_w_in'], 'v_conv_a_w': out['v_conv_a_w'], 'v_conv_a_b': out['v_conv_a_b'], 'v_ln_a_g': out['v_ln_a_g'], 'v_ln_a_b': out['v_ln_a_b'], 'v_w_a_out': out['v_w_a_out'], 'v_conv_b_w': out['v_conv_b_w'], 'v_w_b_out': out['v_w_b_out'], 'v_g_mem': out['v_g_mem'], 'v_w_kv': out['v_w_kv'], 'v_w_x_out': out['v_w_x_out'], 'v_w_o': out['v_w_o'], 'v_g_mix_post': out['v_g_mix_post'], 'v_g_mlp_pre': out['v_g_mlp_pre'], 'v_w_up': out['v_w_up'], 'v_w_down': out['v_w_down'], 'v_g_mlp_post': out['v_g_mlp_post']}


def _loss(weights, diff, rest, loss_target):
    with _jax.named_scope("forward"):
        args = {**rest, TWIN_DIFF_INPUT: diff, **{k: w.astype(_WEIGHT_DTYPES[k]) for k, w in weights.items()}}
        y = _forward(args)
    with _jax.named_scope("loss_head"):
        err = _jnp.square(y.astype(_jnp.float32) - loss_target)
        return 0.5 * _jnp.sum(_jnp.mean(err, axis=-1)) if err.ndim else 0.5 * err


def _adamw(w, g, m, v):
    m = ADAM_B1 * m + (1.0 - ADAM_B1) * g
    v = ADAM_B2 * v + (1.0 - ADAM_B2) * _jnp.square(g)
    m_hat = m / (1.0 - ADAM_B1 ** ADAM_STEP)
    v_hat = v / (1.0 - ADAM_B2 ** ADAM_STEP)
    delta = -ADAM_LR * (m_hat / (_jnp.sqrt(v_hat) + ADAM_EPS) + ADAM_WD * w)
    return delta, m, v


def reference(x, mem, g_mix_pre, w_in, conv_a_w, conv_a_b, ln_a_g, ln_a_b, w_a_out, conv_b_w, w_b_out, g_mem, w_kv, w_x_out, w_o, g_mix_post, g_mlp_pre, w_up, w_down, g_mlp_post, loss_target, m_g_mix_pre, m_w_in, m_conv_a_w, m_conv_a_b, m_ln_a_g, m_ln_a_b, m_w_a_out, m_conv_b_w, m_w_b_out, m_g_mem, m_w_kv, m_w_x_out, m_w_o, m_g_mix_post, m_g_mlp_pre, m_w_up, m_w_down, m_g_mlp_post, v_g_mix_pre, v_w_in, v_conv_a_w, v_conv_a_b, v_ln_a_g, v_ln_a_b, v_w_a_out, v_conv_b_w, v_w_b_out, v_g_mem, v_w_kv, v_w_x_out, v_w_o, v_g_mix_post, v_g_mlp_pre, v_w_up, v_w_down, v_g_mlp_post):
    given = dict(x=x, mem=mem, g_mix_pre=g_mix_pre, w_in=w_in, conv_a_w=conv_a_w, conv_a_b=conv_a_b, ln_a_g=ln_a_g, ln_a_b=ln_a_b, w_a_out=w_a_out, conv_b_w=conv_b_w, w_b_out=w_b_out, g_mem=g_mem, w_kv=w_kv, w_x_out=w_x_out, w_o=w_o, g_mix_post=g_mix_post, g_mlp_pre=g_mlp_pre, w_up=w_up, w_down=w_down, g_mlp_post=g_mlp_post, loss_target=loss_target, m_g_mix_pre=m_g_mix_pre, m_w_in=m_w_in, m_conv_a_w=m_conv_a_w, m_conv_a_b=m_conv_a_b, m_ln_a_g=m_ln_a_g, m_ln_a_b=m_ln_a_b, m_w_a_out=m_w_a_out, m_conv_b_w=m_conv_b_w, m_w_b_out=m_w_b_out, m_g_mem=m_g_mem, m_w_kv=m_w_kv, m_w_x_out=m_w_x_out, m_w_o=m_w_o, m_g_mix_post=m_g_mix_post, m_g_mlp_pre=m_g_mlp_pre, m_w_up=m_w_up, m_w_down=m_w_down, m_g_mlp_post=m_g_mlp_post, v_g_mix_pre=v_g_mix_pre, v_w_in=v_w_in, v_conv_a_w=v_conv_a_w, v_conv_a_b=v_conv_a_b, v_ln_a_g=v_ln_a_g, v_ln_a_b=v_ln_a_b, v_w_a_out=v_w_a_out, v_conv_b_w=v_conv_b_w, v_w_b_out=v_w_b_out, v_g_mem=v_g_mem, v_w_kv=v_w_kv, v_w_x_out=v_w_x_out, v_w_o=v_w_o, v_g_mix_post=v_g_mix_post, v_g_mlp_pre=v_g_mlp_pre, v_w_up=v_w_up, v_w_down=v_w_down, v_g_mlp_post=v_g_mlp_post)
    weights = {n: given[n] for n in TWIN_WEIGHTS}
    shared = {n: given[n] for n in SHARED_INPUTS}
    per_example = {n: given[n] for n in ['x', 'mem']}
    grad_fn = _jax.value_and_grad(_loss, argnums=(0, 1))

    def one_microbatch(ex, loss_target):
        ex = dict(ex)
        diff = ex.pop(TWIN_DIFF_INPUT)
        return grad_fn(weights, diff, {**shared, **ex}, loss_target)

    if N_MICROBATCH == 1:
        loss, (grad_w, grad_x) = one_microbatch(per_example, given["loss_target"])
    else:
        def body(carry, xs):
            loss_sum, grad_sum = carry
            l_k, (gw_k, gx_k) = one_microbatch(xs[0], xs[1])
            with _jax.named_scope("update"):
                return (loss_sum + l_k, _jax.tree.map(_jnp.add, grad_sum, gw_k)), gx_k

        init = (_jnp.zeros((), _jnp.float32), _jax.tree.map(_jnp.zeros_like, weights))
        (loss, grad_w), grad_x = _jax.lax.scan(body, init, (per_example, given["loss_target"]))
    with _jax.named_scope("update"):
        delta_w, new_m, new_v = {}, {}, {}
        for n in TWIN_WEIGHTS:
            delta_w[n], new_m[n], new_v[n] = _adamw(weights[n], grad_w[n], given["m_" + n], given["v_" + n])
    return (loss, grad_x, *[grad_w[n] for n in TWIN_WEIGHTS], *[delta_w[n] for n in TWIN_WEIGHTS],
            *[new_m[n] for n in TWIN_WEIGHTS], *[new_v[n] for n in TWIN_WEIGHTS])
```

```python
import math

import jax
import jax.numpy as jnp
from jax import lax
from jax.experimental import pallas as pl
from jax.experimental.pallas import tpu as pltpu

F32 = jnp.float32
BF16 = jnp.bfloat16

EPS = 1e-6
N_HEADS = 4
N_BRANCH = 3
CONV_A_K = 31
CONV_B_K = 3
HALO_A = 32
HALO_B = 8
N_DEV = 8
MESH_AXES = ("x", "y", "c")

ADAM_LR = 0.001
ADAM_B1 = 0.9
ADAM_B2 = 0.999
ADAM_EPS = 1e-08
ADAM_WD = 0.01
ADAM_STEP = 10

V7X_VMEM_BYTES = 64 * 1024 * 1024
VMEM_LIMIT = V7X_VMEM_BYTES - 8 * 1024 * 1024

TILE_M = 1024
TILE_N = 1024
TILE_K = 2048
TILE_ROWS_WIDE = 512
TILE_ROWS_BRANCH = 256
TILE_ROWS_BRANCH_BWD = 128


def _params():
    return pltpu.CompilerParams(vmem_limit_bytes=VMEM_LIMIT)


def _tile(n, t):
    t = min(n, t)
    assert n % t == 0, (n, t)
    return t


def _my_index():
    return 4 * lax.axis_index("x") + 2 * lax.axis_index("y") + lax.axis_index("c")


def _mesh_id(p):
    return (p // 4, (p // 2) % 2, p % 2)


def _mm(name, grid, a, a_spec, b, b_spec, *, ta, tb, acc_shape, extras, outs, epilogue):
    nk = grid[2]
    ne, no = len(extras), len(outs)
    dn = (((0,) if ta else (1,), (1,) if tb else (0,)), ((), ()))

    def body(a_ref, b_ref, *rest):
        extra_refs = rest[:ne]
        out_refs = rest[ne:ne + no]
        i, j, k = pl.program_id(0), pl.program_id(1), pl.program_id(2)
        prod = lax.dot_general(a_ref[...], b_ref[...], dn, preferred_element_type=F32)
        if nk == 1:
            epilogue(prod, extra_refs, out_refs, i, j)
        else:
            acc_ref = rest[ne + no]

            @pl.when(k == 0)
            def _():
                acc_ref[...] = prod

            @pl.when(k > 0)
            def _():
                acc_ref[...] += prod

            @pl.when(k == nk - 1)
            def _():
                epilogue(acc_ref[...], extra_refs, out_refs, i, j)

    return pl.pallas_call(
        body, name=name, grid=grid,
        in_specs=[a_spec, b_spec] + [s for _, s in extras],
        out_specs=[s for _, s in outs],
        out_shape=[o for o, _ in outs],
        scratch_shapes=[pltpu.VMEM(acc_shape, F32)] if nk > 1 else [],
        compiler_params=_params(),
    )(a, b, *[e for e, _ in extras])


def _store_epilogue(res, extra_refs, out_refs, i, j):
    out_refs[0][...] = res.astype(out_refs[0].dtype)


def _mm_nn(name, a, b, out_dtype):
    m, kd = a.shape
    n = b.shape[1]
    bm, bn = _tile(m, TILE_M), _tile(n, TILE_N)
    return _mm(name, (m // bm, n // bn, 1), a, pl.BlockSpec((bm, kd), lambda i, j, k: (i, 0)),
               b, pl.BlockSpec((kd, bn), lambda i, j, k: (0, j)), ta=False, tb=False, acc_shape=(bm, bn), extras=[],
               outs=[(jax.ShapeDtypeStruct((m, n), out_dtype), pl.BlockSpec((bm, bn), lambda i, j, k: (i, j)))],
               epilogue=_store_epilogue)[0]


def _mm_nt(name, a, b, out_dtype, a_lead=None, b_lead=None):
    m, kd = a.shape[-2:]
    n = b.shape[-2]
    bm, bn = _tile(m, TILE_M), _tile(n, TILE_N)
    if a_lead is None:
        a_spec = pl.BlockSpec((bm, kd), lambda i, j, k: (i, 0))
    else:
        a_spec = pl.BlockSpec((None, bm, kd), lambda i, j, k: (a_lead, i, 0))
    if b_lead is None:
        b_spec = pl.BlockSpec((bn, kd), lambda i, j, k: (j, 0))
    else:
        b_spec = pl.BlockSpec((None, bn, kd), lambda i, j, k: (b_lead, j, 0))
    return _mm(name, (m // bm, n // bn, 1), a, a_spec, b, b_spec, ta=False, tb=True, acc_shape=(bm, bn), extras=[],
               outs=[(jax.ShapeDtypeStruct((m, n), out_dtype), pl.BlockSpec((bm, bn), lambda i, j, k: (i, j)))],
               epilogue=_store_epilogue)[0]


def _mm_tn(name, a, b, out_dtype, a_lead=None, b_lead=None):
    r, m = a.shape[-2:]
    n = b.shape[-1]
    bm, bn, bk = _tile(m, TILE_M), _tile(n, TILE_N), _tile(r, TILE_K)
    if a_lead is None:
        a_spec = pl.BlockSpec((bk, bm), lambda i, j, k: (k, i))
    else:
        a_spec = pl.BlockSpec((None, bk, bm), lambda i, j, k: (a_lead, k, i))
    if b_lead is None:
        b_spec = pl.BlockSpec((bk, bn), lambda i, j, k: (k, j))
    else:
        b_spec = pl.BlockSpec((None, bk, bn), lambda i, j, k: (b_lead, k, j))
    return _mm(name, (m // bm, n // bn, r // bk), a, a_spec, b, b_spec, ta=True, tb=False, acc_shape=(bm, bn),
               extras=[],
               outs=[(jax.ShapeDtypeStruct((m, n), out_dtype), pl.BlockSpec((bm, bn), lambda i, j, k: (i, j)))],
               epilogue=_store_epilogue)[0]


def _mm_relu2(name, h, w):
    m, kd = h.shape
    n = w.shape[1]
    bm, bn = _tile(m, TILE_M), _tile(n, TILE_N)

    def epilogue(res, extra_refs, out_refs, i, j):
        r = jnp.maximum(res, 0.0)
        out_refs[0][...] = r.astype(BF16)
        out_refs[1][...] = (r * r).astype(BF16)

    o = jax.ShapeDtypeStruct((m, n), BF16)
    spec = pl.BlockSpec((bm, bn), lambda i, j, k: (i, j))
    return _mm(name, (m // bm, n // bn, 1), h, pl.BlockSpec((bm, kd), lambda i, j, k: (i, 0)),
               w, pl.BlockSpec((kd, bn), lambda i, j, k: (0, j)), ta=False, tb=False, acc_shape=(bm, bn), extras=[],
               outs=[(o, spec), (o, spec)], epilogue=epilogue)


def _mm_relu2_bwd(name, df, w_down, r):
    m, kd = df.shape
    n = w_down.shape[0]
    bm, bn = _tile(m, TILE_M), _tile(n, TILE_N)

    def epilogue(res, extra_refs, out_refs, i, j):
        out_refs[0][...] = (res * (2.0 * extra_refs[0][...].astype(F32))).astype(BF16)

    spec = pl.BlockSpec((bm, bn), lambda i, j, k: (i, j))
    return _mm(name, (m // bm, n // bn, 1), df, pl.BlockSpec((bm, kd), lambda i, j, k: (i, 0)),
               w_down, pl.BlockSpec((bn, kd), lambda i, j, k: (j, 0)), ta=False, tb=True, acc_shape=(bm, bn),
               extras=[(r, spec)], outs=[(jax.ShapeDtypeStruct((m, n), BF16), spec)], epilogue=epilogue)[0]


def _mm_res_norm(name, a, w, xres, g):
    m, kd = a.shape
    d = w.shape[1]
    bm, bk = _tile(m, TILE_ROWS_WIDE), _tile(kd, TILE_M)

    def epilogue(res, extra_refs, out_refs, i, j):
        x_ref, g_ref = extra_refs
        rr = lax.rsqrt(jnp.mean(res * res, axis=-1, keepdims=True) + EPS)
        out_refs[0][...] = res
        out_refs[1][...] = x_ref[...] + (res * rr) * g_ref[...]

    row = pl.BlockSpec((bm, d), lambda i, j, k: (i, 0))
    o = jax.ShapeDtypeStruct((m, d), F32)
    return _mm(name, (m // bm, 1, kd // bk), a, pl.BlockSpec((bm, bk), lambda i, j, k: (i, k)),
               w, pl.BlockSpec((bk, d), lambda i, j, k: (k, 0)), ta=False, tb=False, acc_shape=(bm, d),
               extras=[(xres, row), (g, pl.BlockSpec((1, d), lambda i, j, k: (0, 0)))],
               outs=[(o, row), (o, row)], epilogue=epilogue)


def _rms_bwd_math(dy, x, g):
    rr = lax.rsqrt(jnp.mean(x * x, axis=-1, keepdims=True) + EPS)
    gy = dy * g
    dx = rr * gy - x * (rr * rr * rr * jnp.mean(x * gy, axis=-1, keepdims=True))
    dg_rows = dy * x * rr
    return dx, dg_rows


def _mm_rms_bwd(name, a, w, x, g, dres):
    m, kd = a.shape
    d = w.shape[0]
    bm, bk = _tile(m, TILE_ROWS_WIDE), _tile(kd, TILE_M)

    def epilogue(res, extra_refs, out_refs, i, j):
        x_ref, g_ref, dres_ref = extra_refs
        dx, dg_rows = _rms_bwd_math(res, x_ref[...], g_ref[...])
        out_refs[0][...] = dres_ref[...] + dx

        @pl.when(i == 0)
        def _():
            out_refs[1][...] = jnp.zeros_like(out_refs[1])

        out_refs[1][...] += jnp.sum(dg_rows, axis=0, keepdims=True)

    row = pl.BlockSpec((bm, d), lambda i, j, k: (i, 0))
    vec = pl.BlockSpec((1, d), lambda i, j, k: (0, 0))
    return _mm(name, (m // bm, 1, kd // bk), a, pl.BlockSpec((bm, bk), lambda i, j, k: (i, k)),
               w, pl.BlockSpec((d, bk), lambda i, j, k: (0, k)), ta=False, tb=True, acc_shape=(bm, d),
               extras=[(x, row), (g, vec), (dres, row)],
               outs=[(jax.ShapeDtypeStruct((m, d), F32), row), (jax.ShapeDtypeStruct((1, d), F32), vec)],
               epilogue=epilogue)


def _rmsnorm(name, x, g):
    r, d = x.shape
    br = _tile(r, TILE_ROWS_WIDE)

    def body(x_ref, g_ref, o_ref):
        xv = x_ref[...]
        rr = lax.rsqrt(jnp.mean(xv * xv, axis=-1, keepdims=True) + EPS)
        o_ref[...] = ((xv * rr) * g_ref[...]).astype(BF16)

    row = pl.BlockSpec((br, d), lambda i: (i, 0))
    return pl.pallas_call(body, name=name, grid=(r // br,), in_specs=[row, pl.BlockSpec((1, d), lambda i: (0, 0))],
                          out_specs=row, out_shape=jax.ShapeDtypeStruct((r, d), BF16), compiler_params=_params())(x, g)


def _rms_bwd(name, dy, x, g, out_dtype):
    r, d = x.shape
    br = _tile(r, TILE_ROWS_WIDE)

    def body(dy_ref, x_ref, g_ref, dx_ref, dg_ref):
        dx, dg_rows = _rms_bwd_math(dy_ref[...], x_ref[...], g_ref[...])
        dx_ref[...] = dx.astype(out_dtype)

        @pl.when(pl.program_id(0) == 0)
        def _():
            dg_ref[...] = jnp.zeros_like(dg_ref)

        dg_ref[...] += jnp.sum(dg_rows, axis=0, keepdims=True)

    row = pl.BlockSpec((br, d), lambda i: (i, 0))
    vec = pl.BlockSpec((1, d), lambda i: (0, 0))
    return pl.pallas_call(body, name=name, grid=(r // br,), in_specs=[row, row, vec], out_specs=[row, vec],
                          out_shape=[jax.ShapeDtypeStruct((r, d), out_dtype), jax.ShapeDtypeStruct((1, d), F32)],
                          compiler_params=_params())(dy, x, g)


def _loss_head(y, target):
    r, d = y.shape
    br = _tile(r, TILE_ROWS_WIDE)

    def body(y_ref, t_ref, dy_ref, l_ref):
        diff = y_ref[...] - t_ref[...]
        dy_ref[...] = diff * (1.0 / d)

        @pl.when(pl.program_id(0) == 0)
        def _():
            l_ref[...] = jnp.zeros_like(l_ref)

        l_ref[...] += 0.5 * jnp.sum(jnp.mean(diff * diff, axis=-1, keepdims=True))

    row = pl.BlockSpec((br, d), lambda i: (i, 0))
    one = pl.BlockSpec((8, 128), lambda i: (0, 0))
    dy, l = pl.pallas_call(body, name="loss_head", grid=(r // br,), in_specs=[row, row], out_specs=[row, one],
                           out_shape=[jax.ShapeDtypeStruct((r, d), F32), jax.ShapeDtypeStruct((8, 128), F32)],
                           compiler_params=_params())(y, target)
    return l[0, 0], dy


def _adamw(name, parts, w, m, v):
    p, r, c = parts.shape
    br = r
    for cand in (512, 256, 128, 64, 32, 16):
        if r % cand == 0 and cand * c * 4 <= 2 * 1024 * 1024:
            br = cand
            break
    bc1 = 1.0 - ADAM_B1 ** ADAM_STEP
    bc2 = 1.0 - ADAM_B2 ** ADAM_STEP

    def body(p_ref, w_ref, m_ref, v_ref, g_out, d_out, m_out, v_out):
        g = p_ref[0].astype(F32)
        for s in range(1, p):
            g = g + p_ref[s].astype(F32)
        m2 = ADAM_B1 * m_ref[...] + (1.0 - ADAM_B1) * g
        v2 = ADAM_B2 * v_ref[...] + (1.0 - ADAM_B2) * (g * g)
        m_hat = m2 / bc1
        v_hat = v2 / bc2
        g_out[...] = g
        d_out[...] = -ADAM_LR * (m_hat / (jnp.sqrt(v_hat) + ADAM_EPS) + ADAM_WD * w_ref[...])
        m_out[...] = m2
        v_out[...] = v2

    blk = pl.BlockSpec((br, c), lambda i: (i, 0))
    o = jax.ShapeDtypeStruct((r, c), F32)
    return pl.pallas_call(body, name=name, grid=(r // br,),
                          in_specs=[pl.BlockSpec((p, br, c), lambda i: (0, i, 0)), blk, blk, blk],
                          out_specs=[blk] * 4, out_shape=[o] * 4, compiler_params=_params())(parts, w, m, v)


def _layer_norm_parts(a1, ln_g, ln_b):
    mu = jnp.mean(a1, axis=-1, keepdims=True)
    xc = a1 - mu
    rstd = lax.rsqrt(jnp.mean(xc * xc, axis=-1, keepdims=True) + EPS)
    xhat = xc * rstd
    a2 = xhat * ln_g + ln_b
    return xhat, rstd, a2


def _softmax_rows(s):
    e = jnp.exp(s - jnp.max(s, axis=-1, keepdims=True))
    return e / jnp.sum(e, axis=-1, keepdims=True)


def _branch_specs(ts, c):
    def col(ci):
        return pl.BlockSpec((ts, c), lambda i: (i, ci))

    def prev(ci, h):
        return pl.BlockSpec((h, c), lambda i: (jnp.maximum(i * (ts // h) - 1, 0), ci))

    return col, prev


def _branch_fwd(proj, kv, conv_a_w, conv_a_b, ln_g, ln_b, conv_b_w):
    s = proj.shape[0]
    mlen, c2 = kv.shape
    c = c2 // 2
    hd = c // N_HEADS
    ts = _tile(s, TILE_ROWS_BRANCH)
    scale = hd ** -0.5
    col, prev = _branch_specs(ts, c)

    def body(av, ag, sb, sc, sx, q, hav, hag, hsc, hsx, kv_ref, caw, cab, lng, lnb, cbw, a1_ref, abo_ref, exta, extb):
        not_first = (pl.program_id(0) > 0).astype(F32)
        exta[pl.ds(0, HALO_A), :] = hav[...] * jax.nn.sigmoid(hag[...]) * not_first
        exta[pl.ds(HALO_A, ts), :] = av[...] * jax.nn.sigmoid(ag[...])
        acc = jnp.broadcast_to(cab[...], (ts, c))
        for k in range(CONV_A_K):
            acc = acc + caw[pl.ds(k, 1), :] * exta[pl.ds(HALO_A - (CONV_A_K - 1) + k, ts), :]
        a1_ref[...] = acc
        _, _, a2 = _layer_norm_parts(acc, lng[...], lnb[...])
        abo_ref[0] = (a2 * jax.nn.sigmoid(a2)).astype(BF16)
        extb[pl.ds(0, HALO_B), :] = hsc[...] * hsx[...] * not_first
        extb[pl.ds(HALO_B, ts), :] = sc[...] * sx[...]
        u = cbw[pl.ds(0, 1), :] * extb[pl.ds(HALO_B - (CONV_B_K - 1), ts), :]
        for k in range(1, CONV_B_K):
            u = u + cbw[pl.ds(k, 1), :] * extb[pl.ds(HALO_B - (CONV_B_K - 1) + k, ts), :]
        abo_ref[1] = (sb[...] * u).astype(BF16)
        for h in range(N_HEADS):
            qh = q[:, h * hd:(h + 1) * hd].astype(BF16)
            kh = kv_ref[:, h * hd:(h + 1) * hd]
            vh = kv_ref[:, c + h * hd:c + (h + 1) * hd]
            sc_ = lax.dot_general(qh, kh, (((1,), (1,)), ((), ())), preferred_element_type=F32) * scale
            p = _softmax_rows(sc_).astype(BF16)
            abo_ref[2, :, h * hd:(h + 1) * hd] = jnp.dot(p, vh, preferred_element_type=F32).astype(BF16)

    full = lambda shp: pl.BlockSpec(shp, lambda i: (0,) * len(shp))
    return pl.pallas_call(
        body, name="branch_fwd", grid=(s // ts,),
        in_specs=[col(0), col(1), col(2), col(3), col(4), col(5), prev(0, HALO_A), prev(1, HALO_A),
                  prev(3, HALO_B), prev(4, HALO_B), full((mlen, c2)), full(conv_a_w.shape), full((1, c)),
                  full((1, c)), full((1, c)), full(conv_b_w.shape)],
        out_specs=[pl.BlockSpec((ts, c), lambda i: (i, 0)), pl.BlockSpec((N_BRANCH, ts, c), lambda i: (0, i, 0))],
        out_shape=[jax.ShapeDtypeStruct((s, c), F32), jax.ShapeDtypeStruct((N_BRANCH, s, c), BF16)],
        scratch_shapes=[pltpu.VMEM((HALO_A + ts, c), F32), pltpu.VMEM((HALO_B + ts, c), F32)],
        compiler_params=_params(),
    )(proj, proj, proj, proj, proj, proj, proj, proj, proj, proj, kv, conv_a_w, conv_a_b, ln_g, ln_b, conv_b_w)


def _branch_bwd(proj, a1, dabo, kv, conv_a_w, ln_g, ln_b, conv_b_w, dproj):
    s = proj.shape[0]
    mlen, c2 = kv.shape
    c = c2 // 2
    hd = c // N_HEADS
    ts = _tile(s, TILE_ROWS_BRANCH_BWD)
    nt = s // ts
    scale = hd ** -0.5
    col, prev = _branch_specs(ts, c)

    def nxt(h, lead=None, ci=0):
        if lead is None:
            return pl.BlockSpec((h, c), lambda i: (jnp.minimum((i + 1) * (ts // h), s // h - 1), ci))
        return pl.BlockSpec((None, h, c), lambda i: (lead, jnp.minimum((i + 1) * (ts // h), s // h - 1), 0))

    def body(av, ag, sb, sc, sx, q, hav, hag, hsc, hsx, nsb, a1_ref, na1, dabo_ref, nda, ndb, kv_ref, caw, lng, lnb,
             cbw, dproj_in, dp_ref, dkv_ref, dwa_ref, misc_ref, exta, extda, extb, extdu):
        i = pl.program_id(0)
        not_first = (i > 0).astype(F32)
        not_last = (i < nt - 1).astype(F32)

        @pl.when(i == 0)
        def _():
            dkv_ref[...] = jnp.zeros_like(dkv_ref)
            dwa_ref[...] = jnp.zeros_like(dwa_ref)
            misc_ref[...] = jnp.zeros_like(misc_ref)

        def rowsum(t):
            return jnp.sum(t, axis=0, keepdims=True)

        def da1_of(a1v, da3):
            xhat, rstd, a2 = _layer_norm_parts(a1v, lng[...], lnb[...])
            sg = jax.nn.sigmoid(a2)
            da2 = da3 * (sg * (1.0 + a2 * (1.0 - sg)))
            dxh = da2 * lng[...]
            da1 = rstd * (dxh - jnp.mean(dxh, axis=-1, keepdims=True)
                          - xhat * jnp.mean(dxh * xhat, axis=-1, keepdims=True))
            return da1, da2, xhat

        da1, da2, xhat = da1_of(a1_ref[...], dabo_ref[0])
        misc_ref[pl.ds(0, 1), :] += rowsum(da1)
        misc_ref[pl.ds(1, 1), :] += rowsum(da2 * xhat)
        misc_ref[pl.ds(2, 1), :] += rowsum(da2)
        extda[pl.ds(0, ts), :] = da1
        extda[pl.ds(ts, HALO_A), :] = da1_of(na1[...], nda[...])[0] * not_last
        sga = jax.nn.sigmoid(ag[...])
        exta[pl.ds(0, HALO_A), :] = hav[...] * jax.nn.sigmoid(hag[...]) * not_first
        exta[pl.ds(HALO_A, ts), :] = av[...] * sga
        da0 = jnp.zeros((ts, c), F32)
        for k in range(CONV_A_K):
            dwa_ref[pl.ds(k, 1), :] += rowsum(da1 * exta[pl.ds(HALO_A - (CONV_A_K - 1) + k, ts), :])
            da0 = da0 + caw[pl.ds(CONV_A_K - 1 - k, 1), :] * extda[pl.ds(k, ts), :]
        dp_ref[:, 0:c] = (da0 * sga).astype(BF16)
        dp_ref[:, c:2 * c] = (da0 * av[...] * sga * (1.0 - sga)).astype(BF16)

        extb[pl.ds(0, HALO_B), :] = hsc[...] * hsx[...] * not_first
        extb[pl.ds(HALO_B, ts), :] = sc[...] * sx[...]
        dbu = dabo_ref[1]
        du = dbu * sb[...]
        extdu[pl.ds(0, ts), :] = du
        extdu[pl.ds(ts, HALO_B), :] = ndb[...] * nsb[...] * not_last
        u = jnp.zeros((ts, c), F32)
        dpr = jnp.zeros((ts, c), F32)
        for k in range(CONV_B_K):
            shifted = extb[pl.ds(HALO_B - (CONV_B_K - 1) + k, ts), :]
            u = u + cbw[pl.ds(k, 1), :] * shifted
            misc_ref[pl.ds(3 + k, 1), :] += rowsum(du * shifted)
            dpr = dpr + cbw[pl.ds(CONV_B_K - 1 - k, 1), :] * extdu[pl.ds(k, ts), :]
        dp_ref[:, 2 * c:3 * c] = (dbu * u).astype(BF16)
        dp_ref[:, 3 * c:4 * c] = (dpr * sx[...]).astype(BF16)
        dp_ref[:, 4 * c:5 * c] = (dpr * sc[...]).astype(BF16)

        nt_dims = (((1,), (1,)), ((), ()))
        tn_dims = (((0,), (0,)), ((), ()))
        for h in range(N_HEADS):
            lo, hi = h * hd, (h + 1) * hd
            qh = q[:, lo:hi].astype(BF16)
            kh = kv_ref[:, lo:hi]
            vh = kv_ref[:, c + lo:c + hi]
            p = _softmax_rows(lax.dot_general(qh, kh, nt_dims, preferred_element_type=F32) * scale)
            pb = p.astype(BF16)
            doh = dabo_ref[2, :, lo:hi].astype(BF16)
            dpm = lax.dot_general(doh, vh, nt_dims, preferred_element_type=F32)
            ds = (p * (dpm - jnp.sum(dpm * p, axis=-1, keepdims=True)) * scale).astype(BF16)
            dp_ref[:, 5 * c + lo:5 * c + hi] = jnp.dot(ds, kh, preferred_element_type=F32).astype(BF16)
            dkv_ref[:, lo:hi] += lax.dot_general(ds, qh, tn_dims, preferred_element_type=F32)
            dkv_ref[:, c + lo:c + hi] += lax.dot_general(pb, doh, tn_dims, preferred_element_type=F32)

    full = lambda shp: pl.BlockSpec(shp, lambda i: (0,) * len(shp))
    n_in_before_dproj = 21
    return pl.pallas_call(
        body, name="branch_bwd", grid=(nt,),
        in_specs=[col(0), col(1), col(2), col(3), col(4), col(5), prev(0, HALO_A), prev(1, HALO_A),
                  prev(3, HALO_B), prev(4, HALO_B), nxt(HALO_B, ci=2),
                  pl.BlockSpec((ts, c), lambda i: (i, 0)), nxt(HALO_A),
                  pl.BlockSpec((N_BRANCH, ts, c), lambda i: (0, i, 0)), nxt(HALO_A, lead=0), nxt(HALO_B, lead=1),
                  full((mlen, c2)), full(conv_a_w.shape), full((1, c)), full((1, c)), full(conv_b_w.shape),
                  pl.BlockSpec(memory_space=pl.ANY)],
        out_specs=[pl.BlockSpec((ts, 6 * c), lambda i: (i, 0)), full((mlen, c2)), full((HALO_A, c)), full((8, c))],
        out_shape=[jax.ShapeDtypeStruct(dproj.shape, BF16), jax.ShapeDtypeStruct((mlen, c2), F32),
                   jax.ShapeDtypeStruct((HALO_A, c), F32), jax.ShapeDtypeStruct((8, c), F32)],
        scratch_shapes=[pltpu.VMEM((HALO_A + ts, c), F32), pltpu.VMEM((ts + HALO_A, c), F32),
                        pltpu.VMEM((HALO_B + ts, c), F32), pltpu.VMEM((ts + HALO_B, c), F32)],
        input_output_aliases={n_in_before_dproj: 0},
        compiler_params=_params(),
    )(proj, proj, proj, proj, proj, proj, proj, proj, proj, proj, proj, a1, a1, dabo, dabo, dabo, kv, conv_a_w,
      ln_g, ln_b, conv_b_w, dproj)


def _merge_fwd(abo, w_cat, proj, d):
    nb, s, c = abo.shape
    bm, bn = _tile(s, TILE_M), _tile(d, TILE_N // 2)
    gate_col0 = (proj.shape[1] - nb * d) // bn

    def body(a_ref, w_ref, g_ref, y_ref, m_ref, acc_ref):
        k = pl.program_id(2)
        y = jnp.dot(a_ref[...], w_ref[...], preferred_element_type=F32)
        y_ref[...] = y.astype(BF16)
        contrib = jax.nn.sigmoid(g_ref[...]) * y

        @pl.when(k == 0)
        def _():
            acc_ref[...] = contrib

        @pl.when(k > 0)
        def _():
            acc_ref[...] += contrib

        @pl.when(k == nb - 1)
        def _():
            m_ref[...] = acc_ref[...].astype(BF16)

    return pl.pallas_call(
        body, name="merge_fwd", grid=(s // bm, d // bn, nb),
        in_specs=[pl.BlockSpec((None, bm, c), lambda i, j, k: (k, i, 0)),
                  pl.BlockSpec((None, c, bn), lambda i, j, k: (k, 0, j)),
                  pl.BlockSpec((bm, bn), lambda i, j, k: (i, gate_col0 + k * (d // bn) + j))],
        out_specs=[pl.BlockSpec((None, bm, bn), lambda i, j, k: (k, i, j)),
                   pl.BlockSpec((bm, bn), lambda i, j, k: (i, j))],
        out_shape=[jax.ShapeDtypeStruct((nb, s, d), BF16), jax.ShapeDtypeStruct((s, d), BF16)],
        scratch_shapes=[pltpu.VMEM((bm, bn), F32)],
        compiler_params=_params(),
    )(abo, w_cat, proj)


def _merge_bwd(dz, w_o, y, proj):
    s, d = dz.shape
    nb = y.shape[0]
    nin = proj.shape[1]
    bm, bn = _tile(s, TILE_M), _tile(d, TILE_N // 2)
    gate_col0 = (nin - nb * d) // bn

    def body(dz_ref, w_ref, y_ref, g_ref, dy_ref, dg_ref, acc_ref):
        @pl.when(pl.program_id(2) == 0)
        def _():
            acc_ref[...] = lax.dot_general(dz_ref[...], w_ref[...], (((1,), (1,)), ((), ())),
                                           preferred_element_type=F32)

        dm = acc_ref[...]
        gt = jax.nn.sigmoid(g_ref[...])
        dy_ref[...] = (dm * gt).astype(BF16)
        dg_ref[...] = (dm * y_ref[...].astype(F32) * gt * (1.0 - gt)).astype(BF16)

    gate = lambda i, j, k: (i, gate_col0 + k * (d // bn) + j)
    return pl.pallas_call(
        body, name="merge_bwd", grid=(s // bm, d // bn, nb),
        in_specs=[pl.BlockSpec((bm, d), lambda i, j, k: (i, 0)), pl.BlockSpec((bn, d), lambda i, j, k: (j, 0)),
                  pl.BlockSpec((None, bm, bn), lambda i, j, k: (k, i, j)), pl.BlockSpec((bm, bn), gate)],
        out_specs=[pl.BlockSpec((None, bm, bn), lambda i, j, k: (k, i, j)), pl.BlockSpec((bm, bn), gate)],
        out_shape=[jax.ShapeDtypeStruct((nb, s, d), BF16), jax.ShapeDtypeStruct((s, nin), BF16)],
        scratch_shapes=[pltpu.VMEM((bm, bn), F32)],
        compiler_params=_params(),
    )(dz, w_o, y, proj)


def _window(ref, axis, who, length):
    idx = [slice(None)] * len(ref.shape)
    idx[axis] = pl.ds(pl.multiple_of(who * length, length), length)
    return ref.at[tuple(idx)]


def _all_gather(name, shards, axes):
    n = len(shards)
    out_shapes = []
    for sh, ax in zip(shards, axes):
        shp = list(sh.shape)
        shp[ax] *= N_DEV
        out_shapes.append(jax.ShapeDtypeStruct(tuple(shp), sh.dtype))

    def body(*refs):
        ins, outs = refs[:n], refs[n:2 * n]
        send, recv, local = refs[2 * n:]
        me = _my_index()
        lens = [ins[a].shape[axes[a]] for a in range(n)]
        mine = [pltpu.make_async_copy(ins[a], _window(outs[a], axes[a], me, lens[a]), local.at[a]) for a in range(n)]
        for cp in mine:
            cp.start()
        pushes = []
        for d in range(1, N_DEV):
            to = (me + d) % N_DEV
            for a in range(n):
                cp = pltpu.make_async_remote_copy(
                    src_ref=ins[a], dst_ref=_window(outs[a], axes[a], me, lens[a]), send_sem=send.at[a, d],
                    recv_sem=recv.at[a, d], device_id=_mesh_id(to), device_id_type=pl.DeviceIdType.MESH)
                cp.start()
                pushes.append(cp)
        for d in range(1, N_DEV):
            frm = (me + N_DEV - d) % N_DEV
            for a in range(n):
                pltpu.make_async_remote_copy(
                    src_ref=ins[a], dst_ref=_window(outs[a], axes[a], frm, lens[a]), send_sem=send.at[a, d],
                    recv_sem=recv.at[a, d], device_id=_mesh_id(frm), device_id_type=pl.DeviceIdType.MESH).wait_recv()
        for cp in pushes:
            cp.wait_send()
        for cp in mine:
            cp.wait()

    hbm = pl.BlockSpec(memory_space=pl.ANY)
    return pl.pallas_call(
        body, name=name, in_specs=[hbm] * n, out_specs=[hbm] * n, out_shape=out_shapes,
        scratch_shapes=[pltpu.SemaphoreType.DMA((n, N_DEV)), pltpu.SemaphoreType.DMA((n, N_DEV)),
                        pltpu.SemaphoreType.DMA((n,))],
    )(*shards)


def _reduce_scatter_slots(name, fulls, axes):
    n = len(fulls)
    lens = [f.shape[ax] // N_DEV for f, ax in zip(fulls, axes)]
    out_shapes = []
    for f, ax, ln in zip(fulls, axes, lens):
        shp = list(f.shape)
        shp[ax] = ln
        out_shapes.append(jax.ShapeDtypeStruct((N_DEV, *shp), f.dtype))

    def body(*refs):
        ins, outs = refs[:n], refs[n:2 * n]
        send, recv, local = refs[2 * n:]
        me = _my_index()
        mine = [pltpu.make_async_copy(_window(ins[a], axes[a], me, lens[a]), outs[a].at[me], local.at[a])
                for a in range(n)]
        for cp in mine:
            cp.start()
        pushes = []
        for d in range(1, N_DEV):
            to = (me + d) % N_DEV
            for a in range(n):
                cp = pltpu.make_async_remote_copy(
                    src_ref=_window(ins[a], axes[a], to, lens[a]), dst_ref=outs[a].at[me], send_sem=send.at[a, d],
                    recv_sem=recv.at[a, d], device_id=_mesh_id(to), device_id_type=pl.DeviceIdType.MESH)
                cp.start()
                pushes.append(cp)
        for d in range(1, N_DEV):
            frm = (me + N_DEV - d) % N_DEV
            for a in range(n):
                pltpu.make_async_remote_copy(
                    src_ref=_window(ins[a], axes[a], me, lens[a]), dst_ref=outs[a].at[frm], send_sem=send.at[a, d],
                    recv_sem=recv.at[a, d], device_id=_mesh_id(frm), device_id_type=pl.DeviceIdType.MESH).wait_recv()
        for cp in pushes:
            cp.wait_send()
        for cp in mine:
            cp.wait()

    hbm = pl.BlockSpec(memory_space=pl.ANY)
    return pl.pallas_call(
        body, name=name, in_specs=[hbm] * n, out_specs=[hbm] * n, out_shape=out_shapes,
        scratch_shapes=[pltpu.SemaphoreType.DMA((n, N_DEV)), pltpu.SemaphoreType.DMA((n, N_DEV)),
                        pltpu.SemaphoreType.DMA((n,))],
    )(*fulls)


def _all_reduce_small(packed):
    r, c = packed.shape

    def body(p_ref, o_ref, buf, send, recv):
        me = _my_index()
        buf[me] = p_ref[...]
        pushes = []
        for d in range(1, N_DEV):
            to = (me + d) % N_DEV
            cp = pltpu.make_async_remote_copy(src_ref=p_ref, dst_ref=buf.at[me], send_sem=send.at[d],
                                              recv_sem=recv.at[d], device_id=_mesh_id(to),
                                              device_id_type=pl.DeviceIdType.MESH)
            cp.start()
            pushes.append(cp)
        for d in range(1, N_DEV):
            frm = (me + N_DEV - d) % N_DEV
            pltpu.make_async_remote_copy(src_ref=p_ref, dst_ref=buf.at[frm], send_sem=send.at[d], recv_sem=recv.at[d],
                                         device_id=_mesh_id(frm), device_id_type=pl.DeviceIdType.MESH).wait_recv()
        for cp in pushes:
            cp.wait_send()
        acc = buf[0]
        for s in range(1, N_DEV):
            acc = acc + buf[s]
        o_ref[...] = acc

    vmem = pl.BlockSpec(memory_space=pltpu.VMEM)
    return pl.pallas_call(
        body, name="all_reduce_small", in_specs=[vmem], out_specs=vmem, out_shape=jax.ShapeDtypeStruct((r, c), F32),
        scratch_shapes=[pltpu.VMEM((N_DEV, r, c), F32), pltpu.SemaphoreType.DMA((N_DEV,)),
                        pltpu.SemaphoreType.DMA((N_DEV,))],
        compiler_params=_params(),
    )(packed)


def _layer_fwd(x, mem, sm, wf):
    d = x.shape[1]
    h = _rmsnorm("rms_mix_pre", x, sm["g_mix_pre"])
    proj = _mm_nn("proj", h, wf["w_in"], F32)
    mem_n = _rmsnorm("rms_mem", mem, sm["g_mem"])
    kv = _mm_nn("kv", mem_n, wf["w_kv"], BF16)
    a1, abo = _branch_fwd(proj, kv, wf["conv_a_w"], sm["conv_a_b"], sm["ln_a_g"], sm["ln_a_b"], wf["conv_b_w"])
    y, merged = _merge_fwd(abo, wf["w_cat"], proj, d)
    z, x1 = _mm_res_norm("mix_out", merged, wf["w_o"], x, sm["g_mix_post"])
    h2 = _rmsnorm("rms_mlp_pre", x1, sm["g_mlp_pre"])
    r, act = _mm_relu2("mlp_up", h2, wf["w_up"])
    f, x2 = _mm_res_norm("mlp_down", act, wf["w_down"], x1, sm["g_mlp_post"])
    saved = dict(x=x, h=h, proj=proj, mem_n=mem_n, kv=kv, a1=a1, abo=abo, y=y, merged=merged, z=z, x1=x1, h2=h2,
                 r=r, act=act, f=f)
    return x2, saved


def _layer_bwd(dx2, mem, sm, wf, sv):
    c = sv["a1"].shape[1]
    df, dg_mlp_post = _rms_bwd("rms_bwd_post", dx2, sv["f"], sm["g_mlp_post"], BF16)
    d_up = _mm_relu2_bwd("mlp_down_bwd", df, wf["w_down"], sv["r"])
    gw_down = _mm_tn("gw_down", sv["act"], df, BF16)
    dx1, dg_mlp_pre = _mm_rms_bwd("mlp_up_bwd", d_up, wf["w_up"], sv["x1"], sm["g_mlp_pre"], dx2)
    gw_up = _mm_tn("gw_up", sv["h2"], d_up, BF16)
    dz, dg_mix_post = _rms_bwd("rms_bwd_post", dx1, sv["z"], sm["g_mix_post"], BF16)
    dy, dproj = _merge_bwd(dz, wf["w_o"], sv["y"], sv["proj"])
    gw_o = _mm_tn("gw_o", sv["merged"], dz, BF16)
    dabo = jnp.stack([_mm_nt("branch_out_bwd", dy, wf["w_cat"], F32, a_lead=b, b_lead=b) for b in range(N_BRANCH)])
    gw_cat = jnp.stack([_mm_tn("gw_branch_out", sv["abo"], dy, BF16, a_lead=b, b_lead=b) for b in range(N_BRANCH)])
    dproj, dkv, dconv_a_w, misc = _branch_bwd(sv["proj"], sv["a1"], dabo, sv["kv"], wf["conv_a_w"], sm["ln_a_g"],
                                              sm["ln_a_b"], wf["conv_b_w"], dproj)
    dkv = dkv.astype(BF16)
    gw_kv = _mm_tn("gw_kv", sv["mem_n"], dkv, BF16)
    dmem_n = _mm_nt("kv_bwd", dkv, wf["w_kv"], F32)
    _, dg_mem = _rms_bwd("rms_bwd_mem", dmem_n, mem, sm["g_mem"], BF16)
    dx, dg_mix_pre = _mm_rms_bwd("proj_bwd", dproj, wf["w_in"], sv["x"], sm["g_mix_pre"], dx1)
    gw_in = _mm_tn("gw_in", sv["h"], dproj, BF16)
    big = dict(w_in=gw_in, w_cat=gw_cat, w_kv=gw_kv, w_o=gw_o, w_up=gw_up, w_down=gw_down)
    rows = [dg_mix_pre.reshape(2, c), dg_mem.reshape(2, c), dg_mix_post.reshape(2, c), dg_mlp_pre.reshape(2, c),
            dg_mlp_post.reshape(2, c), misc[0:3], dconv_a_w, misc[3:6]]
    return dx, big, jnp.concatenate(rows, axis=0)


_SMALL_ROWS = 5 * 2 + 3 + HALO_A + CONV_B_K


def kernel(x, mem, g_mix_pre, w_in, conv_a_w, conv_a_b, ln_a_g, ln_a_b, w_a_out, conv_b_w, w_b_out, g_mem, w_kv, w_x_out, w_o, g_mix_post, g_mlp_pre, w_up, w_down, g_mlp_post, loss_target, m_g_mix_pre, m_w_in, m_conv_a_w, m_conv_a_b, m_ln_a_g, m_ln_a_b, m_w_a_out, m_conv_b_w, m_w_b_out, m_g_mem, m_w_kv, m_w_x_out, m_w_o, m_g_mix_post, m_g_mlp_pre, m_w_up, m_w_down, m_g_mlp_post, v_g_mix_pre, v_w_in, v_conv_a_w, v_conv_a_b, v_ln_a_g, v_ln_a_b, v_w_a_out, v_conv_b_w, v_w_b_out, v_g_mem, v_w_kv, v_w_x_out, v_w_o, v_g_mix_post, v_g_mlp_pre, v_w_up, v_w_down, v_g_mlp_post):
    names = ["g_mix_pre", "w_in", "conv_a_w", "conv_a_b", "ln_a_g", "ln_a_b", "w_a_out", "conv_b_w", "w_b_out",
             "g_mem", "w_kv", "w_x_out", "w_o", "g_mix_post", "g_mlp_pre", "w_up", "w_down", "g_mlp_post"]
    w = dict(zip(names, [g_mix_pre, w_in, conv_a_w, conv_a_b, ln_a_g, ln_a_b, w_a_out, conv_b_w, w_b_out, g_mem,
                         w_kv, w_x_out, w_o, g_mix_post, g_mlp_pre, w_up, w_down, g_mlp_post]))
    mo = dict(zip(names, [m_g_mix_pre, m_w_in, m_conv_a_w, m_conv_a_b, m_ln_a_g, m_ln_a_b, m_w_a_out, m_conv_b_w,
                          m_w_b_out, m_g_mem, m_w_kv, m_w_x_out, m_w_o, m_g_mix_post, m_g_mlp_pre, m_w_up, m_w_down,
                          m_g_mlp_post]))
    vo = dict(zip(names, [v_g_mix_pre, v_w_in, v_conv_a_w, v_conv_a_b, v_ln_a_g, v_ln_a_b, v_w_a_out, v_conv_b_w,
                          v_w_b_out, v_g_mem, v_w_kv, v_w_x_out, v_w_o, v_g_mix_post, v_g_mlp_pre, v_w_up, v_w_down,
                          v_g_mlp_post]))
    depth = w_in.shape[0]
    c = conv_a_b.shape[1]
    cs = conv_a_w.shape[2]
    me = _my_index()
    xs, mems, tgt = x[0], mem[0], loss_target[0]

    def cat3(t, l):
        return jnp.stack([t["w_a_out"][l], t["w_b_out"][l], t["w_x_out"][l]])

    big_names = ["w_in", "w_cat", "w_kv", "w_o", "w_up", "w_down"]
    big_axes = [1, 2, 0, 0, 1, 0]
    fulls, smalls = [], []
    for l in range(depth):
        shards = [w_in[l].astype(BF16), cat3(w, l).astype(BF16), w_kv[l].astype(BF16), w_o[l].astype(BF16),
                  w_up[l].astype(BF16), w_down[l].astype(BF16),
                  jnp.pad(conv_a_w[l], ((0, HALO_A - CONV_A_K), (0, 0))),
                  jnp.pad(conv_b_w[l], ((0, HALO_B - CONV_B_K), (0, 0)))]
        got = _all_gather("gather_weights", shards, big_axes + [1, 1])
        fulls.append(dict(zip(big_names + ["conv_a_w", "conv_b_w"], got)))
        smalls.append({k: w[k][l][None, :] for k in ["g_mix_pre", "conv_a_b", "ln_a_g", "ln_a_b", "g_mem",
                                                     "g_mix_post", "g_mlp_pre", "g_mlp_post"]})

    saved = []
    xc = xs
    for l in range(depth):
        xc, sv = _layer_fwd(xc, mems, smalls[l], fulls[l])
        saved.append(sv)
    loss_part, dx = _loss_head(xc, tgt)
    loss = lax.psum(loss_part, MESH_AXES)

    small_parts = [None] * depth
    upd = {k: [None] * depth for k in names}
    for l in reversed(range(depth)):
        dx, big, small_parts[l] = _layer_bwd(dx, mems, smalls[l], fulls[l], saved[l])
        slots = _reduce_scatter_slots("scatter_grads", [big[k] for k in big_names], big_axes)
        for k, sl in zip(big_names, slots):
            if k == "w_cat":
                wl, ml, vl = cat3(w, l), cat3(mo, l), cat3(vo, l)
            else:
                wl, ml, vl = w[k][l], mo[k][l], vo[k][l]
            shp = wl.shape
            two_d = (-1, shp[-1])
            res = _adamw("adamw_" + k, sl.reshape(N_DEV, *two_d), wl.reshape(two_d), ml.reshape(two_d),
                         vl.reshape(two_d))
            res = [t.reshape(shp) for t in res]
            if k == "w_cat":
                for b, nm in enumerate(["w_a_out", "w_b_out", "w_x_out"]):
                    upd[nm][l] = [t[b] for t in res]
            else:
                upd[k][l] = res

    tot = _all_reduce_small(jnp.concatenate(small_parts, axis=0)).reshape(depth, _SMALL_ROWS, c)
    d_names = ["g_mix_pre", "g_mem", "g_mix_post", "g_mlp_pre", "g_mlp_post"]
    c_names = ["conv_a_b", "ln_a_g", "ln_a_b"]
    g_small = {}
    for n_, nm in enumerate(d_names):
        g_small[nm] = tot[:, 2 * n_:2 * n_ + 2, :].reshape(depth, 2 * c)
    for n_, nm in enumerate(c_names):
        g_small[nm] = tot[:, 10 + n_, :]
    g_conv_a = lax.dynamic_slice_in_dim(tot[:, 13:13 + CONV_A_K, :], me * cs, cs, axis=2)
    g_conv_b = lax.dynamic_slice_in_dim(tot[:, 13 + HALO_A:13 + HALO_A + CONV_B_K, :], me * cs, cs, axis=2)

    def pack(t):
        rows = [t[nm].reshape(-1, c) for nm in d_names + c_names]
        flat = jnp.concatenate(rows, axis=0)
        return jnp.pad(flat, ((0, (-flat.shape[0]) % 8), (0, 0)))

    res = _adamw("adamw_small", pack(g_small)[None], pack(w), pack(mo), pack(vo))
    row = 0
    for nm in d_names + c_names:
        n_rows = w[nm].size // c
        upd[nm] = [t[row:row + n_rows].reshape(w[nm].shape) for t in res]
        row += n_rows
    for nm, g in (("conv_a_w", g_conv_a), ("conv_b_w", g_conv_b)):
        shp = w[nm].shape
        res = _adamw("adamw_" + nm, g.reshape(1, -1, cs), w[nm].reshape(-1, cs), mo[nm].reshape(-1, cs),
                     vo[nm].reshape(-1, cs))
        upd[nm] = [t.reshape(shp) for t in res]

    def out(nm, which):
        u = upd[nm]
        if isinstance(u[0], list):
            return jnp.stack([u[l][which] for l in range(depth)])
        return u[which]

    return (loss, dx[None], *[out(nm, 0) for nm in names], *[out(nm, 1) for nm in names],
            *[out(nm, 2) for nm in names], *[out(nm, 3) for nm in names])
```

```python
import math

import jax
import jax.numpy as jnp
from jax import lax
from jax.experimental import pallas as pl
from jax.experimental.pallas import tpu as pltpu

F32 = jnp.float32
BF16 = jnp.bfloat16

EPS = 1e-6
N_HEADS = 4
N_BRANCH = 3
CONV_A_K = 31
CONV_B_K = 3
HALO_A = 32
HALO_B = 8
N_DEV = 8
MESH_AXES = ("x", "y", "c")

ADAM_LR = 0.001
ADAM_B1 = 0.9
ADAM_B2 = 0.999
ADAM_EPS = 1e-08
ADAM_WD = 0.01
ADAM_STEP = 10

V7X_VMEM_BYTES = 64 * 1024 * 1024
VMEM_LIMIT = V7X_VMEM_BYTES - 8 * 1024 * 1024

TILE_M = 1024
TILE_N = 1024
TILE_K = 2048
TILE_ROWS_WIDE = 512
TILE_ROWS_BRANCH = 256
TILE_ROWS_BRANCH_BWD = 128


def _params():
    return pltpu.CompilerParams(vmem_limit_bytes=VMEM_LIMIT)


def _tile(n, t):
    t = min(n, t)
    assert n % t == 0, (n, t)
    return t


def _my_index():
    return 4 * lax.axis_index("x") + 2 * lax.axis_index("y") + lax.axis_index("c")


def _mesh_id(p):
    return (p // 4, (p // 2) % 2, p % 2)


def _mm(name, grid, a, a_spec, b, b_spec, *, ta, tb, acc_shape, extras, outs, epilogue):
    nk = grid[-1]
    lead = len(grid) - 3
    ne, no = len(extras), len(outs)
    dn = (((0,) if ta else (1,), (1,) if tb else (0,)), ((), ()))

    def body(a_ref, b_ref, *rest):
        extra_refs = rest[:ne]
        out_refs = rest[ne:ne + no]
        i, j, k = pl.program_id(lead), pl.program_id(lead + 1), pl.program_id(lead + 2)
        prod = lax.dot_general(a_ref[...], b_ref[...], dn, preferred_element_type=F32)
        if nk == 1:
            epilogue(prod, extra_refs, out_refs, i, j)
        else:
            acc_ref = rest[ne + no]

            @pl.when(k == 0)
            def _():
                acc_ref[...] = prod

            @pl.when(k > 0)
            def _():
                acc_ref[...] += prod

            @pl.when(k == nk - 1)
            def _():
                epilogue(acc_ref[...], extra_refs, out_refs, i, j)

    return pl.pallas_call(
        body, name=name, grid=grid,
        in_specs=[a_spec, b_spec] + [s for _, s in extras],
        out_specs=[s for _, s in outs],
        out_shape=[o for o, _ in outs],
        scratch_shapes=[pltpu.VMEM(acc_shape, F32)] if nk > 1 else [],
        compiler_params=_params(),
    )(a, b, *[e for e, _ in extras])


def _store_epilogue(res, extra_refs, out_refs, i, j):
    out_refs[0][...] = res.astype(out_refs[0].dtype)


def _mm_nn(name, a, b, out_dtype):
    m, kd = a.shape
    n = b.shape[1]
    bm, bn = _tile(m, TILE_M), _tile(n, TILE_N)
    return _mm(name, (m // bm, n // bn, 1), a, pl.BlockSpec((bm, kd), lambda i, j, k: (i, 0)),
               b, pl.BlockSpec((kd, bn), lambda i, j, k: (0, j)), ta=False, tb=False, acc_shape=(bm, bn), extras=[],
               outs=[(jax.ShapeDtypeStruct((m, n), out_dtype), pl.BlockSpec((bm, bn), lambda i, j, k: (i, j)))],
               epilogue=_store_epilogue)[0]


def _mm_nt(name, a, b, out_dtype):
    m, kd = a.shape[-2:]
    n = b.shape[-2]
    bm, bn = _tile(m, TILE_M), _tile(n, TILE_N)
    if a.ndim == 2:
        return _mm(name, (m // bm, n // bn, 1), a, pl.BlockSpec((bm, kd), lambda i, j, k: (i, 0)),
                   b, pl.BlockSpec((bn, kd), lambda i, j, k: (j, 0)), ta=False, tb=True, acc_shape=(bm, bn), extras=[],
                   outs=[(jax.ShapeDtypeStruct((m, n), out_dtype), pl.BlockSpec((bm, bn), lambda i, j, k: (i, j)))],
                   epilogue=_store_epilogue)[0]
    nb = a.shape[0]
    return _mm(name, (nb, m // bm, n // bn, 1), a, pl.BlockSpec((None, bm, kd), lambda s, i, j, k: (s, i, 0)),
               b, pl.BlockSpec((None, bn, kd), lambda s, i, j, k: (s, j, 0)), ta=False, tb=True, acc_shape=(bm, bn),
               extras=[],
               outs=[(jax.ShapeDtypeStruct((nb, m, n), out_dtype),
                      pl.BlockSpec((None, bm, bn), lambda s, i, j, k: (s, i, j)))],
               epilogue=_store_epilogue)[0]


def _mm_tn(name, a, b, out_dtype):
    r, m = a.shape[-2:]
    n = b.shape[-1]
    bm, bn, bk = _tile(m, TILE_M), _tile(n, TILE_N), _tile(r, TILE_K)
    if a.ndim == 2:
        return _mm(name, (m // bm, n // bn, r // bk), a, pl.BlockSpec((bk, bm), lambda i, j, k: (k, i)),
                   b, pl.BlockSpec((bk, bn), lambda i, j, k: (k, j)), ta=True, tb=False, acc_shape=(bm, bn), extras=[],
                   outs=[(jax.ShapeDtypeStruct((m, n), out_dtype), pl.BlockSpec((bm, bn), lambda i, j, k: (i, j)))],
                   epilogue=_store_epilogue)[0]
    nb = a.shape[0]
    return _mm(name, (nb, m // bm, n // bn, r // bk), a, pl.BlockSpec((None, bk, bm), lambda s, i, j, k: (s, k, i)),
               b, pl.BlockSpec((None, bk, bn), lambda s, i, j, k: (s, k, j)), ta=True, tb=False, acc_shape=(bm, bn),
               extras=[],
               outs=[(jax.ShapeDtypeStruct((nb, m, n), out_dtype),
                      pl.BlockSpec((None, bm, bn), lambda s, i, j, k: (s, i, j)))],
               epilogue=_store_epilogue)[0]


def _mm_relu2(name, h, w):
    m, kd = h.shape
    n = w.shape[1]
    bm, bn = _tile(m, TILE_M), _tile(n, TILE_N)

    def epilogue(res, extra_refs, out_refs, i, j):
        r = jnp.maximum(res, 0.0)
        out_refs[0][...] = r.astype(BF16)
        out_refs[1][...] = (r * r).astype(BF16)

    o = jax.ShapeDtypeStruct((m, n), BF16)
    spec = pl.BlockSpec((bm, bn), lambda i, j, k: (i, j))
    return _mm(name, (m // bm, n // bn, 1), h, pl.BlockSpec((bm, kd), lambda i, j, k: (i, 0)),
               w, pl.BlockSpec((kd, bn), lambda i, j, k: (0, j)), ta=False, tb=False, acc_shape=(bm, bn), extras=[],
               outs=[(o, spec), (o, spec)], epilogue=epilogue)


def _mm_relu2_bwd(name, df, w_down, r):
    m, kd = df.shape
    n = w_down.shape[0]
    bm, bn = _tile(m, TILE_M), _tile(n, TILE_N)

    def epilogue(res, extra_refs, out_refs, i, j):
        out_refs[0][...] = (res * (2.0 * extra_refs[0][...].astype(F32))).astype(BF16)

    spec = pl.BlockSpec((bm, bn), lambda i, j, k: (i, j))
    return _mm(name, (m // bm, n // bn, 1), df, pl.BlockSpec((bm, kd), lambda i, j, k: (i, 0)),
               w_down, pl.BlockSpec((bn, kd), lambda i, j, k: (j, 0)), ta=False, tb=True, acc_shape=(bm, bn),
               extras=[(r, spec)], outs=[(jax.ShapeDtypeStruct((m, n), BF16), spec)], epilogue=epilogue)[0]


def _mm_res_norm(name, a, w, xres, g):
    m, kd = a.shape
    d = w.shape[1]
    bm, bk = _tile(m, TILE_ROWS_WIDE), _tile(kd, TILE_M)

    def epilogue(res, extra_refs, out_refs, i, j):
        x_ref, g_ref = extra_refs
        rr = lax.rsqrt(jnp.mean(res * res, axis=-1, keepdims=True) + EPS)
        out_refs[0][...] = res
        out_refs[1][...] = x_ref[...] + (res * rr) * g_ref[...]

    row = pl.BlockSpec((bm, d), lambda i, j, k: (i, 0))
    o = jax.ShapeDtypeStruct((m, d), F32)
    return _mm(name, (m // bm, 1, kd // bk), a, pl.BlockSpec((bm, bk), lambda i, j, k: (i, k)),
               w, pl.BlockSpec((bk, d), lambda i, j, k: (k, 0)), ta=False, tb=False, acc_shape=(bm, d),
               extras=[(xres, row), (g, pl.BlockSpec((1, d), lambda i, j, k: (0, 0)))],
               outs=[(o, row), (o, row)], epilogue=epilogue)


def _rms_bwd_math(dy, x, g):
    rr = lax.rsqrt(jnp.mean(x * x, axis=-1, keepdims=True) + EPS)
    gy = dy * g
    dx = rr * gy - x * (rr * rr * rr * jnp.mean(x * gy, axis=-1, keepdims=True))
    dg_rows = dy * x * rr
    return dx, dg_rows


def _mm_rms_bwd(name, a, w, x, g, dres):
    m, kd = a.shape
    d = w.shape[0]
    bm, bk = _tile(m, TILE_ROWS_WIDE), _tile(kd, TILE_M)

    def epilogue(res, extra_refs, out_refs, i, j):
        x_ref, g_ref, dres_ref = extra_refs
        dx, dg_rows = _rms_bwd_math(res, x_ref[...], g_ref[...])
        out_refs[0][...] = dres_ref[...] + dx

        @pl.when(i == 0)
        def _():
            out_refs[1][...] = jnp.zeros_like(out_refs[1])

        out_refs[1][...] += jnp.sum(dg_rows, axis=0, keepdims=True)

    row = pl.BlockSpec((bm, d), lambda i, j, k: (i, 0))
    vec = pl.BlockSpec((1, d), lambda i, j, k: (0, 0))
    return _mm(name, (m // bm, 1, kd // bk), a, pl.BlockSpec((bm, bk), lambda i, j, k: (i, k)),
               w, pl.BlockSpec((d, bk), lambda i, j, k: (0, k)), ta=False, tb=True, acc_shape=(bm, d),
               extras=[(x, row), (g, vec), (dres, row)],
               outs=[(jax.ShapeDtypeStruct((m, d), F32), row), (jax.ShapeDtypeStruct((1, d), F32), vec)],
               epilogue=epilogue)


_UNTOUCHED = pl.BlockSpec(memory_space=pl.ANY)


def _rmsnorm(name, x, g, after):
    r, d = x.shape
    br = _tile(r, TILE_ROWS_WIDE)

    def body(x_ref, g_ref, after_ref, o_ref):
        xv = x_ref[...]
        rr = lax.rsqrt(jnp.mean(xv * xv, axis=-1, keepdims=True) + EPS)
        o_ref[...] = ((xv * rr) * g_ref[...]).astype(BF16)

    row = pl.BlockSpec((br, d), lambda i: (i, 0))
    return pl.pallas_call(body, name=name, grid=(r // br,),
                          in_specs=[row, pl.BlockSpec((1, d), lambda i: (0, 0)), _UNTOUCHED],
                          out_specs=row, out_shape=jax.ShapeDtypeStruct((r, d), BF16),
                          compiler_params=_params())(x, g, after)


def _rms_bwd(name, dy, x, g, out_dtype, after):
    r, d = x.shape
    br = _tile(r, TILE_ROWS_WIDE)

    def body(dy_ref, x_ref, g_ref, after_ref, dx_ref, dg_ref):
        dx, dg_rows = _rms_bwd_math(dy_ref[...], x_ref[...], g_ref[...])
        dx_ref[...] = dx.astype(out_dtype)

        @pl.when(pl.program_id(0) == 0)
        def _():
            dg_ref[...] = jnp.zeros_like(dg_ref)

        dg_ref[...] += jnp.sum(dg_rows, axis=0, keepdims=True)

    row = pl.BlockSpec((br, d), lambda i: (i, 0))
    vec = pl.BlockSpec((1, d), lambda i: (0, 0))
    return pl.pallas_call(body, name=name, grid=(r // br,), in_specs=[row, row, vec, _UNTOUCHED],
                          out_specs=[row, vec],
                          out_shape=[jax.ShapeDtypeStruct((r, d), out_dtype), jax.ShapeDtypeStruct((1, d), F32)],
                          compiler_params=_params())(dy, x, g, after)


def _loss_head(y, target):
    r, d = y.shape
    br = _tile(r, TILE_ROWS_WIDE)

    def body(y_ref, t_ref, dy_ref, l_ref):
        diff = y_ref[...] - t_ref[...]
        dy_ref[...] = diff * (1.0 / d)

        @pl.when(pl.program_id(0) == 0)
        def _():
            l_ref[...] = jnp.zeros_like(l_ref)

        l_ref[...] += 0.5 * jnp.sum(jnp.mean(diff * diff, axis=-1, keepdims=True))

    row = pl.BlockSpec((br, d), lambda i: (i, 0))
    one = pl.BlockSpec((8, 128), lambda i: (0, 0))
    dy, l = pl.pallas_call(body, name="loss_head", grid=(r // br,), in_specs=[row, row], out_specs=[row, one],
                           out_shape=[jax.ShapeDtypeStruct((r, d), F32), jax.ShapeDtypeStruct((8, 128), F32)],
                           compiler_params=_params())(y, target)
    return l[0, 0], dy


def _adamw(name, parts, branch, w, m, v, layer, prev, after):
    _, r, c = w.shape
    p = parts.shape[0]
    br = r
    for cand in (512, 256, 128, 64, 32, 16):
        if r % cand == 0 and cand * c * 4 <= 2 * 1024 * 1024:
            br = cand
            break
    bc1 = 1.0 - ADAM_B1 ** ADAM_STEP
    bc2 = 1.0 - ADAM_B2 ** ADAM_STEP
    if prev is None:
        prev = [lax.empty(w.shape, F32) for _ in range(4)]

    def body(p_ref, w_ref, m_ref, v_ref, after_ref, pg, pd, pm, pv, g_out, d_out, m_out, v_out):
        g = p_ref[0].astype(F32)
        for s in range(1, p):
            g = g + p_ref[s].astype(F32)
        m2 = ADAM_B1 * m_ref[...] + (1.0 - ADAM_B1) * g
        v2 = ADAM_B2 * v_ref[...] + (1.0 - ADAM_B2) * (g * g)
        m_hat = m2 / bc1
        v_hat = v2 / bc2
        g_out[...] = g
        d_out[...] = -ADAM_LR * (m_hat / (jnp.sqrt(v_hat) + ADAM_EPS) + ADAM_WD * w_ref[...])
        m_out[...] = m2
        v_out[...] = v2

    if branch is None:
        p_spec = pl.BlockSpec((p, br, c), lambda i: (0, i, 0))
    else:
        p_spec = pl.BlockSpec((p, None, br, c), lambda i: (0, branch, i, 0))
    slab = pl.BlockSpec((None, br, c), lambda i: (layer, i, 0))
    o = jax.ShapeDtypeStruct(w.shape, F32)
    return pl.pallas_call(body, name=name, grid=(r // br,),
                          in_specs=[p_spec, slab, slab, slab, _UNTOUCHED] + [_UNTOUCHED] * 4,
                          out_specs=[slab] * 4, out_shape=[o] * 4, input_output_aliases={5: 0, 6: 1, 7: 2, 8: 3},
                          compiler_params=_params())(parts, w, m, v, after, *prev)


def _layer_norm_parts(a1, ln_g, ln_b):
    mu = jnp.mean(a1, axis=-1, keepdims=True)
    xc = a1 - mu
    rstd = lax.rsqrt(jnp.mean(xc * xc, axis=-1, keepdims=True) + EPS)
    xhat = xc * rstd
    a2 = xhat * ln_g + ln_b
    return xhat, rstd, a2


def _softmax_rows(s):
    e = jnp.exp(s - jnp.max(s, axis=-1, keepdims=True))
    return e / jnp.sum(e, axis=-1, keepdims=True)


def _branch_specs(ts, c):
    def col(ci):
        return pl.BlockSpec((ts, c), lambda i: (i, ci))

    def prev(ci, h):
        return pl.BlockSpec((h, c), lambda i: (jnp.maximum(i * (ts // h) - 1, 0), ci))

    return col, prev


def _branch_fwd(proj, kv, conv_a_w, conv_a_b, ln_g, ln_b, conv_b_w):
    s = proj.shape[0]
    mlen, c2 = kv.shape
    c = c2 // 2
    hd = c // N_HEADS
    ts = _tile(s, TILE_ROWS_BRANCH)
    scale = hd ** -0.5
    col, prev = _branch_specs(ts, c)

    def body(av, ag, sb, sc, sx, q, hav, hag, hsc, hsx, kv_ref, caw, cab, lng, lnb, cbw, a1_ref, abo_ref, exta, extb):
        not_first = (pl.program_id(0) > 0).astype(F32)
        exta[pl.ds(0, HALO_A), :] = hav[...] * jax.nn.sigmoid(hag[...]) * not_first
        exta[pl.ds(HALO_A, ts), :] = av[...] * jax.nn.sigmoid(ag[...])
        acc = jnp.broadcast_to(cab[...], (ts, c))
        for k in range(CONV_A_K):
            acc = acc + caw[pl.ds(k, 1), :] * exta[pl.ds(HALO_A - (CONV_A_K - 1) + k, ts), :]
        a1_ref[...] = acc
        _, _, a2 = _layer_norm_parts(acc, lng[...], lnb[...])
        abo_ref[0] = (a2 * jax.nn.sigmoid(a2)).astype(BF16)
        extb[pl.ds(0, HALO_B), :] = hsc[...] * hsx[...] * not_first
        extb[pl.ds(HALO_B, ts), :] = sc[...] * sx[...]
        u = cbw[pl.ds(0, 1), :] * extb[pl.ds(HALO_B - (CONV_B_K - 1), ts), :]
        for k in range(1, CONV_B_K):
            u = u + cbw[pl.ds(k, 1), :] * extb[pl.ds(HALO_B - (CONV_B_K - 1) + k, ts), :]
        abo_ref[1] = (sb[...] * u).astype(BF16)
        for h in range(N_HEADS):
            qh = q[:, h * hd:(h + 1) * hd].astype(BF16)
            kh = kv_ref[:, h * hd:(h + 1) * hd]
            vh = kv_ref[:, c + h * hd:c + (h + 1) * hd]
            sc_ = lax.dot_general(qh, kh, (((1,), (1,)), ((), ())), preferred_element_type=F32) * scale
            p = _softmax_rows(sc_).astype(BF16)
            abo_ref[2, :, h * hd:(h + 1) * hd] = jnp.dot(p, vh, preferred_element_type=F32).astype(BF16)

    full = lambda shp: pl.BlockSpec(shp, lambda i: (0,) * len(shp))
    return pl.pallas_call(
        body, name="branch_fwd", grid=(s // ts,),
        in_specs=[col(0), col(1), col(2), col(3), col(4), col(5), prev(0, HALO_A), prev(1, HALO_A),
                  prev(3, HALO_B), prev(4, HALO_B), full((mlen, c2)), full(conv_a_w.shape), full((1, c)),
                  full((1, c)), full((1, c)), full(conv_b_w.shape)],
        out_specs=[pl.BlockSpec((ts, c), lambda i: (i, 0)), pl.BlockSpec((N_BRANCH, ts, c), lambda i: (0, i, 0))],
        out_shape=[jax.ShapeDtypeStruct((s, c), F32), jax.ShapeDtypeStruct((N_BRANCH, s, c), BF16)],
        scratch_shapes=[pltpu.VMEM((HALO_A + ts, c), F32), pltpu.VMEM((HALO_B + ts, c), F32)],
        compiler_params=_params(),
    )(proj, proj, proj, proj, proj, proj, proj, proj, proj, proj, kv, conv_a_w, conv_a_b, ln_g, ln_b, conv_b_w)


def _branch_bwd(proj, a1, dabo, kv, conv_a_w, ln_g, ln_b, conv_b_w, dproj):
    s = proj.shape[0]
    mlen, c2 = kv.shape
    c = c2 // 2
    hd = c // N_HEADS
    ts = _tile(s, TILE_ROWS_BRANCH_BWD)
    nt = s // ts
    scale = hd ** -0.5
    col, prev = _branch_specs(ts, c)

    def nxt(h, lead=None, ci=0):
        if lead is None:
            return pl.BlockSpec((h, c), lambda i: (jnp.minimum((i + 1) * (ts // h), s // h - 1), ci))
        return pl.BlockSpec((None, h, c), lambda i: (lead, jnp.minimum((i + 1) * (ts // h), s // h - 1), 0))

    def body(av, ag, sb, sc, sx, q, hav, hag, hsc, hsx, nsb, a1_ref, na1, dabo_ref, nda, ndb, kv_ref, caw, lng, lnb,
             cbw, dproj_in, dp_ref, dkv_ref, dwa_ref, misc_ref, exta, extda, extb, extdu):
        i = pl.program_id(0)
        not_first = (i > 0).astype(F32)
        not_last = (i < nt - 1).astype(F32)

        @pl.when(i == 0)
        def _():
            dkv_ref[...] = jnp.zeros_like(dkv_ref)
            dwa_ref[...] = jnp.zeros_like(dwa_ref)
            misc_ref[...] = jnp.zeros_like(misc_ref)

        def rowsum(t):
            return jnp.sum(t, axis=0, keepdims=True)

        def da1_of(a1v, da3):
            xhat, rstd, a2 = _layer_norm_parts(a1v, lng[...], lnb[...])
            sg = jax.nn.sigmoid(a2)
            da2 = da3 * (sg * (1.0 + a2 * (1.0 - sg)))
            dxh = da2 * lng[...]
            da1 = rstd * (dxh - jnp.mean(dxh, axis=-1, keepdims=True)
                          - xhat * jnp.mean(dxh * xhat, axis=-1, keepdims=True))
            return da1, da2, xhat

        da1, da2, xhat = da1_of(a1_ref[...], dabo_ref[0])
        misc_ref[pl.ds(0, 1), :] += rowsum(da1)
        misc_ref[pl.ds(1, 1), :] += rowsum(da2 * xhat)
        misc_ref[pl.ds(2, 1), :] += rowsum(da2)
        extda[pl.ds(0, ts), :] = da1
        extda[pl.ds(ts, HALO_A), :] = da1_of(na1[...], nda[...])[0] * not_last
        sga = jax.nn.sigmoid(ag[...])
        exta[pl.ds(0, HALO_A), :] = hav[...] * jax.nn.sigmoid(hag[...]) * not_first
        exta[pl.ds(HALO_A, ts), :] = av[...] * sga
        da0 = jnp.zeros((ts, c), F32)
        for k in range(CONV_A_K):
            dwa_ref[pl.ds(k, 1), :] += rowsum(da1 * exta[pl.ds(HALO_A - (CONV_A_K - 1) + k, ts), :])
            da0 = da0 + caw[pl.ds(CONV_A_K - 1 - k, 1), :] * extda[pl.ds(k, ts), :]
        dp_ref[:, 0:c] = (da0 * sga).astype(BF16)
        dp_ref[:, c:2 * c] = (da0 * av[...] * sga * (1.0 - sga)).astype(BF16)

        extb[pl.ds(0, HALO_B), :] = hsc[...] * hsx[...] * not_first
        extb[pl.ds(HALO_B, ts), :] = sc[...] * sx[...]
        dbu = dabo_ref[1]
        du = dbu * sb[...]
        extdu[pl.ds(0, ts), :] = du
        extdu[pl.ds(ts, HALO_B), :] = ndb[...] * nsb[...] * not_last
        u = jnp.zeros((ts, c), F32)
        dpr = jnp.zeros((ts, c), F32)
        for k in range(CONV_B_K):
            shifted = extb[pl.ds(HALO_B - (CONV_B_K - 1) + k, ts), :]
            u = u + cbw[pl.ds(k, 1), :] * shifted
            misc_ref[pl.ds(3 + k, 1), :] += rowsum(du * shifted)
            dpr = dpr + cbw[pl.ds(CONV_B_K - 1 - k, 1), :] * extdu[pl.ds(k, ts), :]
        dp_ref[:, 2 * c:3 * c] = (dbu * u).astype(BF16)
        dp_ref[:, 3 * c:4 * c] = (dpr * sx[...]).astype(BF16)
        dp_ref[:, 4 * c:5 * c] = (dpr * sc[...]).astype(BF16)

        nt_dims = (((1,), (1,)), ((), ()))
        tn_dims = (((0,), (0,)), ((), ()))
        for h in range(N_HEADS):
            lo, hi = h * hd, (h + 1) * hd
            qh = q[:, lo:hi].astype(BF16)
            kh = kv_ref[:, lo:hi]
            vh = kv_ref[:, c + lo:c + hi]
            p = _softmax_rows(lax.dot_general(qh, kh, nt_dims, preferred_element_type=F32) * scale)
            pb = p.astype(BF16)
            doh = dabo_ref[2, :, lo:hi].astype(BF16)
            dpm = lax.dot_general(doh, vh, nt_dims, preferred_element_type=F32)
            ds = (p * (dpm - jnp.sum(dpm * p, axis=-1, keepdims=True)) * scale).astype(BF16)
            dp_ref[:, 5 * c + lo:5 * c + hi] = jnp.dot(ds, kh, preferred_element_type=F32).astype(BF16)
            dkv_ref[:, lo:hi] += lax.dot_general(ds, qh, tn_dims, preferred_element_type=F32)
            dkv_ref[:, c + lo:c + hi] += lax.dot_general(pb, doh, tn_dims, preferred_element_type=F32)

    full = lambda shp: pl.BlockSpec(shp, lambda i: (0,) * len(shp))
    n_in_before_dproj = 21
    return pl.pallas_call(
        body, name="branch_bwd", grid=(nt,),
        in_specs=[col(0), col(1), col(2), col(3), col(4), col(5), prev(0, HALO_A), prev(1, HALO_A),
                  prev(3, HALO_B), prev(4, HALO_B), nxt(HALO_B, ci=2),
                  pl.BlockSpec((ts, c), lambda i: (i, 0)), nxt(HALO_A),
                  pl.BlockSpec((N_BRANCH, ts, c), lambda i: (0, i, 0)), nxt(HALO_A, lead=0), nxt(HALO_B, lead=1),
                  full((mlen, c2)), full(conv_a_w.shape), full((1, c)), full((1, c)), full(conv_b_w.shape),
                  pl.BlockSpec(memory_space=pl.ANY)],
        out_specs=[pl.BlockSpec((ts, 6 * c), lambda i: (i, 0)), full((mlen, c2)), full((HALO_A, c)), full((8, c))],
        out_shape=[jax.ShapeDtypeStruct(dproj.shape, BF16), jax.ShapeDtypeStruct((mlen, c2), F32),
                   jax.ShapeDtypeStruct((HALO_A, c), F32), jax.ShapeDtypeStruct((8, c), F32)],
        scratch_shapes=[pltpu.VMEM((HALO_A + ts, c), F32), pltpu.VMEM((ts + HALO_A, c), F32),
                        pltpu.VMEM((HALO_B + ts, c), F32), pltpu.VMEM((ts + HALO_B, c), F32)],
        input_output_aliases={n_in_before_dproj: 0},
        compiler_params=_params(),
    )(proj, proj, proj, proj, proj, proj, proj, proj, proj, proj, proj, a1, a1, dabo, dabo, dabo, kv, conv_a_w,
      ln_g, ln_b, conv_b_w, dproj)


def _merge_fwd(abo, w_cat, proj, d):
    nb, s, c = abo.shape
    bm, bn = _tile(s, TILE_M), _tile(d, TILE_N // 2)
    gate_col0 = (proj.shape[1] - nb * d) // bn

    def body(a_ref, w_ref, g_ref, y_ref, m_ref, acc_ref):
        k = pl.program_id(2)
        y = jnp.dot(a_ref[...], w_ref[...], preferred_element_type=F32)
        y_ref[...] = y.astype(BF16)
        contrib = jax.nn.sigmoid(g_ref[...]) * y

        @pl.when(k == 0)
        def _():
            acc_ref[...] = contrib

        @pl.when(k > 0)
        def _():
            acc_ref[...] += contrib

        @pl.when(k == nb - 1)
        def _():
            m_ref[...] = acc_ref[...].astype(BF16)

    return pl.pallas_call(
        body, name="merge_fwd", grid=(s // bm, d // bn, nb),
        in_specs=[pl.BlockSpec((None, bm, c), lambda i, j, k: (k, i, 0)),
                  pl.BlockSpec((None, c, bn), lambda i, j, k: (k, 0, j)),
                  pl.BlockSpec((bm, bn), lambda i, j, k: (i, gate_col0 + k * (d // bn) + j))],
        out_specs=[pl.BlockSpec((None, bm, bn), lambda i, j, k: (k, i, j)),
                   pl.BlockSpec((bm, bn), lambda i, j, k: (i, j))],
        out_shape=[jax.ShapeDtypeStruct((nb, s, d), BF16), jax.ShapeDtypeStruct((s, d), BF16)],
        scratch_shapes=[pltpu.VMEM((bm, bn), F32)],
        compiler_params=_params(),
    )(abo, w_cat, proj)


def _merge_bwd(dz, w_o, y, proj):
    s, d = dz.shape
    nb = y.shape[0]
    nin = proj.shape[1]
    bm, bn = _tile(s, TILE_M), _tile(d, TILE_N // 2)
    gate_col0 = (nin - nb * d) // bn

    def body(dz_ref, w_ref, y_ref, g_ref, dy_ref, dg_ref, acc_ref):
        @pl.when(pl.program_id(2) == 0)
        def _():
            acc_ref[...] = lax.dot_general(dz_ref[...], w_ref[...], (((1,), (1,)), ((), ())),
                                           preferred_element_type=F32)

        dm = acc_ref[...]
        gt = jax.nn.sigmoid(g_ref[...])
        dy_ref[...] = (dm * gt).astype(BF16)
        dg_ref[...] = (dm * y_ref[...].astype(F32) * gt * (1.0 - gt)).astype(BF16)

    gate = lambda i, j, k: (i, gate_col0 + k * (d // bn) + j)
    return pl.pallas_call(
        body, name="merge_bwd", grid=(s // bm, d // bn, nb),
        in_specs=[pl.BlockSpec((bm, d), lambda i, j, k: (i, 0)), pl.BlockSpec((bn, d), lambda i, j, k: (j, 0)),
                  pl.BlockSpec((None, bm, bn), lambda i, j, k: (k, i, j)), pl.BlockSpec((bm, bn), gate)],
        out_specs=[pl.BlockSpec((None, bm, bn), lambda i, j, k: (k, i, j)), pl.BlockSpec((bm, bn), gate)],
        out_shape=[jax.ShapeDtypeStruct((nb, s, d), BF16), jax.ShapeDtypeStruct((s, nin), BF16)],
        scratch_shapes=[pltpu.VMEM((bm, bn), F32)],
        compiler_params=_params(),
    )(dz, w_o, y, proj)


def _window(ref, axis, who, length):
    idx = [slice(None)] * len(ref.shape)
    idx[axis] = pl.ds(pl.multiple_of(who * length, length), length)
    return ref.at[tuple(idx)]


def _all_gather(name, shards, axes):
    n = len(shards)
    out_shapes = []
    for sh, ax in zip(shards, axes):
        shp = list(sh.shape)
        shp[ax] *= N_DEV
        out_shapes.append(jax.ShapeDtypeStruct(tuple(shp), sh.dtype))

    def body(*refs):
        ins, outs = refs[:n], refs[n:2 * n]
        send, recv, local = refs[2 * n:]
        me = _my_index()
        lens = [ins[a].shape[axes[a]] for a in range(n)]
        mine = [pltpu.make_async_copy(ins[a], _window(outs[a], axes[a], me, lens[a]), local.at[a]) for a in range(n)]
        for cp in mine:
            cp.start()
        pushes = []
        for d in range(1, N_DEV):
            to = (me + d) % N_DEV
            for a in range(n):
                cp = pltpu.make_async_remote_copy(
                    src_ref=ins[a], dst_ref=_window(outs[a], axes[a], me, lens[a]), send_sem=send.at[a, d],
                    recv_sem=recv.at[a, d], device_id=_mesh_id(to), device_id_type=pl.DeviceIdType.MESH)
                cp.start()
                pushes.append(cp)
        for d in range(1, N_DEV):
            frm = (me + N_DEV - d) % N_DEV
            for a in range(n):
                pltpu.make_async_remote_copy(
                    src_ref=ins[a], dst_ref=_window(outs[a], axes[a], frm, lens[a]), send_sem=send.at[a, d],
                    recv_sem=recv.at[a, d], device_id=_mesh_id(frm), device_id_type=pl.DeviceIdType.MESH).wait_recv()
        for cp in pushes:
            cp.wait_send()
        for cp in mine:
            cp.wait()

    hbm = pl.BlockSpec(memory_space=pl.ANY)
    return pl.pallas_call(
        body, name=name, in_specs=[hbm] * n, out_specs=[hbm] * n, out_shape=out_shapes,
        scratch_shapes=[pltpu.SemaphoreType.DMA((n, N_DEV)), pltpu.SemaphoreType.DMA((n, N_DEV)),
                        pltpu.SemaphoreType.DMA((n,))],
    )(*shards)


def _pair(a, d):
    return a * N_DEV + d


def _push_ends(kind, src, land, axis, length, me, to):
    if kind == "gather":
        return src, _window(land, axis, me, length)
    return _window(src, axis, to, length), land.at[me]


def _arrival_ends(kind, src, land, axis, length, me, frm):
    if kind == "gather":
        return src, _window(land, axis, frm, length)
    return _window(src, axis, me, length), land.at[frm]


def _exchange_start(name, kind, srcs, axes, after):
    n = len(srcs)
    if kind == "gather":
        lens = [s.shape[ax] for s, ax in zip(srcs, axes)]
        land_shapes = [s.shape[:ax] + (s.shape[ax] * N_DEV,) + s.shape[ax + 1:] for s, ax in zip(srcs, axes)]
    else:
        lens = [s.shape[ax] // N_DEV for s, ax in zip(srcs, axes)]
        land_shapes = [(N_DEV,) + s.shape[:ax] + (ln,) + s.shape[ax + 1:] for s, ax, ln in zip(srcs, axes, lens)]
    lands = [lax.empty(shp, s.dtype) for shp, s in zip(land_shapes, srcs)]

    def body(*refs):
        ins, lnd = refs[:n], refs[n:2 * n]
        send, recv = refs[2 * n + 1], refs[2 * n + 2]
        token = refs[-1]
        me = _my_index()
        for d in range(1, N_DEV):
            to = (me + d) % N_DEV
            for a in range(n):
                src, dst = _push_ends(kind, ins[a], lnd[a], axes[a], lens[a], me, to)
                pltpu.make_async_remote_copy(src_ref=src, dst_ref=dst, send_sem=send.at[_pair(a, d)],
                                             recv_sem=recv.at[_pair(a, d)], device_id=_mesh_id(to),
                                             device_id_type=pl.DeviceIdType.MESH).start()
        token[...] = jnp.zeros_like(token)

    hbm = pl.BlockSpec(memory_space=pltpu.HBM)
    sem = pl.BlockSpec(memory_space=pltpu.SEMAPHORE)
    held = [pltpu.with_memory_space_constraint(t, pltpu.HBM) for t in list(srcs) + lands]
    outs = pl.pallas_call(
        body, name=name,
        in_specs=[hbm] * (2 * n) + [_UNTOUCHED],
        out_specs=[sem, sem] + [hbm] * (2 * n) + [pl.BlockSpec(memory_space=pltpu.VMEM)],
        out_shape=[pltpu.SemaphoreType.DMA((n * N_DEV,)), pltpu.SemaphoreType.DMA((n * N_DEV,))]
        + [pltpu.HBM(t.shape, t.dtype) for t in held] + [jax.ShapeDtypeStruct((8, 128), F32)],
        input_output_aliases={i: 2 + i for i in range(2 * n)},
        compiler_params=pltpu.CompilerParams(has_side_effects=pltpu.SideEffectType.DATAFLOW_SIDE_EFFECTING),
    )(*held, after)
    handle = dict(kind=kind, axes=axes, lens=lens, send=outs[0], recv=outs[1], srcs=outs[2:2 + n],
                  lands=outs[2 + n:2 + 2 * n])
    return handle, outs[-1]


def _exchange_wait(name, handle, after):
    kind, axes, lens = handle["kind"], handle["axes"], handle["lens"]
    srcs, lands = handle["srcs"], handle["lands"]
    n = len(srcs)
    n_in = 2 * n + 2 + len(after)

    def body(*refs):
        ins = refs[:n]
        send, recv = refs[2 * n], refs[2 * n + 1]
        got = refs[n_in + n:n_in + 2 * n]
        local = refs[-1]
        me = _my_index()
        mine = [pltpu.make_async_copy(*_push_ends(kind, ins[a], got[a], axes[a], lens[a], me, me), local.at[a])
                for a in range(n)]
        for cp in mine:
            cp.start()
        for d in range(1, N_DEV):
            frm = (me + N_DEV - d) % N_DEV
            for a in range(n):
                src, dst = _arrival_ends(kind, ins[a], got[a], axes[a], lens[a], me, frm)
                pltpu.make_async_remote_copy(src_ref=src, dst_ref=dst, send_sem=send.at[_pair(a, d)],
                                             recv_sem=recv.at[_pair(a, d)], device_id=_mesh_id(frm),
                                             device_id_type=pl.DeviceIdType.MESH).wait_recv()
        for d in range(1, N_DEV):
            to = (me + d) % N_DEV
            for a in range(n):
                src, dst = _push_ends(kind, ins[a], got[a], axes[a], lens[a], me, to)
                pltpu.make_async_remote_copy(src_ref=src, dst_ref=dst, send_sem=send.at[_pair(a, d)],
                                             recv_sem=recv.at[_pair(a, d)], device_id=_mesh_id(to),
                                             device_id_type=pl.DeviceIdType.MESH).wait_send()
        for cp in mine:
            cp.wait()

    hbm = pl.BlockSpec(memory_space=pltpu.HBM)
    sem = pl.BlockSpec(memory_space=pltpu.SEMAPHORE)
    outs = pl.pallas_call(
        body, name=name,
        in_specs=[hbm] * (2 * n) + [sem, sem] + [_UNTOUCHED] * len(after),
        out_specs=[hbm] * (2 * n),
        out_shape=[pltpu.HBM(t.shape, t.dtype) for t in list(srcs) + list(lands)],
        input_output_aliases={i: i for i in range(2 * n)},
        scratch_shapes=[pltpu.SemaphoreType.DMA((n,))],
        compiler_params=pltpu.CompilerParams(has_side_effects=pltpu.SideEffectType.DATAFLOW_SIDE_EFFECTING),
    )(*srcs, *lands, handle["send"], handle["recv"], *after)
    return outs[n:]


def _all_reduce_small(packed):
    r, c = packed.shape

    def body(p_ref, o_ref, buf, send, recv):
        me = _my_index()
        buf[me] = p_ref[...]
        pushes = []
        for d in range(1, N_DEV):
            to = (me + d) % N_DEV
            cp = pltpu.make_async_remote_copy(src_ref=p_ref, dst_ref=buf.at[me], send_sem=send.at[d],
                                              recv_sem=recv.at[d], device_id=_mesh_id(to),
                                              device_id_type=pl.DeviceIdType.MESH)
            cp.start()
            pushes.append(cp)
        for d in range(1, N_DEV):
            frm = (me + N_DEV - d) % N_DEV
            pltpu.make_async_remote_copy(src_ref=p_ref, dst_ref=buf.at[frm], send_sem=send.at[d], recv_sem=recv.at[d],
                                         device_id=_mesh_id(frm), device_id_type=pl.DeviceIdType.MESH).wait_recv()
        for cp in pushes:
            cp.wait_send()
        acc = buf[0]
        for s in range(1, N_DEV):
            acc = acc + buf[s]
        o_ref[...] = acc

    vmem = pl.BlockSpec(memory_space=pltpu.VMEM)
    return pl.pallas_call(
        body, name="all_reduce_small", in_specs=[vmem], out_specs=vmem, out_shape=jax.ShapeDtypeStruct((r, c), F32),
        scratch_shapes=[pltpu.VMEM((N_DEV, r, c), F32), pltpu.SemaphoreType.DMA((N_DEV,)),
                        pltpu.SemaphoreType.DMA((N_DEV,))],
        compiler_params=_params(),
    )(packed)


def _layer_fwd(x, mem, sm, wf, after):
    d = x.shape[1]
    h = _rmsnorm("rms_mix_pre", x, sm["g_mix_pre"], after)
    proj = _mm_nn("proj", h, wf["w_in"], F32)
    mem_n = _rmsnorm("rms_mem", mem, sm["g_mem"], mem)
    kv = _mm_nn("kv", mem_n, wf["w_kv"], BF16)
    a1, abo = _branch_fwd(proj, kv, wf["conv_a_w"], sm["conv_a_b"], sm["ln_a_g"], sm["ln_a_b"], wf["conv_b_w"])
    y, merged = _merge_fwd(abo, wf["w_cat"], proj, d)
    z, x1 = _mm_res_norm("mix_out", merged, wf["w_o"], x, sm["g_mix_post"])
    h2 = _rmsnorm("rms_mlp_pre", x1, sm["g_mlp_pre"], x1)
    r, act = _mm_relu2("mlp_up", h2, wf["w_up"])
    f, x2 = _mm_res_norm("mlp_down", act, wf["w_down"], x1, sm["g_mlp_post"])
    saved = dict(x=x, h=h, proj=proj, mem_n=mem_n, kv=kv, a1=a1, abo=abo, y=y, merged=merged, z=z, x1=x1, h2=h2,
                 r=r, act=act, f=f)
    return x2, saved


def _pad_rows(t):
    return jnp.pad(t, ((0, (-t.shape[0]) % 8), (0, 0)))


def _layer_bwd(dx2, mem, sm, wf, sv, after):
    c = sv["a1"].shape[1]
    df, dg_mlp_post = _rms_bwd("rms_bwd_post", dx2, sv["f"], sm["g_mlp_post"], BF16, after)
    d_up = _mm_relu2_bwd("mlp_down_bwd", df, wf["w_down"], sv["r"])
    gw_down = _mm_tn("gw_down", sv["act"], df, BF16)
    dx1, dg_mlp_pre = _mm_rms_bwd("mlp_up_bwd", d_up, wf["w_up"], sv["x1"], sm["g_mlp_pre"], dx2)
    gw_up = _mm_tn("gw_up", sv["h2"], d_up, BF16)
    dz, dg_mix_post = _rms_bwd("rms_bwd_post", dx1, sv["z"], sm["g_mix_post"], BF16, dx1)
    dy, dproj = _merge_bwd(dz, wf["w_o"], sv["y"], sv["proj"])
    gw_o = _mm_tn("gw_o", sv["merged"], dz, BF16)
    dabo = _mm_nt("branch_out_bwd", dy, wf["w_cat"], F32)
    gw_cat = _mm_tn("gw_branch_out", sv["abo"], dy, BF16)
    dproj, dkv, dconv_a_w, misc = _branch_bwd(sv["proj"], sv["a1"], dabo, sv["kv"], wf["conv_a_w"], sm["ln_a_g"],
                                              sm["ln_a_b"], wf["conv_b_w"], dproj)
    dkv = dkv.astype(BF16)
    gw_kv = _mm_tn("gw_kv", sv["mem_n"], dkv, BF16)
    dmem_n = _mm_nt("kv_bwd", dkv, wf["w_kv"], F32)
    _, dg_mem = _rms_bwd("rms_bwd_mem", dmem_n, mem, sm["g_mem"], BF16, dmem_n)
    gw_in = _mm_tn("gw_in", sv["h"], dproj, BF16)
    dx, dg_mix_pre = _mm_rms_bwd("proj_bwd", dproj, wf["w_in"], sv["x"], sm["g_mix_pre"], dx1)
    big = dict(w_in=gw_in, w_cat=gw_cat, w_kv=gw_kv, w_o=gw_o, w_up=gw_up, w_down=gw_down)
    rows = [_pad_rows(t.reshape(2, c)) for t in (dg_mix_pre, dg_mem, dg_mix_post, dg_mlp_pre, dg_mlp_post)]
    return dx, big, jnp.concatenate(rows + [misc, dconv_a_w], axis=0)


_SMALL_D_NAMES = ["g_mix_pre", "g_mem", "g_mix_post", "g_mlp_pre", "g_mlp_post"]
_SMALL_C_NAMES = ["conv_a_b", "ln_a_g", "ln_a_b"]
_SMALL_MISC_ROW = 8 * len(_SMALL_D_NAMES)
_SMALL_CONV_B_ROW = _SMALL_MISC_ROW + len(_SMALL_C_NAMES)
_SMALL_CONV_A_ROW = _SMALL_MISC_ROW + 8
_SMALL_ROWS = _SMALL_CONV_A_ROW + HALO_A


def kernel(x, mem, g_mix_pre, w_in, conv_a_w, conv_a_b, ln_a_g, ln_a_b, w_a_out, conv_b_w, w_b_out, g_mem, w_kv, w_x_out, w_o, g_mix_post, g_mlp_pre, w_up, w_down, g_mlp_post, loss_target, m_g_mix_pre, m_w_in, m_conv_a_w, m_conv_a_b, m_ln_a_g, m_ln_a_b, m_w_a_out, m_conv_b_w, m_w_b_out, m_g_mem, m_w_kv, m_w_x_out, m_w_o, m_g_mix_post, m_g_mlp_pre, m_w_up, m_w_down, m_g_mlp_post, v_g_mix_pre, v_w_in, v_conv_a_w, v_conv_a_b, v_ln_a_g, v_ln_a_b, v_w_a_out, v_conv_b_w, v_w_b_out, v_g_mem, v_w_kv, v_w_x_out, v_w_o, v_g_mix_post, v_g_mlp_pre, v_w_up, v_w_down, v_g_mlp_post):
    names = ["g_mix_pre", "w_in", "conv_a_w", "conv_a_b", "ln_a_g", "ln_a_b", "w_a_out", "conv_b_w", "w_b_out",
             "g_mem", "w_kv", "w_x_out", "w_o", "g_mix_post", "g_mlp_pre", "w_up", "w_down", "g_mlp_post"]
    w = dict(zip(names, [g_mix_pre, w_in, conv_a_w, conv_a_b, ln_a_g, ln_a_b, w_a_out, conv_b_w, w_b_out, g_mem,
                         w_kv, w_x_out, w_o, g_mix_post, g_mlp_pre, w_up, w_down, g_mlp_post]))
    mo = dict(zip(names, [m_g_mix_pre, m_w_in, m_conv_a_w, m_conv_a_b, m_ln_a_g, m_ln_a_b, m_w_a_out, m_conv_b_w,
                          m_w_b_out, m_g_mem, m_w_kv, m_w_x_out, m_w_o, m_g_mix_post, m_g_mlp_pre, m_w_up, m_w_down,
                          m_g_mlp_post]))
    vo = dict(zip(names, [v_g_mix_pre, v_w_in, v_conv_a_w, v_conv_a_b, v_ln_a_g, v_ln_a_b, v_w_a_out, v_conv_b_w,
                          v_w_b_out, v_g_mem, v_w_kv, v_w_x_out, v_w_o, v_g_mix_post, v_g_mlp_pre, v_w_up, v_w_down,
                          v_g_mlp_post]))
    depth = w_in.shape[0]
    c = conv_a_b.shape[1]
    cs = conv_a_w.shape[2]
    me = _my_index()
    xs, mems, tgt = x[0], mem[0], loss_target[0]

    big_names = ["w_in", "w_cat", "w_kv", "w_o", "w_up", "w_down"]
    big_axes = [1, 2, 0, 0, 1, 0]
    branch_names = ["w_a_out", "w_b_out", "w_x_out"]
    smalls = [{k: w[k][l][None, :] for k in _SMALL_D_NAMES + _SMALL_C_NAMES} for l in range(depth)]

    def weight_shards(l):
        return [w_in[l].astype(BF16), jnp.stack([w[k][l] for k in branch_names]).astype(BF16), w_kv[l].astype(BF16),
                w_o[l].astype(BF16), w_up[l].astype(BF16), w_down[l].astype(BF16)]

    taps_a, taps_b = _all_gather(
        "gather_conv_taps",
        [jnp.pad(conv_a_w, ((0, 0), (0, HALO_A - CONV_A_K), (0, 0))),
         jnp.pad(conv_b_w, ((0, 0), (0, HALO_B - CONV_B_K), (0, 0)))], [2, 2])

    handle, _ = _exchange_start("gather_start_0", "gather", weight_shards(0), big_axes, taps_a)
    fulls = [None] * depth
    saved = []
    xc = xs
    for l in range(depth):
        fulls[l] = dict(zip(big_names, _exchange_wait(f"gather_wait_{l}", handle, [xc])))
        fulls[l]["conv_a_w"], fulls[l]["conv_b_w"] = taps_a[l], taps_b[l]
        token = fulls[l]["w_in"]
        if l + 1 < depth:
            handle, token = _exchange_start(f"gather_start_{l + 1}", "gather", weight_shards(l + 1), big_axes,
                                            fulls[l]["w_in"])
        xc, sv = _layer_fwd(xc, mems, smalls[l], fulls[l], token)
        saved.append(sv)
    loss_part, dx = _loss_head(xc, tgt)
    loss = lax.psum(loss_part, MESH_AXES)

    upd = {}

    def adamw_layer(l, slots, after):
        for k, sl in zip(big_names, slots):
            if k == "w_cat":
                for b, nm in enumerate(branch_names):
                    upd[nm] = _adamw("adamw_" + nm, sl, b, w[nm], mo[nm], vo[nm], l, upd.get(nm), after)
            else:
                upd[k] = _adamw("adamw_" + k, sl, None, w[k], mo[k], vo[k], l, upd.get(k), after)

    small_parts = [None] * depth
    in_flight = None
    token = dx
    for l in reversed(range(depth)):
        dx, big, small_parts[l] = _layer_bwd(dx, mems, smalls[l], fulls[l], saved[l], token)
        slots = None if in_flight is None else _exchange_wait(f"scatter_wait_{l + 1}", in_flight, [dx])
        in_flight, token = _exchange_start(f"scatter_start_{l}", "scatter", [big[k] for k in big_names], big_axes,
                                           dx if slots is None else slots[0])
        if slots is not None:
            adamw_layer(l + 1, slots, token)

    tot = _all_reduce_small(jnp.concatenate(small_parts, axis=0)).reshape(depth, _SMALL_ROWS, c)
    g_small = {}
    for n_, nm in enumerate(_SMALL_D_NAMES):
        g_small[nm] = tot[:, 8 * n_:8 * n_ + 2, :].reshape(depth, 2 * c)
    for n_, nm in enumerate(_SMALL_C_NAMES):
        g_small[nm] = tot[:, _SMALL_MISC_ROW + n_, :]
    g_taps = {"conv_a_w": tot[:, _SMALL_CONV_A_ROW:_SMALL_CONV_A_ROW + CONV_A_K, :],
              "conv_b_w": tot[:, _SMALL_CONV_B_ROW:_SMALL_CONV_B_ROW + CONV_B_K, :]}

    def pack(t):
        return jnp.concatenate([_pad_rows(t[nm].reshape(-1, c)) for nm in _SMALL_D_NAMES + _SMALL_C_NAMES], axis=0)

    res = _adamw("adamw_small", pack(g_small)[None], None, pack(w)[None], pack(mo)[None], pack(vo)[None], 0, None, tot)
    row = 0
    for nm in _SMALL_D_NAMES + _SMALL_C_NAMES:
        n_rows = w[nm].size // c
        upd[nm] = [t[0, row:row + n_rows].reshape(w[nm].shape) for t in res]
        row += n_rows + (-n_rows) % 8
    for nm, g in g_taps.items():
        shp = w[nm].shape
        mine = lax.dynamic_slice_in_dim(g, me * cs, cs, axis=2).reshape(1, -1, cs)
        res = _adamw("adamw_" + nm, mine, None, w[nm].reshape(1, -1, cs), mo[nm].reshape(1, -1, cs),
                     vo[nm].reshape(1, -1, cs), 0, None, tot)
        upd[nm] = [t.reshape(shp) for t in res]
    done = [u[0] for u in upd.values()]
    adamw_layer(0, _exchange_wait("scatter_wait_0", in_flight, done), tot)

    return (loss, dx[None], *[upd[nm][0] for nm in names], *[upd[nm][1] for nm in names],
            *[upd[nm][2] for nm in names], *[upd[nm][3] for nm in names])
```

```python
import math

import jax
import jax.numpy as jnp
from jax import lax
from jax.experimental import pallas as pl
from jax.experimental.pallas import tpu as pltpu

F32 = jnp.float32
BF16 = jnp.bfloat16

EPS = 1e-6
N_HEADS = 4
N_BRANCH = 3
CONV_A_K = 31
CONV_B_K = 3
HALO_A = 32
HALO_B = 8
N_DEV = 8
MESH_AXES = ("x", "y", "c")

ADAM_LR = 0.001
ADAM_B1 = 0.9
ADAM_B2 = 0.999
ADAM_EPS = 1e-08
ADAM_WD = 0.01
ADAM_STEP = 10

V7X_VMEM_BYTES = 64 * 1024 * 1024
VMEM_LIMIT = V7X_VMEM_BYTES - 8 * 1024 * 1024

TILE_M = 1024
TILE_N = 1024
TILE_K = 2048
TILE_ROWS_WIDE = 512
TILE_K_WIDE = 1024
TILE_ROWS_BRANCH = 256
TILE_ROWS_BRANCH_BWD = 128


def _params():
    return pltpu.CompilerParams(vmem_limit_bytes=VMEM_LIMIT)


def _tile(n, t):
    t = min(n, t)
    assert n % t == 0, (n, t)
    return t


def _my_index():
    return 4 * lax.axis_index("x") + 2 * lax.axis_index("y") + lax.axis_index("c")


def _mesh_id(p):
    return (p // 4, (p // 2) % 2, p % 2)


def _mm(name, grid, a, a_spec, b, b_spec, *, ta, tb, acc_shape, extras, outs, epilogue):
    nk = grid[-1]
    lead = len(grid) - 3
    ne, no = len(extras), len(outs)
    dn = (((0,) if ta else (1,), (1,) if tb else (0,)), ((), ()))

    def body(a_ref, b_ref, *rest):
        extra_refs = rest[:ne]
        out_refs = rest[ne:ne + no]
        i, j, k = pl.program_id(lead), pl.program_id(lead + 1), pl.program_id(lead + 2)
        prod = lax.dot_general(a_ref[...], b_ref[...], dn, preferred_element_type=F32)
        if nk == 1:
            epilogue(prod, extra_refs, out_refs, i, j)
        else:
            acc_ref = rest[ne + no]

            @pl.when(k == 0)
            def _():
                acc_ref[...] = prod

            @pl.when(k > 0)
            def _():
                acc_ref[...] += prod

            @pl.when(k == nk - 1)
            def _():
                epilogue(acc_ref[...], extra_refs, out_refs, i, j)

    return pl.pallas_call(
        body, name=name, grid=grid,
        in_specs=[a_spec, b_spec] + [s for _, s in extras],
        out_specs=[s for _, s in outs],
        out_shape=[o for o, _ in outs],
        scratch_shapes=[pltpu.VMEM(acc_shape, F32)] if nk > 1 else [],
        compiler_params=_params(),
    )(a, b, *[e for e, _ in extras])


def _store_epilogue(res, extra_refs, out_refs, i, j):
    out_refs[0][...] = res.astype(out_refs[0].dtype)


def _mm_nn(name, a, b, out_dtype):
    m, kd = a.shape
    n = b.shape[1]
    bm, bn = _tile(m, TILE_M), _tile(n, TILE_N)
    return _mm(name, (m // bm, n // bn, 1), a, pl.BlockSpec((bm, kd), lambda i, j, k: (i, 0)),
               b, pl.BlockSpec((kd, bn), lambda i, j, k: (0, j)), ta=False, tb=False, acc_shape=(bm, bn), extras=[],
               outs=[(jax.ShapeDtypeStruct((m, n), out_dtype), pl.BlockSpec((bm, bn), lambda i, j, k: (i, j)))],
               epilogue=_store_epilogue)[0]


def _mm_nt(name, a, b, out_dtype):
    m, kd = a.shape[-2:]
    n = b.shape[-2]
    bm, bn = _tile(m, TILE_M), _tile(n, TILE_N)
    if a.ndim == 2:
        return _mm(name, (m // bm, n // bn, 1), a, pl.BlockSpec((bm, kd), lambda i, j, k: (i, 0)),
                   b, pl.BlockSpec((bn, kd), lambda i, j, k: (j, 0)), ta=False, tb=True, acc_shape=(bm, bn), extras=[],
                   outs=[(jax.ShapeDtypeStruct((m, n), out_dtype), pl.BlockSpec((bm, bn), lambda i, j, k: (i, j)))],
                   epilogue=_store_epilogue)[0]
    nb = a.shape[0]
    return _mm(name, (nb, m // bm, n // bn, 1), a, pl.BlockSpec((None, bm, kd), lambda s, i, j, k: (s, i, 0)),
               b, pl.BlockSpec((None, bn, kd), lambda s, i, j, k: (s, j, 0)), ta=False, tb=True, acc_shape=(bm, bn),
               extras=[],
               outs=[(jax.ShapeDtypeStruct((nb, m, n), out_dtype),
                      pl.BlockSpec((None, bm, bn), lambda s, i, j, k: (s, i, j)))],
               epilogue=_store_epilogue)[0]


def _mm_tn(name, a, b, out_dtype):
    r, m = a.shape[-2:]
    n = b.shape[-1]
    bm, bn, bk = _tile(m, TILE_M), _tile(n, TILE_N), _tile(r, TILE_K)
    if a.ndim == 2:
        return _mm(name, (m // bm, n // bn, r // bk), a, pl.BlockSpec((bk, bm), lambda i, j, k: (k, i)),
                   b, pl.BlockSpec((bk, bn), lambda i, j, k: (k, j)), ta=True, tb=False, acc_shape=(bm, bn), extras=[],
                   outs=[(jax.ShapeDtypeStruct((m, n), out_dtype), pl.BlockSpec((bm, bn), lambda i, j, k: (i, j)))],
                   epilogue=_store_epilogue)[0]
    nb = a.shape[0]
    return _mm(name, (nb, m // bm, n // bn, r // bk), a, pl.BlockSpec((None, bk, bm), lambda s, i, j, k: (s, k, i)),
               b, pl.BlockSpec((None, bk, bn), lambda s, i, j, k: (s, k, j)), ta=True, tb=False, acc_shape=(bm, bn),
               extras=[],
               outs=[(jax.ShapeDtypeStruct((nb, m, n), out_dtype),
                      pl.BlockSpec((None, bm, bn), lambda s, i, j, k: (s, i, j)))],
               epilogue=_store_epilogue)[0]


def _mm_relu2(name, h, w):
    m, kd = h.shape
    n = w.shape[1]
    bm, bn = _tile(m, TILE_M), _tile(n, TILE_N)

    def epilogue(res, extra_refs, out_refs, i, j):
        r = jnp.maximum(res, 0.0)
        out_refs[0][...] = r.astype(BF16)
        out_refs[1][...] = (r * r).astype(BF16)

    o = jax.ShapeDtypeStruct((m, n), BF16)
    spec = pl.BlockSpec((bm, bn), lambda i, j, k: (i, j))
    return _mm(name, (m // bm, n // bn, 1), h, pl.BlockSpec((bm, kd), lambda i, j, k: (i, 0)),
               w, pl.BlockSpec((kd, bn), lambda i, j, k: (0, j)), ta=False, tb=False, acc_shape=(bm, bn), extras=[],
               outs=[(o, spec), (o, spec)], epilogue=epilogue)


def _mm_relu2_bwd(name, df, w_down, r):
    m, kd = df.shape
    n = w_down.shape[0]
    bm, bn = _tile(m, TILE_M), _tile(n, TILE_N)

    def epilogue(res, extra_refs, out_refs, i, j):
        out_refs[0][...] = (res * (2.0 * extra_refs[0][...].astype(F32))).astype(BF16)

    spec = pl.BlockSpec((bm, bn), lambda i, j, k: (i, j))
    return _mm(name, (m // bm, n // bn, 1), df, pl.BlockSpec((bm, kd), lambda i, j, k: (i, 0)),
               w_down, pl.BlockSpec((bn, kd), lambda i, j, k: (j, 0)), ta=False, tb=True, acc_shape=(bm, bn),
               extras=[(r, spec)], outs=[(jax.ShapeDtypeStruct((m, n), BF16), spec)], epilogue=epilogue)[0]


def _mm_res_norm(name, a, w, xres, g):
    m, kd = a.shape
    d = w.shape[1]
    bm, bk = _tile(m, TILE_ROWS_WIDE), _tile(kd, TILE_K_WIDE)

    def epilogue(res, extra_refs, out_refs, i, j):
        x_ref, g_ref = extra_refs
        rr = lax.rsqrt(jnp.mean(res * res, axis=-1, keepdims=True) + EPS)
        out_refs[0][...] = res
        out_refs[1][...] = x_ref[...] + (res * rr) * g_ref[...]

    row = pl.BlockSpec((bm, d), lambda i, j, k: (i, 0))
    o = jax.ShapeDtypeStruct((m, d), F32)
    return _mm(name, (m // bm, 1, kd // bk), a, pl.BlockSpec((bm, bk), lambda i, j, k: (i, k)),
               w, pl.BlockSpec((bk, d), lambda i, j, k: (k, 0)), ta=False, tb=False, acc_shape=(bm, d),
               extras=[(xres, row), (g, pl.BlockSpec((1, d), lambda i, j, k: (0, 0)))],
               outs=[(o, row), (o, row)], epilogue=epilogue)


def _rms_bwd_math(dy, x, g):
    rr = lax.rsqrt(jnp.mean(x * x, axis=-1, keepdims=True) + EPS)
    gy = dy * g
    dx = rr * gy - x * (rr * rr * rr * jnp.mean(x * gy, axis=-1, keepdims=True))
    dg_rows = dy * x * rr
    return dx, dg_rows


def _mm_rms_bwd(name, a, w, x, g, dres, after):
    m, kd = a.shape
    d = w.shape[0]
    bm, bk = _tile(m, TILE_ROWS_WIDE), _tile(kd, TILE_K_WIDE)

    def epilogue(res, extra_refs, out_refs, i, j):
        x_ref, g_ref, dres_ref, _ = extra_refs
        dx, dg_rows = _rms_bwd_math(res, x_ref[...], g_ref[...])
        out_refs[0][...] = dres_ref[...] + dx

        @pl.when(i == 0)
        def _():
            out_refs[1][...] = jnp.zeros_like(out_refs[1])

        out_refs[1][...] += jnp.sum(dg_rows, axis=0, keepdims=True)

    row = pl.BlockSpec((bm, d), lambda i, j, k: (i, 0))
    vec = pl.BlockSpec((1, d), lambda i, j, k: (0, 0))
    return _mm(name, (m // bm, 1, kd // bk), a, pl.BlockSpec((bm, bk), lambda i, j, k: (i, k)),
               w, pl.BlockSpec((d, bk), lambda i, j, k: (0, k)), ta=False, tb=True, acc_shape=(bm, d),
               extras=[(x, row), (g, vec), (dres, row), (after, _UNTOUCHED)],
               outs=[(jax.ShapeDtypeStruct((m, d), F32), row), (jax.ShapeDtypeStruct((1, d), F32), vec)],
               epilogue=epilogue)


_UNTOUCHED = pl.BlockSpec(memory_space=pl.ANY)


def _rmsnorm(name, x, g, after):
    r, d = x.shape
    br = _tile(r, TILE_ROWS_WIDE)

    def body(x_ref, g_ref, after_ref, o_ref):
        xv = x_ref[...]
        rr = lax.rsqrt(jnp.mean(xv * xv, axis=-1, keepdims=True) + EPS)
        o_ref[...] = ((xv * rr) * g_ref[...]).astype(BF16)

    row = pl.BlockSpec((br, d), lambda i: (i, 0))
    return pl.pallas_call(body, name=name, grid=(r // br,),
                          in_specs=[row, pl.BlockSpec((1, d), lambda i: (0, 0)), _UNTOUCHED],
                          out_specs=row, out_shape=jax.ShapeDtypeStruct((r, d), BF16),
                          compiler_params=_params())(x, g, after)


def _rms_bwd(name, dy, x, g, out_dtype, after):
    r, d = x.shape
    br = _tile(r, TILE_ROWS_WIDE)

    def body(dy_ref, x_ref, g_ref, after_ref, dx_ref, dg_ref):
        dx, dg_rows = _rms_bwd_math(dy_ref[...], x_ref[...], g_ref[...])
        dx_ref[...] = dx.astype(out_dtype)

        @pl.when(pl.program_id(0) == 0)
        def _():
            dg_ref[...] = jnp.zeros_like(dg_ref)

        dg_ref[...] += jnp.sum(dg_rows, axis=0, keepdims=True)

    row = pl.BlockSpec((br, d), lambda i: (i, 0))
    vec = pl.BlockSpec((1, d), lambda i: (0, 0))
    return pl.pallas_call(body, name=name, grid=(r // br,), in_specs=[row, row, vec, _UNTOUCHED],
                          out_specs=[row, vec],
                          out_shape=[jax.ShapeDtypeStruct((r, d), out_dtype), jax.ShapeDtypeStruct((1, d), F32)],
                          compiler_params=_params())(dy, x, g, after)


def _loss_head(y, target):
    r, d = y.shape
    br = _tile(r, TILE_ROWS_WIDE)

    def body(y_ref, t_ref, dy_ref, l_ref):
        diff = y_ref[...] - t_ref[...]
        dy_ref[...] = diff * (1.0 / d)

        @pl.when(pl.program_id(0) == 0)
        def _():
            l_ref[...] = jnp.zeros_like(l_ref)

        l_ref[...] += 0.5 * jnp.sum(jnp.mean(diff * diff, axis=-1, keepdims=True))

    row = pl.BlockSpec((br, d), lambda i: (i, 0))
    one = pl.BlockSpec((8, 128), lambda i: (0, 0))
    dy, l = pl.pallas_call(body, name="loss_head", grid=(r // br,), in_specs=[row, row], out_specs=[row, one],
                           out_shape=[jax.ShapeDtypeStruct((r, d), F32), jax.ShapeDtypeStruct((8, 128), F32)],
                           compiler_params=_params())(y, target)
    return l[0, 0], dy


def _adamw(name, parts, branch, w, m, v, layer, prev, after):
    _, r, c = w.shape
    p = parts.shape[0]
    br = r
    for cand in (512, 256, 128, 64, 32, 16):
        if r % cand == 0 and cand * c * 4 <= 2 * 1024 * 1024:
            br = cand
            break
    bc1 = 1.0 - ADAM_B1 ** ADAM_STEP
    bc2 = 1.0 - ADAM_B2 ** ADAM_STEP
    if prev is None:
        prev = [lax.empty(w.shape, F32) for _ in range(4)]

    def body(p_ref, w_ref, m_ref, v_ref, after_ref, pg, pd, pm, pv, g_out, d_out, m_out, v_out):
        g = p_ref[0].astype(F32)
        for s in range(1, p):
            g = g + p_ref[s].astype(F32)
        m2 = ADAM_B1 * m_ref[...] + (1.0 - ADAM_B1) * g
        v2 = ADAM_B2 * v_ref[...] + (1.0 - ADAM_B2) * (g * g)
        m_hat = m2 / bc1
        v_hat = v2 / bc2
        g_out[...] = g
        d_out[...] = -ADAM_LR * (m_hat / (jnp.sqrt(v_hat) + ADAM_EPS) + ADAM_WD * w_ref[...])
        m_out[...] = m2
        v_out[...] = v2

    if branch is None:
        p_spec = pl.BlockSpec((p, br, c), lambda i: (0, i, 0))
    else:
        p_spec = pl.BlockSpec((p, None, br, c), lambda i: (0, branch, i, 0))
    slab = pl.BlockSpec((None, br, c), lambda i: (layer, i, 0))
    o = jax.ShapeDtypeStruct(w.shape, F32)
    return pl.pallas_call(body, name=name, grid=(r // br,),
                          in_specs=[p_spec, slab, slab, slab, _UNTOUCHED] + [_UNTOUCHED] * 4,
                          out_specs=[slab] * 4, out_shape=[o] * 4, input_output_aliases={5: 0, 6: 1, 7: 2, 8: 3},
                          compiler_params=_params())(parts, w, m, v, after, *prev)


def _layer_norm_parts(a1, ln_g, ln_b):
    mu = jnp.mean(a1, axis=-1, keepdims=True)
    xc = a1 - mu
    rstd = lax.rsqrt(jnp.mean(xc * xc, axis=-1, keepdims=True) + EPS)
    xhat = xc * rstd
    a2 = xhat * ln_g + ln_b
    return xhat, rstd, a2


def _softmax_rows(s):
    e = jnp.exp(s - jnp.max(s, axis=-1, keepdims=True))
    return e / jnp.sum(e, axis=-1, keepdims=True)


SUBLANES = 8


def _shift_copies(ext_ref, sh_ref, rows):
    for r in range(1, SUBLANES):
        sh_ref[r - 1, pl.ds(0, rows), :] = ext_ref[pl.ds(r, rows), :]


def _rows_at(ext_ref, sh_ref, offset, ts):
    q, r = divmod(offset, SUBLANES)
    if r == 0:
        return ext_ref[pl.ds(SUBLANES * q, ts), :]
    return sh_ref[r - 1, pl.ds(SUBLANES * q, ts), :]


def _branch_specs(ts, c):
    def col(ci):
        return pl.BlockSpec((ts, c), lambda i: (i, ci))

    def prev(ci, h):
        return pl.BlockSpec((h, c), lambda i: (jnp.maximum(i * (ts // h) - 1, 0), ci))

    return col, prev


def _branch_fwd(proj, kv, conv_a_w, conv_a_b, ln_g, ln_b, conv_b_w):
    s = proj.shape[0]
    mlen, c2 = kv.shape
    c = c2 // 2
    hd = c // N_HEADS
    ts = _tile(s, TILE_ROWS_BRANCH)
    scale = hd ** -0.5
    col, prev = _branch_specs(ts, c)

    def body(av, ag, sb, sc, sx, q, hav, hag, hsc, hsx, kv_ref, caw, cab, lng, lnb, cbw, a1_ref, abo_ref, exta, extb,
             sha):
        not_first = (pl.program_id(0) > 0).astype(F32)
        exta[pl.ds(0, HALO_A), :] = hav[...] * jax.nn.sigmoid(hag[...]) * not_first
        exta[pl.ds(HALO_A, ts), :] = av[...] * jax.nn.sigmoid(ag[...])
        _shift_copies(exta, sha, ts + HALO_A - SUBLANES)
        acc = jnp.broadcast_to(cab[...], (ts, c))
        for k in range(CONV_A_K):
            acc = acc + caw[pl.ds(k, 1), :] * _rows_at(exta, sha, HALO_A - (CONV_A_K - 1) + k, ts)
        a1_ref[...] = acc
        _, _, a2 = _layer_norm_parts(acc, lng[...], lnb[...])
        abo_ref[0] = (a2 * jax.nn.sigmoid(a2)).astype(BF16)
        extb[pl.ds(0, HALO_B), :] = hsc[...] * hsx[...] * not_first
        extb[pl.ds(HALO_B, ts), :] = sc[...] * sx[...]
        u = cbw[pl.ds(0, 1), :] * extb[pl.ds(HALO_B - (CONV_B_K - 1), ts), :]
        for k in range(1, CONV_B_K):
            u = u + cbw[pl.ds(k, 1), :] * extb[pl.ds(HALO_B - (CONV_B_K - 1) + k, ts), :]
        abo_ref[1] = (sb[...] * u).astype(BF16)
        for h in range(N_HEADS):
            qh = q[:, h * hd:(h + 1) * hd].astype(BF16)
            kh = kv_ref[:, h * hd:(h + 1) * hd]
            vh = kv_ref[:, c + h * hd:c + (h + 1) * hd]
            sc_ = lax.dot_general(qh, kh, (((1,), (1,)), ((), ())), preferred_element_type=F32) * scale
            p = _softmax_rows(sc_).astype(BF16)
            abo_ref[2, :, h * hd:(h + 1) * hd] = jnp.dot(p, vh, preferred_element_type=F32).astype(BF16)

    full = lambda shp: pl.BlockSpec(shp, lambda i: (0,) * len(shp))
    return pl.pallas_call(
        body, name="branch_fwd", grid=(s // ts,),
        in_specs=[col(0), col(1), col(2), col(3), col(4), col(5), prev(0, HALO_A), prev(1, HALO_A),
                  prev(3, HALO_B), prev(4, HALO_B), full((mlen, c2)), full(conv_a_w.shape), full((1, c)),
                  full((1, c)), full((1, c)), full(conv_b_w.shape)],
        out_specs=[pl.BlockSpec((ts, c), lambda i: (i, 0)), pl.BlockSpec((N_BRANCH, ts, c), lambda i: (0, i, 0))],
        out_shape=[jax.ShapeDtypeStruct((s, c), F32), jax.ShapeDtypeStruct((N_BRANCH, s, c), BF16)],
        scratch_shapes=[pltpu.VMEM((HALO_A + ts, c), F32), pltpu.VMEM((HALO_B + ts, c), F32),
                        pltpu.VMEM((SUBLANES - 1, ts + HALO_A - SUBLANES, c), F32)],
        compiler_params=_params(),
    )(proj, proj, proj, proj, proj, proj, proj, proj, proj, proj, kv, conv_a_w, conv_a_b, ln_g, ln_b, conv_b_w)


def _branch_bwd(proj, a1, dabo, kv, conv_a_w, ln_g, ln_b, conv_b_w, dproj):
    s = proj.shape[0]
    mlen, c2 = kv.shape
    c = c2 // 2
    hd = c // N_HEADS
    ts = _tile(s, TILE_ROWS_BRANCH_BWD)
    nt = s // ts
    scale = hd ** -0.5
    col, prev = _branch_specs(ts, c)

    def nxt(h, lead=None, ci=0):
        if lead is None:
            return pl.BlockSpec((h, c), lambda i: (jnp.minimum((i + 1) * (ts // h), s // h - 1), ci))
        return pl.BlockSpec((None, h, c), lambda i: (lead, jnp.minimum((i + 1) * (ts // h), s // h - 1), 0))

    def body(av, ag, sb, sc, sx, q, hav, hag, hsc, hsx, nsb, a1_ref, na1, dabo_ref, nda, ndb, kv_ref, caw, lng, lnb,
             cbw, dproj_in, dp_ref, dkv_ref, dwa_ref, misc_ref, exta, extda, extb, extdu, sha, shda):
        i = pl.program_id(0)
        not_first = (i > 0).astype(F32)
        not_last = (i < nt - 1).astype(F32)

        @pl.when(i == 0)
        def _():
            dkv_ref[...] = jnp.zeros_like(dkv_ref)
            dwa_ref[...] = jnp.zeros_like(dwa_ref)
            misc_ref[...] = jnp.zeros_like(misc_ref)

        def rowsum(t):
            return jnp.sum(t, axis=0, keepdims=True)

        def da1_of(a1v, da3):
            xhat, rstd, a2 = _layer_norm_parts(a1v, lng[...], lnb[...])
            sg = jax.nn.sigmoid(a2)
            da2 = da3 * (sg * (1.0 + a2 * (1.0 - sg)))
            dxh = da2 * lng[...]
            da1 = rstd * (dxh - jnp.mean(dxh, axis=-1, keepdims=True)
                          - xhat * jnp.mean(dxh * xhat, axis=-1, keepdims=True))
            return da1, da2, xhat

        da1, da2, xhat = da1_of(a1_ref[...], dabo_ref[0])
        misc_ref[pl.ds(0, 1), :] += rowsum(da1)
        misc_ref[pl.ds(1, 1), :] += rowsum(da2 * xhat)
        misc_ref[pl.ds(2, 1), :] += rowsum(da2)
        extda[pl.ds(0, ts), :] = da1
        extda[pl.ds(ts, HALO_A), :] = da1_of(na1[...], nda[...])[0] * not_last
        sga = jax.nn.sigmoid(ag[...])
        exta[pl.ds(0, HALO_A), :] = hav[...] * jax.nn.sigmoid(hag[...]) * not_first
        exta[pl.ds(HALO_A, ts), :] = av[...] * sga
        _shift_copies(exta, sha, ts + HALO_A - SUBLANES)
        _shift_copies(extda, shda, ts + HALO_A - SUBLANES)
        da0 = jnp.zeros((ts, c), F32)
        for k in range(CONV_A_K):
            tap = da1 * _rows_at(exta, sha, HALO_A - (CONV_A_K - 1) + k, ts)
            dwa_ref[pl.ds(SUBLANES * k, SUBLANES), :] += tap.reshape(ts // SUBLANES, SUBLANES, c).sum(axis=0)
            da0 = da0 + caw[pl.ds(CONV_A_K - 1 - k, 1), :] * _rows_at(extda, shda, k, ts)
        dp_ref[:, 0:c] = (da0 * sga).astype(BF16)
        dp_ref[:, c:2 * c] = (da0 * av[...] * sga * (1.0 - sga)).astype(BF16)

        extb[pl.ds(0, HALO_B), :] = hsc[...] * hsx[...] * not_first
        extb[pl.ds(HALO_B, ts), :] = sc[...] * sx[...]
        dbu = dabo_ref[1]
        du = dbu * sb[...]
        extdu[pl.ds(0, ts), :] = du
        extdu[pl.ds(ts, HALO_B), :] = ndb[...] * nsb[...] * not_last
        u = jnp.zeros((ts, c), F32)
        dpr = jnp.zeros((ts, c), F32)
        for k in range(CONV_B_K):
            shifted = extb[pl.ds(HALO_B - (CONV_B_K - 1) + k, ts), :]
            u = u + cbw[pl.ds(k, 1), :] * shifted
            misc_ref[pl.ds(3 + k, 1), :] += rowsum(du * shifted)
            dpr = dpr + cbw[pl.ds(CONV_B_K - 1 - k, 1), :] * extdu[pl.ds(k, ts), :]
        dp_ref[:, 2 * c:3 * c] = (dbu * u).astype(BF16)
        dp_ref[:, 3 * c:4 * c] = (dpr * sx[...]).astype(BF16)
        dp_ref[:, 4 * c:5 * c] = (dpr * sc[...]).astype(BF16)

        nt_dims = (((1,), (1,)), ((), ()))
        tn_dims = (((0,), (0,)), ((), ()))
        for h in range(N_HEADS):
            lo, hi = h * hd, (h + 1) * hd
            qh = q[:, lo:hi].astype(BF16)
            kh = kv_ref[:, lo:hi]
            vh = kv_ref[:, c + lo:c + hi]
            p = _softmax_rows(lax.dot_general(qh, kh, nt_dims, preferred_element_type=F32) * scale)
            pb = p.astype(BF16)
            doh = dabo_ref[2, :, lo:hi].astype(BF16)
            dpm = lax.dot_general(doh, vh, nt_dims, preferred_element_type=F32)
            ds = (p * (dpm - jnp.sum(dpm * p, axis=-1, keepdims=True)) * scale).astype(BF16)
            dp_ref[:, 5 * c + lo:5 * c + hi] = jnp.dot(ds, kh, preferred_element_type=F32).astype(BF16)
            dkv_ref[:, lo:hi] += lax.dot_general(ds, qh, tn_dims, preferred_element_type=F32)
            dkv_ref[:, c + lo:c + hi] += lax.dot_general(pb, doh, tn_dims, preferred_element_type=F32)

    full = lambda shp: pl.BlockSpec(shp, lambda i: (0,) * len(shp))
    n_in_before_dproj = 21
    shifted = pltpu.VMEM((SUBLANES - 1, ts + HALO_A - SUBLANES, c), F32)
    dp, dkv, dwa, misc = pl.pallas_call(
        body, name="branch_bwd", grid=(nt,),
        in_specs=[col(0), col(1), col(2), col(3), col(4), col(5), prev(0, HALO_A), prev(1, HALO_A),
                  prev(3, HALO_B), prev(4, HALO_B), nxt(HALO_B, ci=2),
                  pl.BlockSpec((ts, c), lambda i: (i, 0)), nxt(HALO_A),
                  pl.BlockSpec((N_BRANCH, ts, c), lambda i: (0, i, 0)), nxt(HALO_A, lead=0), nxt(HALO_B, lead=1),
                  full((mlen, c2)), full(conv_a_w.shape), full((1, c)), full((1, c)), full(conv_b_w.shape),
                  pl.BlockSpec(memory_space=pl.ANY)],
        out_specs=[pl.BlockSpec((ts, 6 * c), lambda i: (i, 0)), full((mlen, c2)), full((HALO_A * SUBLANES, c)),
                   full((8, c))],
        out_shape=[jax.ShapeDtypeStruct(dproj.shape, BF16), jax.ShapeDtypeStruct((mlen, c2), F32),
                   jax.ShapeDtypeStruct((HALO_A * SUBLANES, c), F32), jax.ShapeDtypeStruct((8, c), F32)],
        scratch_shapes=[pltpu.VMEM((HALO_A + ts, c), F32), pltpu.VMEM((ts + HALO_A, c), F32),
                        pltpu.VMEM((HALO_B + ts, c), F32), pltpu.VMEM((ts + HALO_B, c), F32), shifted, shifted],
        input_output_aliases={n_in_before_dproj: 0},
        compiler_params=_params(),
    )(proj, proj, proj, proj, proj, proj, proj, proj, proj, proj, proj, a1, a1, dabo, dabo, dabo, kv, conv_a_w,
      ln_g, ln_b, conv_b_w, dproj)
    return dp, dkv, dwa.reshape(HALO_A, SUBLANES, c).sum(axis=1), misc


def _merge_fwd(abo, w_cat, proj, d):
    nb, s, c = abo.shape
    bm, bn = _tile(s, TILE_M), _tile(d, TILE_N // 2)
    gate_col0 = (proj.shape[1] - nb * d) // bn

    def body(a_ref, w_ref, g_ref, y_ref, m_ref, acc_ref):
        k = pl.program_id(2)
        y = jnp.dot(a_ref[...], w_ref[...], preferred_element_type=F32)
        y_ref[...] = y.astype(BF16)
        contrib = jax.nn.sigmoid(g_ref[...]) * y

        @pl.when(k == 0)
        def _():
            acc_ref[...] = contrib

        @pl.when(k > 0)
        def _():
            acc_ref[...] += contrib

        @pl.when(k == nb - 1)
        def _():
            m_ref[...] = acc_ref[...].astype(BF16)

    return pl.pallas_call(
        body, name="merge_fwd", grid=(s // bm, d // bn, nb),
        in_specs=[pl.BlockSpec((None, bm, c), lambda i, j, k: (k, i, 0)),
                  pl.BlockSpec((None, c, bn), lambda i, j, k: (k, 0, j)),
                  pl.BlockSpec((bm, bn), lambda i, j, k: (i, gate_col0 + k * (d // bn) + j))],
        out_specs=[pl.BlockSpec((None, bm, bn), lambda i, j, k: (k, i, j)),
                   pl.BlockSpec((bm, bn), lambda i, j, k: (i, j))],
        out_shape=[jax.ShapeDtypeStruct((nb, s, d), BF16), jax.ShapeDtypeStruct((s, d), BF16)],
        scratch_shapes=[pltpu.VMEM((bm, bn), F32)],
        compiler_params=_params(),
    )(abo, w_cat, proj)


def _merge_bwd(dz, w_o, y, proj):
    s, d = dz.shape
    nb = y.shape[0]
    nin = proj.shape[1]
    bm, bn = _tile(s, TILE_M), _tile(d, TILE_N // 2)
    gate_col0 = (nin - nb * d) // bn

    def body(dz_ref, w_ref, y_ref, g_ref, dy_ref, dg_ref, acc_ref):
        @pl.when(pl.program_id(2) == 0)
        def _():
            acc_ref[...] = lax.dot_general(dz_ref[...], w_ref[...], (((1,), (1,)), ((), ())),
                                           preferred_element_type=F32)

        dm = acc_ref[...]
        gt = jax.nn.sigmoid(g_ref[...])
        dy_ref[...] = (dm * gt).astype(BF16)
        dg_ref[...] = (dm * y_ref[...].astype(F32) * gt * (1.0 - gt)).astype(BF16)

    gate = lambda i, j, k: (i, gate_col0 + k * (d // bn) + j)
    return pl.pallas_call(
        body, name="merge_bwd", grid=(s // bm, d // bn, nb),
        in_specs=[pl.BlockSpec((bm, d), lambda i, j, k: (i, 0)), pl.BlockSpec((bn, d), lambda i, j, k: (j, 0)),
                  pl.BlockSpec((None, bm, bn), lambda i, j, k: (k, i, j)), pl.BlockSpec((bm, bn), gate)],
        out_specs=[pl.BlockSpec((None, bm, bn), lambda i, j, k: (k, i, j)), pl.BlockSpec((bm, bn), gate)],
        out_shape=[jax.ShapeDtypeStruct((nb, s, d), BF16), jax.ShapeDtypeStruct((s, nin), BF16)],
        scratch_shapes=[pltpu.VMEM((bm, bn), F32)],
        compiler_params=_params(),
    )(dz, w_o, y, proj)


def _window(ref, axis, who, length):
    idx = [slice(None)] * len(ref.shape)
    idx[axis] = pl.ds(pl.multiple_of(who * length, length), length)
    return ref.at[tuple(idx)]


def _all_gather(name, shards, axes):
    n = len(shards)
    out_shapes = []
    for sh, ax in zip(shards, axes):
        shp = list(sh.shape)
        shp[ax] *= N_DEV
        out_shapes.append(jax.ShapeDtypeStruct(tuple(shp), sh.dtype))

    def body(*refs):
        ins, outs = refs[:n], refs[n:2 * n]
        send, recv, local = refs[2 * n:]
        me = _my_index()
        lens = [ins[a].shape[axes[a]] for a in range(n)]
        mine = [pltpu.make_async_copy(ins[a], _window(outs[a], axes[a], me, lens[a]), local.at[a]) for a in range(n)]
        for cp in mine:
            cp.start()
        pushes = []
        for d in range(1, N_DEV):
            to = (me + d) % N_DEV
            for a in range(n):
                cp = pltpu.make_async_remote_copy(
                    src_ref=ins[a], dst_ref=_window(outs[a], axes[a], me, lens[a]), send_sem=send.at[a, d],
                    recv_sem=recv.at[a, d], device_id=_mesh_id(to), device_id_type=pl.DeviceIdType.MESH)
                cp.start()
                pushes.append(cp)
        for d in range(1, N_DEV):
            frm = (me + N_DEV - d) % N_DEV
            for a in range(n):
                pltpu.make_async_remote_copy(
                    src_ref=ins[a], dst_ref=_window(outs[a], axes[a], frm, lens[a]), send_sem=send.at[a, d],
                    recv_sem=recv.at[a, d], device_id=_mesh_id(frm), device_id_type=pl.DeviceIdType.MESH).wait_recv()
        for cp in pushes:
            cp.wait_send()
        for cp in mine:
            cp.wait()

    hbm = pl.BlockSpec(memory_space=pl.ANY)
    return pl.pallas_call(
        body, name=name, in_specs=[hbm] * n, out_specs=[hbm] * n, out_shape=out_shapes,
        scratch_shapes=[pltpu.SemaphoreType.DMA((n, N_DEV)), pltpu.SemaphoreType.DMA((n, N_DEV)),
                        pltpu.SemaphoreType.DMA((n,))],
    )(*shards)


def _pair(a, d):
    return a * N_DEV + d


def _push_ends(kind, src, land, axis, length, me, to):
    if kind == "gather":
        return src, _window(land, axis, me, length)
    return _window(src, axis, to, length), land.at[me]


def _arrival_ends(kind, src, land, axis, length, me, frm):
    if kind == "gather":
        return src, _window(land, axis, frm, length)
    return _window(src, axis, me, length), land.at[frm]


def _exchange_start(name, kind, srcs, axes, after):
    n = len(srcs)
    if kind == "gather":
        lens = [s.shape[ax] for s, ax in zip(srcs, axes)]
        land_shapes = [s.shape[:ax] + (s.shape[ax] * N_DEV,) + s.shape[ax + 1:] for s, ax in zip(srcs, axes)]
    else:
        lens = [s.shape[ax] // N_DEV for s, ax in zip(srcs, axes)]
        land_shapes = [(N_DEV,) + s.shape[:ax] + (ln,) + s.shape[ax + 1:] for s, ax, ln in zip(srcs, axes, lens)]
    lands = [lax.empty(shp, s.dtype) for shp, s in zip(land_shapes, srcs)]

    def body(*refs):
        ins = refs[:n]
        send, recv = refs[2 * n + 1], refs[2 * n + 2]
        lnd = refs[2 * n + 3 + n:2 * n + 3 + 2 * n]
        token = refs[-1]
        me = _my_index()
        for a in range(n):
            for d in range(1, N_DEV):
                to = (me + d) % N_DEV
                src, dst = _push_ends(kind, ins[a], lnd[a], axes[a], lens[a], me, to)
                pltpu.make_async_remote_copy(src_ref=src, dst_ref=dst, send_sem=send.at[_pair(a, d)],
                                             recv_sem=recv.at[_pair(a, d)], device_id=_mesh_id(to),
                                             device_id_type=pl.DeviceIdType.MESH).start()
        for a in range(n):
            pltpu.make_async_copy(*_push_ends(kind, ins[a], lnd[a], axes[a], lens[a], me, me),
                                  send.at[_pair(a, 0)]).start()
        token[...] = jnp.zeros_like(token)

    hbm = pl.BlockSpec(memory_space=pltpu.HBM)
    sem = pl.BlockSpec(memory_space=pltpu.SEMAPHORE)
    held = [pltpu.with_memory_space_constraint(t, pltpu.HBM) for t in list(srcs) + lands]
    outs = pl.pallas_call(
        body, name=name,
        in_specs=[hbm] * (2 * n) + [_UNTOUCHED],
        out_specs=[sem, sem] + [hbm] * (2 * n) + [pl.BlockSpec(memory_space=pltpu.VMEM)],
        out_shape=[pltpu.SemaphoreType.DMA((n * N_DEV,)), pltpu.SemaphoreType.DMA((n * N_DEV,))]
        + [pltpu.HBM(t.shape, t.dtype) for t in held] + [jax.ShapeDtypeStruct((8, 128), F32)],
        input_output_aliases={i: 2 + i for i in range(2 * n)},
        compiler_params=pltpu.CompilerParams(has_side_effects=pltpu.SideEffectType.DATAFLOW_SIDE_EFFECTING),
    )(*held, after)
    handle = dict(kind=kind, axes=axes, lens=lens, send=outs[0], recv=outs[1], srcs=outs[2:2 + n],
                  lands=outs[2 + n:2 + 2 * n])
    return handle, outs[-1]


def _exchange_wait(name, handle, after):
    kind, axes, lens = handle["kind"], handle["axes"], handle["lens"]
    srcs, lands = handle["srcs"], handle["lands"]
    n = len(srcs)
    n_in = 2 * n + 2 + len(after)

    def body(*refs):
        ins = refs[:n]
        send, recv = refs[2 * n], refs[2 * n + 1]
        got = refs[n_in + n:n_in + 2 * n]
        me = _my_index()
        for d in range(1, N_DEV):
            frm = (me + N_DEV - d) % N_DEV
            for a in range(n):
                src, dst = _arrival_ends(kind, ins[a], got[a], axes[a], lens[a], me, frm)
                pltpu.make_async_remote_copy(src_ref=src, dst_ref=dst, send_sem=send.at[_pair(a, d)],
                                             recv_sem=recv.at[_pair(a, d)], device_id=_mesh_id(frm),
                                             device_id_type=pl.DeviceIdType.MESH).wait_recv()
        for d in range(1, N_DEV):
            to = (me + d) % N_DEV
            for a in range(n):
                src, dst = _push_ends(kind, ins[a], got[a], axes[a], lens[a], me, to)
                pltpu.make_async_remote_copy(src_ref=src, dst_ref=dst, send_sem=send.at[_pair(a, d)],
                                             recv_sem=recv.at[_pair(a, d)], device_id=_mesh_id(to),
                                             device_id_type=pl.DeviceIdType.MESH).wait_send()
        for a in range(n):
            pltpu.make_async_copy(*_push_ends(kind, ins[a], got[a], axes[a], lens[a], me, me),
                                  send.at[_pair(a, 0)]).wait()

    hbm = pl.BlockSpec(memory_space=pltpu.HBM)
    sem = pl.BlockSpec(memory_space=pltpu.SEMAPHORE)
    outs = pl.pallas_call(
        body, name=name,
        in_specs=[hbm] * (2 * n) + [sem, sem] + [_UNTOUCHED] * len(after),
        out_specs=[hbm] * (2 * n),
        out_shape=[pltpu.HBM(t.shape, t.dtype) for t in list(srcs) + list(lands)],
        input_output_aliases={i: i for i in range(2 * n)},
        compiler_params=pltpu.CompilerParams(has_side_effects=pltpu.SideEffectType.DATAFLOW_SIDE_EFFECTING),
    )(*srcs, *lands, handle["send"], handle["recv"], *after)
    return outs[n:]


def _all_reduce_small(packed):
    r, c = packed.shape

    def body(p_ref, o_ref, buf, send, recv):
        me = _my_index()
        buf[me] = p_ref[...]
        pushes = []
        for d in range(1, N_DEV):
            to = (me + d) % N_DEV
            cp = pltpu.make_async_remote_copy(src_ref=p_ref, dst_ref=buf.at[me], send_sem=send.at[d],
                                              recv_sem=recv.at[d], device_id=_mesh_id(to),
                                              device_id_type=pl.DeviceIdType.MESH)
            cp.start()
            pushes.append(cp)
        for d in range(1, N_DEV):
            frm = (me + N_DEV - d) % N_DEV
            pltpu.make_async_remote_copy(src_ref=p_ref, dst_ref=buf.at[frm], send_sem=send.at[d], recv_sem=recv.at[d],
                                         device_id=_mesh_id(frm), device_id_type=pl.DeviceIdType.MESH).wait_recv()
        for cp in pushes:
            cp.wait_send()
        acc = buf[0]
        for s in range(1, N_DEV):
            acc = acc + buf[s]
        o_ref[...] = acc

    vmem = pl.BlockSpec(memory_space=pltpu.VMEM)
    return pl.pallas_call(
        body, name="all_reduce_small", in_specs=[vmem], out_specs=vmem, out_shape=jax.ShapeDtypeStruct((r, c), F32),
        scratch_shapes=[pltpu.VMEM((N_DEV, r, c), F32), pltpu.SemaphoreType.DMA((N_DEV,)),
                        pltpu.SemaphoreType.DMA((N_DEV,))],
        compiler_params=_params(),
    )(packed)


def _layer_fwd_branches(x, mem, sm, wf, after):
    h = _rmsnorm("rms_mix_pre", x, sm["g_mix_pre"], after)
    proj = _mm_nn("proj", h, wf["w_in"], F32)
    mem_n = _rmsnorm("rms_mem", mem, sm["g_mem"], mem)
    kv = _mm_nn("kv", mem_n, wf["w_kv"], BF16)
    a1, abo = _branch_fwd(proj, kv, wf["conv_a_w"], sm["conv_a_b"], sm["ln_a_g"], sm["ln_a_b"], wf["conv_b_w"])
    return dict(x=x, h=h, proj=proj, mem_n=mem_n, kv=kv, a1=a1, abo=abo)


def _layer_fwd_rest(sm, wf, sv):
    x = sv["x"]
    y, merged = _merge_fwd(sv["abo"], wf["w_cat"], sv["proj"], x.shape[1])
    z, x1 = _mm_res_norm("mix_out", merged, wf["w_o"], x, sm["g_mix_post"])
    h2 = _rmsnorm("rms_mlp_pre", x1, sm["g_mlp_pre"], x1)
    r, act = _mm_relu2("mlp_up", h2, wf["w_up"])
    f, x2 = _mm_res_norm("mlp_down", act, wf["w_down"], x1, sm["g_mlp_post"])
    sv.update(y=y, merged=merged, z=z, x1=x1, h2=h2, r=r, act=act, f=f)
    return x2


def _pad_rows(t):
    return jnp.pad(t, ((0, (-t.shape[0]) % 8), (0, 0)))


def _layer_bwd(dx2, mem, sm, wf, sv, send_off):
    c = sv["a1"].shape[1]
    df, dg_mlp_post = _rms_bwd("rms_bwd_post", dx2, sv["f"], sm["g_mlp_post"], BF16, dx2)
    d_up = _mm_relu2_bwd("mlp_down_bwd", df, wf["w_down"], sv["r"])
    gw_down = _mm_tn("gw_down", sv["act"], df, BF16)
    dx1, dg_mlp_pre = _mm_rms_bwd("mlp_up_bwd", d_up, wf["w_up"], sv["x1"], sm["g_mlp_pre"], dx2, dx2)
    gw_up = _mm_tn("gw_up", sv["h2"], d_up, BF16)
    dz, dg_mix_post = _rms_bwd("rms_bwd_post", dx1, sv["z"], sm["g_mix_post"], BF16, dx1)
    dy, dproj = _merge_bwd(dz, wf["w_o"], sv["y"], sv["proj"])
    gw_o = _mm_tn("gw_o", sv["merged"], dz, BF16)
    dabo = _mm_nt("branch_out_bwd", dy, wf["w_cat"], F32)
    gw_cat = _mm_tn("gw_branch_out", sv["abo"], dy, BF16)
    dproj, dkv, dconv_a_w, misc = _branch_bwd(sv["proj"], sv["a1"], dabo, sv["kv"], wf["conv_a_w"], sm["ln_a_g"],
                                              sm["ln_a_b"], wf["conv_b_w"], dproj)
    dkv = dkv.astype(BF16)
    gw_kv = _mm_tn("gw_kv", sv["mem_n"], dkv, BF16)
    dmem_n = _mm_nt("kv_bwd", dkv, wf["w_kv"], F32)
    _, dg_mem = _rms_bwd("rms_bwd_mem", dmem_n, mem, sm["g_mem"], BF16, dmem_n)
    gw_in = _mm_tn("gw_in", sv["h"], dproj, BF16)
    sent = send_off(dict(w_in=gw_in, w_cat=gw_cat, w_kv=gw_kv, w_o=gw_o, w_up=gw_up, w_down=gw_down))
    dx, dg_mix_pre = _mm_rms_bwd("proj_bwd", dproj, wf["w_in"], sv["x"], sm["g_mix_pre"], dx1, sent)
    rows = [_pad_rows(t.reshape(2, c)) for t in (dg_mix_pre, dg_mem, dg_mix_post, dg_mlp_pre, dg_mlp_post)]
    return dx, jnp.concatenate(rows + [misc, dconv_a_w], axis=0)


_SMALL_D_NAMES = ["g_mix_pre", "g_mem", "g_mix_post", "g_mlp_pre", "g_mlp_post"]
_SMALL_C_NAMES = ["conv_a_b", "ln_a_g", "ln_a_b"]
_SMALL_MISC_ROW = 8 * len(_SMALL_D_NAMES)
_SMALL_CONV_B_ROW = _SMALL_MISC_ROW + len(_SMALL_C_NAMES)
_SMALL_CONV_A_ROW = _SMALL_MISC_ROW + 8
_SMALL_ROWS = _SMALL_CONV_A_ROW + HALO_A


def kernel(x, mem, g_mix_pre, w_in, conv_a_w, conv_a_b, ln_a_g, ln_a_b, w_a_out, conv_b_w, w_b_out, g_mem, w_kv, w_x_out, w_o, g_mix_post, g_mlp_pre, w_up, w_down, g_mlp_post, loss_target, m_g_mix_pre, m_w_in, m_conv_a_w, m_conv_a_b, m_ln_a_g, m_ln_a_b, m_w_a_out, m_conv_b_w, m_w_b_out, m_g_mem, m_w_kv, m_w_x_out, m_w_o, m_g_mix_post, m_g_mlp_pre, m_w_up, m_w_down, m_g_mlp_post, v_g_mix_pre, v_w_in, v_conv_a_w, v_conv_a_b, v_ln_a_g, v_ln_a_b, v_w_a_out, v_conv_b_w, v_w_b_out, v_g_mem, v_w_kv, v_w_x_out, v_w_o, v_g_mix_post, v_g_mlp_pre, v_w_up, v_w_down, v_g_mlp_post):
    names = ["g_mix_pre", "w_in", "conv_a_w", "conv_a_b", "ln_a_g", "ln_a_b", "w_a_out", "conv_b_w", "w_b_out",
             "g_mem", "w_kv", "w_x_out", "w_o", "g_mix_post", "g_mlp_pre", "w_up", "w_down", "g_mlp_post"]
    w = dict(zip(names, [g_mix_pre, w_in, conv_a_w, conv_a_b, ln_a_g, ln_a_b, w_a_out, conv_b_w, w_b_out, g_mem,
                         w_kv, w_x_out, w_o, g_mix_post, g_mlp_pre, w_up, w_down, g_mlp_post]))
    mo = dict(zip(names, [m_g_mix_pre, m_w_in, m_conv_a_w, m_conv_a_b, m_ln_a_g, m_ln_a_b, m_w_a_out, m_conv_b_w,
                          m_w_b_out, m_g_mem, m_w_kv, m_w_x_out, m_w_o, m_g_mix_post, m_g_mlp_pre, m_w_up, m_w_down,
                          m_g_mlp_post]))
    vo = dict(zip(names, [v_g_mix_pre, v_w_in, v_conv_a_w, v_conv_a_b, v_ln_a_g, v_ln_a_b, v_w_a_out, v_conv_b_w,
                          v_w_b_out, v_g_mem, v_w_kv, v_w_x_out, v_w_o, v_g_mix_post, v_g_mlp_pre, v_w_up, v_w_down,
                          v_g_mlp_post]))
    depth = w_in.shape[0]
    c = conv_a_b.shape[1]
    cs = conv_a_w.shape[2]
    me = _my_index()
    xs, mems, tgt = x[0], mem[0], loss_target[0]

    big_names = ["w_in", "w_kv", "w_cat", "w_o", "w_up", "w_down"]
    big_axes = [1, 0, 2, 0, 1, 0]
    n_early = 2
    branch_names = ["w_a_out", "w_b_out", "w_x_out"]
    smalls = [{k: w[k][l][None, :] for k in _SMALL_D_NAMES + _SMALL_C_NAMES} for l in range(depth)]

    def weight_shards(l):
        return [w_in[l].astype(BF16), w_kv[l].astype(BF16), jnp.stack([w[k][l] for k in branch_names]).astype(BF16),
                w_o[l].astype(BF16), w_up[l].astype(BF16), w_down[l].astype(BF16)]

    def start_gather(l, after):
        shards = weight_shards(l)
        early, token = _exchange_start(f"gather_start_{l}_early", "gather", shards[:n_early], big_axes[:n_early],
                                       after)
        late, token = _exchange_start(f"gather_start_{l}_late", "gather", shards[n_early:], big_axes[n_early:], token)
        return early, late, token

    taps_a, taps_b = _all_gather(
        "gather_conv_taps",
        [jnp.pad(conv_a_w, ((0, 0), (0, HALO_A - CONV_A_K), (0, 0))),
         jnp.pad(conv_b_w, ((0, 0), (0, HALO_B - CONV_B_K), (0, 0)))], [2, 2])

    early, late, _ = start_gather(0, taps_a)
    fulls = [None] * depth
    saved = []
    xc = xs
    for l in range(depth):
        fulls[l] = dict(zip(big_names[:n_early], _exchange_wait(f"gather_wait_{l}_early", early, [xc])))
        fulls[l]["conv_a_w"], fulls[l]["conv_b_w"] = taps_a[l], taps_b[l]
        token, mine_late = fulls[l]["w_in"], late
        if l + 1 < depth:
            early, late, token = start_gather(l + 1, fulls[l]["w_in"])
        sv = _layer_fwd_branches(xc, mems, smalls[l], fulls[l], token)
        fulls[l].update(zip(big_names[n_early:], _exchange_wait(f"gather_wait_{l}_late", mine_late, [sv["abo"]])))
        xc = _layer_fwd_rest(smalls[l], fulls[l], sv)
        saved.append(sv)
    loss_part, dx = _loss_head(xc, tgt)
    loss = lax.psum(loss_part, MESH_AXES)

    upd = {}

    def adamw_layer(l, slots, after):
        for k, sl in zip(big_names, slots):
            if k == "w_cat":
                for b, nm in enumerate(branch_names):
                    upd[nm] = _adamw("adamw_" + nm, sl, b, w[nm], mo[nm], vo[nm], l, upd.get(nm), after)
            else:
                upd[k] = _adamw("adamw_" + k, sl, None, w[k], mo[k], vo[k], l, upd.get(k), after)

    small_parts = [None] * depth
    in_flight = [None]

    def send_off_layer(l):
        def send_off(big):
            grads = [big[k] for k in big_names]
            slots = None
            if in_flight[0] is not None:
                slots = _exchange_wait(f"scatter_wait_{l + 1}", in_flight[0], [grads[0]])
            in_flight[0], token = _exchange_start(f"scatter_start_{l}", "scatter", grads, big_axes,
                                                  grads[0] if slots is None else slots[0])
            if slots is not None:
                adamw_layer(l + 1, slots, token)
            return token
        return send_off

    for l in reversed(range(depth)):
        dx, small_parts[l] = _layer_bwd(dx, mems, smalls[l], fulls[l], saved[l], send_off_layer(l))

    tot = _all_reduce_small(jnp.concatenate(small_parts, axis=0)).reshape(depth, _SMALL_ROWS, c)
    g_small = {}
    for n_, nm in enumerate(_SMALL_D_NAMES):
        g_small[nm] = tot[:, 8 * n_:8 * n_ + 2, :].reshape(depth, 2 * c)
    for n_, nm in enumerate(_SMALL_C_NAMES):
        g_small[nm] = tot[:, _SMALL_MISC_ROW + n_, :]
    g_taps = {"conv_a_w": tot[:, _SMALL_CONV_A_ROW:_SMALL_CONV_A_ROW + CONV_A_K, :],
              "conv_b_w": tot[:, _SMALL_CONV_B_ROW:_SMALL_CONV_B_ROW + CONV_B_K, :]}

    def pack(t):
        return jnp.concatenate([_pad_rows(t[nm].reshape(-1, c)) for nm in _SMALL_D_NAMES + _SMALL_C_NAMES], axis=0)

    res = _adamw("adamw_small", pack(g_small)[None], None, pack(w)[None], pack(mo)[None], pack(vo)[None], 0, None, tot)
    row = 0
    for nm in _SMALL_D_NAMES + _SMALL_C_NAMES:
        n_rows = w[nm].size // c
        upd[nm] = [t[0, row:row + n_rows].reshape(w[nm].shape) for t in res]
        row += n_rows + (-n_rows) % 8
    for nm, g in g_taps.items():
        shp = w[nm].shape
        mine = lax.dynamic_slice_in_dim(g, me * cs, cs, axis=2).reshape(1, -1, cs)
        res = _adamw("adamw_" + nm, mine, None, w[nm].reshape(1, -1, cs), mo[nm].reshape(1, -1, cs),
                     vo[nm].reshape(1, -1, cs), 0, None, tot)
        upd[nm] = [t.reshape(shp) for t in res]
    done = [u[0] for u in upd.values()]
    adamw_layer(0, _exchange_wait("scatter_wait_0", in_flight[0], done + [dx]), tot)

    return (loss, dx[None], *[upd[nm][0] for nm in names], *[upd[nm][1] for nm in names],
            *[upd[nm][2] for nm in names], *[upd[nm][3] for nm in names])
```

```python
import math

import jax
import jax.numpy as jnp
from jax import lax
from jax.experimental import pallas as pl
from jax.experimental.pallas import tpu as pltpu

F32 = jnp.float32
BF16 = jnp.bfloat16

EPS = 1e-6
N_HEADS = 4
N_BRANCH = 3
CONV_A_K = 31
CONV_B_K = 3
HALO_A = 32
HALO_B = 8
N_DEV = 8
MESH_AXES = ("x", "y", "c")

ADAM_LR = 0.001
ADAM_B1 = 0.9
ADAM_B2 = 0.999
ADAM_EPS = 1e-08
ADAM_WD = 0.01
ADAM_STEP = 10

V7X_VMEM_BYTES = 64 * 1024 * 1024
VMEM_LIMIT = V7X_VMEM_BYTES - 8 * 1024 * 1024

TILE_M = 1024
TILE_N = 1024
TILE_K = 2048
TILE_ROWS_WIDE = 512
TILE_ROWS_BRANCH = 256
TILE_ROWS_BRANCH_BWD = 128


def _params():
    return pltpu.CompilerParams(vmem_limit_bytes=VMEM_LIMIT)


def _tile(n, t):
    t = min(n, t)
    assert n % t == 0, (n, t)
    return t


def _my_index():
    return 4 * lax.axis_index("x") + 2 * lax.axis_index("y") + lax.axis_index("c")


def _mesh_id(p):
    return (p // 4, (p // 2) % 2, p % 2)


def _mm(name, grid, a, a_spec, b, b_spec, *, ta, tb, acc_shape, extras, outs, epilogue):
    nk = grid[-1]
    lead = len(grid) - 3
    ne, no = len(extras), len(outs)
    dn = (((0,) if ta else (1,), (1,) if tb else (0,)), ((), ()))

    def body(a_ref, b_ref, *rest):
        extra_refs = rest[:ne]
        out_refs = rest[ne:ne + no]
        i, j, k = pl.program_id(lead), pl.program_id(lead + 1), pl.program_id(lead + 2)
        prod = lax.dot_general(a_ref[...], b_ref[...], dn, preferred_element_type=F32)
        if nk == 1:
            epilogue(prod, extra_refs, out_refs, i, j)
        else:
            acc_ref = rest[ne + no]

            @pl.when(k == 0)
            def _():
                acc_ref[...] = prod

            @pl.when(k > 0)
            def _():
                acc_ref[...] += prod

            @pl.when(k == nk - 1)
            def _():
                epilogue(acc_ref[...], extra_refs, out_refs, i, j)

    return pl.pallas_call(
        body, name=name, grid=grid,
        in_specs=[a_spec, b_spec] + [s for _, s in extras],
        out_specs=[s for _, s in outs],
        out_shape=[o for o, _ in outs],
        scratch_shapes=[pltpu.VMEM(acc_shape, F32)] if nk > 1 else [],
        compiler_params=_params(),
    )(a, b, *[e for e, _ in extras])


def _store_epilogue(res, extra_refs, out_refs, i, j):
    out_refs[0][...] = res.astype(out_refs[0].dtype)


def _mm_nn(name, a, b, out_dtype):
    m, kd = a.shape
    n = b.shape[1]
    bm, bn, bk = _tile(m, TILE_M), _tile(n, TILE_N), _tile(kd, TILE_K)
    return _mm(name, (m // bm, n // bn, kd // bk), a, pl.BlockSpec((bm, bk), lambda i, j, k: (i, k)),
               b, pl.BlockSpec((bk, bn), lambda i, j, k: (k, j)), ta=False, tb=False, acc_shape=(bm, bn), extras=[],
               outs=[(jax.ShapeDtypeStruct((m, n), out_dtype), pl.BlockSpec((bm, bn), lambda i, j, k: (i, j)))],
               epilogue=_store_epilogue)[0]


def _mm_nt(name, a, b, out_dtype, after=None):
    m, kd = a.shape[-2:]
    n = b.shape[-2]
    bm, bn = _tile(m, TILE_M), _tile(n, TILE_N)
    if a.ndim == 2:
        bk = _tile(kd, TILE_K)
        return _mm(name, (m // bm, n // bn, kd // bk), a, pl.BlockSpec((bm, bk), lambda i, j, k: (i, k)),
                   b, pl.BlockSpec((bn, bk), lambda i, j, k: (j, k)), ta=False, tb=True, acc_shape=(bm, bn),
                   extras=[] if after is None else [(after, _UNTOUCHED)],
                   outs=[(jax.ShapeDtypeStruct((m, n), out_dtype), pl.BlockSpec((bm, bn), lambda i, j, k: (i, j)))],
                   epilogue=_store_epilogue)[0]
    nb = a.shape[0]
    return _mm(name, (nb, m // bm, n // bn, 1), a, pl.BlockSpec((None, bm, kd), lambda s, i, j, k: (s, i, 0)),
               b, pl.BlockSpec((None, bn, kd), lambda s, i, j, k: (s, j, 0)), ta=False, tb=True, acc_shape=(bm, bn),
               extras=[],
               outs=[(jax.ShapeDtypeStruct((nb, m, n), out_dtype),
                      pl.BlockSpec((None, bm, bn), lambda s, i, j, k: (s, i, j)))],
               epilogue=_store_epilogue)[0]


def _mm_tn(name, a, b, out_dtype):
    r, m = a.shape[-2:]
    n = b.shape[-1]
    bm, bn, bk = _tile(m, TILE_M), _tile(n, TILE_N), _tile(r, TILE_K)
    if a.ndim == 2:
        return _mm(name, (m // bm, n // bn, r // bk), a, pl.BlockSpec((bk, bm), lambda i, j, k: (k, i)),
                   b, pl.BlockSpec((bk, bn), lambda i, j, k: (k, j)), ta=True, tb=False, acc_shape=(bm, bn), extras=[],
                   outs=[(jax.ShapeDtypeStruct((m, n), out_dtype), pl.BlockSpec((bm, bn), lambda i, j, k: (i, j)))],
                   epilogue=_store_epilogue)[0]
    nb = a.shape[0]
    return _mm(name, (nb, m // bm, n // bn, r // bk), a, pl.BlockSpec((None, bk, bm), lambda s, i, j, k: (s, k, i)),
               b, pl.BlockSpec((None, bk, bn), lambda s, i, j, k: (s, k, j)), ta=True, tb=False, acc_shape=(bm, bn),
               extras=[],
               outs=[(jax.ShapeDtypeStruct((nb, m, n), out_dtype),
                      pl.BlockSpec((None, bm, bn), lambda s, i, j, k: (s, i, j)))],
               epilogue=_store_epilogue)[0]


def _mm_relu2(name, h, w):
    m, kd = h.shape
    n = w.shape[1]
    bm, bn = _tile(m, TILE_M), _tile(n, TILE_N)

    def epilogue(res, extra_refs, out_refs, i, j):
        r = jnp.maximum(res, 0.0)
        out_refs[0][...] = r.astype(BF16)
        out_refs[1][...] = (r * r).astype(BF16)

    o = jax.ShapeDtypeStruct((m, n), BF16)
    spec = pl.BlockSpec((bm, bn), lambda i, j, k: (i, j))
    return _mm(name, (m // bm, n // bn, 1), h, pl.BlockSpec((bm, kd), lambda i, j, k: (i, 0)),
               w, pl.BlockSpec((kd, bn), lambda i, j, k: (0, j)), ta=False, tb=False, acc_shape=(bm, bn), extras=[],
               outs=[(o, spec), (o, spec)], epilogue=epilogue)


def _mm_relu2_bwd(name, df, w_down, r):
    m, kd = df.shape
    n = w_down.shape[0]
    bm, bn = _tile(m, TILE_M), _tile(n, TILE_N)

    def epilogue(res, extra_refs, out_refs, i, j):
        out_refs[0][...] = (res * (2.0 * extra_refs[0][...].astype(F32))).astype(BF16)

    spec = pl.BlockSpec((bm, bn), lambda i, j, k: (i, j))
    return _mm(name, (m // bm, n // bn, 1), df, pl.BlockSpec((bm, kd), lambda i, j, k: (i, 0)),
               w_down, pl.BlockSpec((bn, kd), lambda i, j, k: (j, 0)), ta=False, tb=True, acc_shape=(bm, bn),
               extras=[(r, spec)], outs=[(jax.ShapeDtypeStruct((m, n), BF16), spec)], epilogue=epilogue)[0]


def _rms_bwd_math(dy, x, g):
    rr = lax.rsqrt(jnp.mean(x * x, axis=-1, keepdims=True) + EPS)
    gy = dy * g
    dx = rr * gy - x * (rr * rr * rr * jnp.mean(x * gy, axis=-1, keepdims=True))
    dg_rows = dy * x * rr
    return dx, dg_rows


_UNTOUCHED = pl.BlockSpec(memory_space=pl.ANY)


def _rmsnorm(name, x, g, after):
    r, d = x.shape
    br = _tile(r, TILE_ROWS_WIDE)

    def body(x_ref, g_ref, after_ref, o_ref):
        xv = x_ref[...]
        rr = lax.rsqrt(jnp.mean(xv * xv, axis=-1, keepdims=True) + EPS)
        o_ref[...] = ((xv * rr) * g_ref[...]).astype(BF16)

    row = pl.BlockSpec((br, d), lambda i: (i, 0))
    return pl.pallas_call(body, name=name, grid=(r // br,),
                          in_specs=[row, pl.BlockSpec((1, d), lambda i: (0, 0)), _UNTOUCHED],
                          out_specs=row, out_shape=jax.ShapeDtypeStruct((r, d), BF16),
                          compiler_params=_params())(x, g, after)


def _res_norm(name, z, x, g):
    r, d = x.shape
    br = _tile(r, TILE_ROWS_WIDE)

    def body(z_ref, x_ref, g_ref, o_ref):
        zv = z_ref[...]
        rr = lax.rsqrt(jnp.mean(zv * zv, axis=-1, keepdims=True) + EPS)
        o_ref[...] = x_ref[...] + (zv * rr) * g_ref[...]

    row = pl.BlockSpec((br, d), lambda i: (i, 0))
    return pl.pallas_call(body, name=name, grid=(r // br,),
                          in_specs=[row, row, pl.BlockSpec((1, d), lambda i: (0, 0))],
                          out_specs=row, out_shape=jax.ShapeDtypeStruct((r, d), F32),
                          compiler_params=_params())(z, x, g)


def _rms_bwd(name, dy, x, g, out_dtype, after, residual=None):
    r, d = x.shape
    br = _tile(r, TILE_ROWS_WIDE)
    n_res = 0 if residual is None else 1

    def body(dy_ref, x_ref, g_ref, after_ref, *rest):
        dx_ref, dg_ref = rest[n_res:]
        dx, dg_rows = _rms_bwd_math(dy_ref[...], x_ref[...], g_ref[...])
        if n_res:
            dx = dx + rest[0][...]
        dx_ref[...] = dx.astype(out_dtype)

        @pl.when(pl.program_id(0) == 0)
        def _():
            dg_ref[...] = jnp.zeros_like(dg_ref)

        dg_ref[...] += jnp.sum(dg_rows, axis=0, keepdims=True)

    row = pl.BlockSpec((br, d), lambda i: (i, 0))
    vec = pl.BlockSpec((1, d), lambda i: (0, 0))
    return pl.pallas_call(body, name=name, grid=(r // br,), in_specs=[row, row, vec, _UNTOUCHED] + [row] * n_res,
                          out_specs=[row, vec],
                          out_shape=[jax.ShapeDtypeStruct((r, d), out_dtype), jax.ShapeDtypeStruct((1, d), F32)],
                          compiler_params=_params())(dy, x, g, after, *([] if residual is None else [residual]))


def _loss_head(y, target):
    r, d = y.shape
    br = _tile(r, TILE_ROWS_WIDE)

    def body(y_ref, t_ref, dy_ref, l_ref):
        diff = y_ref[...] - t_ref[...]
        dy_ref[...] = diff * (1.0 / d)

        @pl.when(pl.program_id(0) == 0)
        def _():
            l_ref[...] = jnp.zeros_like(l_ref)

        l_ref[...] += 0.5 * jnp.sum(jnp.mean(diff * diff, axis=-1, keepdims=True))

    row = pl.BlockSpec((br, d), lambda i: (i, 0))
    one = pl.BlockSpec((8, 128), lambda i: (0, 0))
    dy, l = pl.pallas_call(body, name="loss_head", grid=(r // br,), in_specs=[row, row], out_specs=[row, one],
                           out_shape=[jax.ShapeDtypeStruct((r, d), F32), jax.ShapeDtypeStruct((8, 128), F32)],
                           compiler_params=_params())(y, target)
    return l[0, 0], dy


def _adamw(name, parts, branch, w, m, v, layer, prev, after):
    _, r, c = w.shape
    p = parts.shape[0]
    br = r
    for cand in (512, 256, 128, 64, 32, 16):
        if r % cand == 0 and cand * c * 4 <= 2 * 1024 * 1024:
            br = cand
            break
    bc1 = 1.0 - ADAM_B1 ** ADAM_STEP
    bc2 = 1.0 - ADAM_B2 ** ADAM_STEP
    if prev is None:
        prev = [lax.empty(w.shape, F32) for _ in range(4)]

    def body(p_ref, w_ref, m_ref, v_ref, after_ref, pg, pd, pm, pv, g_out, d_out, m_out, v_out):
        g = p_ref[0].astype(F32)
        for s in range(1, p):
            g = g + p_ref[s].astype(F32)
        m2 = ADAM_B1 * m_ref[...] + (1.0 - ADAM_B1) * g
        v2 = ADAM_B2 * v_ref[...] + (1.0 - ADAM_B2) * (g * g)
        m_hat = m2 / bc1
        v_hat = v2 / bc2
        g_out[...] = g
        d_out[...] = -ADAM_LR * (m_hat / (jnp.sqrt(v_hat) + ADAM_EPS) + ADAM_WD * w_ref[...])
        m_out[...] = m2
        v_out[...] = v2

    if branch is None:
        p_spec = pl.BlockSpec((p, br, c), lambda i: (0, i, 0))
    else:
        p_spec = pl.BlockSpec((p, None, br, c), lambda i: (0, branch, i, 0))
    slab = pl.BlockSpec((None, br, c), lambda i: (layer, i, 0))
    o = jax.ShapeDtypeStruct(w.shape, F32)
    return pl.pallas_call(body, name=name, grid=(r // br,),
                          in_specs=[p_spec, slab, slab, slab, _UNTOUCHED] + [_UNTOUCHED] * 4,
                          out_specs=[slab] * 4, out_shape=[o] * 4, input_output_aliases={5: 0, 6: 1, 7: 2, 8: 3},
                          compiler_params=_params())(parts, w, m, v, after, *prev)


def _layer_norm_parts(a1, ln_g, ln_b):
    mu = jnp.mean(a1, axis=-1, keepdims=True)
    xc = a1 - mu
    rstd = lax.rsqrt(jnp.mean(xc * xc, axis=-1, keepdims=True) + EPS)
    xhat = xc * rstd
    a2 = xhat * ln_g + ln_b
    return xhat, rstd, a2


def _softmax_rows(s):
    e = jnp.exp(s - jnp.max(s, axis=-1, keepdims=True))
    return e / jnp.sum(e, axis=-1, keepdims=True)


SUBLANES = 8


def _shift_copies(ext_ref, sh_ref, rows):
    for r in range(1, SUBLANES):
        sh_ref[r - 1, pl.ds(0, rows), :] = ext_ref[pl.ds(r, rows), :]


def _rows_at(ext_ref, sh_ref, offset, ts):
    q, r = divmod(offset, SUBLANES)
    if r == 0:
        return ext_ref[pl.ds(SUBLANES * q, ts), :]
    return sh_ref[r - 1, pl.ds(SUBLANES * q, ts), :]


def _branch_specs(ts, c):
    def col(ci):
        return pl.BlockSpec((ts, c), lambda i: (i, ci))

    def prev(ci, h):
        return pl.BlockSpec((h, c), lambda i: (jnp.maximum(i * (ts // h) - 1, 0), ci))

    return col, prev


def _branch_fwd(proj, kv, conv_a_w, conv_a_b, ln_g, ln_b, conv_b_w):
    s = proj.shape[0]
    mlen, c2 = kv.shape
    c = c2 // 2
    hd = c // N_HEADS
    ts = _tile(s, TILE_ROWS_BRANCH)
    scale = hd ** -0.5
    col, prev = _branch_specs(ts, c)

    def body(av, ag, sb, sc, sx, q, hav, hag, hsc, hsx, kv_ref, caw, cab, lng, lnb, cbw, a1_ref, abo_ref, exta, extb,
             sha):
        not_first = (pl.program_id(0) > 0).astype(F32)
        exta[pl.ds(0, HALO_A), :] = hav[...] * jax.nn.sigmoid(hag[...]) * not_first
        exta[pl.ds(HALO_A, ts), :] = av[...] * jax.nn.sigmoid(ag[...])
        _shift_copies(exta, sha, ts + HALO_A - SUBLANES)
        acc = jnp.broadcast_to(cab[...], (ts, c))
        for k in range(CONV_A_K):
            acc = acc + caw[pl.ds(k, 1), :] * _rows_at(exta, sha, HALO_A - (CONV_A_K - 1) + k, ts)
        a1_ref[...] = acc
        _, _, a2 = _layer_norm_parts(acc, lng[...], lnb[...])
        abo_ref[0] = (a2 * jax.nn.sigmoid(a2)).astype(BF16)
        extb[pl.ds(0, HALO_B), :] = hsc[...] * hsx[...] * not_first
        extb[pl.ds(HALO_B, ts), :] = sc[...] * sx[...]
        u = cbw[pl.ds(0, 1), :] * extb[pl.ds(HALO_B - (CONV_B_K - 1), ts), :]
        for k in range(1, CONV_B_K):
            u = u + cbw[pl.ds(k, 1), :] * extb[pl.ds(HALO_B - (CONV_B_K - 1) + k, ts), :]
        abo_ref[1] = (sb[...] * u).astype(BF16)
        for h in range(N_HEADS):
            qh = q[:, h * hd:(h + 1) * hd].astype(BF16)
            kh = kv_ref[:, h * hd:(h + 1) * hd]
            vh = kv_ref[:, c + h * hd:c + (h + 1) * hd]
            sc_ = lax.dot_general(qh, kh, (((1,), (1,)), ((), ())), preferred_element_type=F32) * scale
            p = _softmax_rows(sc_).astype(BF16)
            abo_ref[2, :, h * hd:(h + 1) * hd] = jnp.dot(p, vh, preferred_element_type=F32).astype(BF16)

    full = lambda shp: pl.BlockSpec(shp, lambda i: (0,) * len(shp))
    return pl.pallas_call(
        body, name="branch_fwd", grid=(s // ts,),
        in_specs=[col(0), col(1), col(2), col(3), col(4), col(5), prev(0, HALO_A), prev(1, HALO_A),
                  prev(3, HALO_B), prev(4, HALO_B), full((mlen, c2)), full(conv_a_w.shape), full((1, c)),
                  full((1, c)), full((1, c)), full(conv_b_w.shape)],
        out_specs=[pl.BlockSpec((ts, c), lambda i: (i, 0)), pl.BlockSpec((N_BRANCH, ts, c), lambda i: (0, i, 0))],
        out_shape=[jax.ShapeDtypeStruct((s, c), F32), jax.ShapeDtypeStruct((N_BRANCH, s, c), BF16)],
        scratch_shapes=[pltpu.VMEM((HALO_A + ts, c), F32), pltpu.VMEM((HALO_B + ts, c), F32),
                        pltpu.VMEM((SUBLANES - 1, ts + HALO_A - SUBLANES, c), F32)],
        compiler_params=_params(),
    )(proj, proj, proj, proj, proj, proj, proj, proj, proj, proj, kv, conv_a_w, conv_a_b, ln_g, ln_b, conv_b_w)


def _branch_bwd(proj, a1, dabo, kv, conv_a_w, ln_g, ln_b, conv_b_w, dproj):
    s = proj.shape[0]
    mlen, c2 = kv.shape
    c = c2 // 2
    hd = c // N_HEADS
    ts = _tile(s, TILE_ROWS_BRANCH_BWD)
    nt = s // ts
    scale = hd ** -0.5
    col, prev = _branch_specs(ts, c)

    def nxt(h, lead=None, ci=0):
        if lead is None:
            return pl.BlockSpec((h, c), lambda i: (jnp.minimum((i + 1) * (ts // h), s // h - 1), ci))
        return pl.BlockSpec((None, h, c), lambda i: (lead, jnp.minimum((i + 1) * (ts // h), s // h - 1), 0))

    def body(av, ag, sb, sc, sx, q, hav, hag, hsc, hsx, nsb, a1_ref, na1, dabo_ref, nda, ndb, kv_ref, caw, lng, lnb,
             cbw, dproj_in, dp_ref, dkv_ref, dwa_ref, misc_ref, exta, extda, extb, extdu, sha, shda):
        i = pl.program_id(0)
        not_first = (i > 0).astype(F32)
        not_last = (i < nt - 1).astype(F32)

        @pl.when(i == 0)
        def _():
            dkv_ref[...] = jnp.zeros_like(dkv_ref)
            dwa_ref[...] = jnp.zeros_like(dwa_ref)
            misc_ref[...] = jnp.zeros_like(misc_ref)

        def rowsum(t):
            return jnp.sum(t, axis=0, keepdims=True)

        def da1_of(a1v, da3):
            xhat, rstd, a2 = _layer_norm_parts(a1v, lng[...], lnb[...])
            sg = jax.nn.sigmoid(a2)
            da2 = da3 * (sg * (1.0 + a2 * (1.0 - sg)))
            dxh = da2 * lng[...]
            da1 = rstd * (dxh - jnp.mean(dxh, axis=-1, keepdims=True)
                          - xhat * jnp.mean(dxh * xhat, axis=-1, keepdims=True))
            return da1, da2, xhat

        da1, da2, xhat = da1_of(a1_ref[...], dabo_ref[0])
        misc_ref[pl.ds(0, 1), :] += rowsum(da1)
        misc_ref[pl.ds(1, 1), :] += rowsum(da2 * xhat)
        misc_ref[pl.ds(2, 1), :] += rowsum(da2)
        extda[pl.ds(0, ts), :] = da1
        extda[pl.ds(ts, HALO_A), :] = da1_of(na1[...], nda[...])[0] * not_last
        sga = jax.nn.sigmoid(ag[...])
        exta[pl.ds(0, HALO_A), :] = hav[...] * jax.nn.sigmoid(hag[...]) * not_first
        exta[pl.ds(HALO_A, ts), :] = av[...] * sga
        _shift_copies(exta, sha, ts + HALO_A - SUBLANES)
        _shift_copies(extda, shda, ts + HALO_A - SUBLANES)
        da0 = jnp.zeros((ts, c), F32)
        for k in range(CONV_A_K):
            tap = da1 * _rows_at(exta, sha, HALO_A - (CONV_A_K - 1) + k, ts)
            dwa_ref[pl.ds(SUBLANES * k, SUBLANES), :] += tap.reshape(ts // SUBLANES, SUBLANES, c).sum(axis=0)
            da0 = da0 + caw[pl.ds(CONV_A_K - 1 - k, 1), :] * _rows_at(extda, shda, k, ts)
        dp_ref[:, 0:c] = (da0 * sga).astype(BF16)
        dp_ref[:, c:2 * c] = (da0 * av[...] * sga * (1.0 - sga)).astype(BF16)

        extb[pl.ds(0, HALO_B), :] = hsc[...] * hsx[...] * not_first
        extb[pl.ds(HALO_B, ts), :] = sc[...] * sx[...]
        dbu = dabo_ref[1]
        du = dbu * sb[...]
        extdu[pl.ds(0, ts), :] = du
        extdu[pl.ds(ts, HALO_B), :] = ndb[...] * nsb[...] * not_last
        u = jnp.zeros((ts, c), F32)
        dpr = jnp.zeros((ts, c), F32)
        for k in range(CONV_B_K):
            shifted = extb[pl.ds(HALO_B - (CONV_B_K - 1) + k, ts), :]
            u = u + cbw[pl.ds(k, 1), :] * shifted
            misc_ref[pl.ds(3 + k, 1), :] += rowsum(du * shifted)
            dpr = dpr + cbw[pl.ds(CONV_B_K - 1 - k, 1), :] * extdu[pl.ds(k, ts), :]
        dp_ref[:, 2 * c:3 * c] = (dbu * u).astype(BF16)
        dp_ref[:, 3 * c:4 * c] = (dpr * sx[...]).astype(BF16)
        dp_ref[:, 4 * c:5 * c] = (dpr * sc[...]).astype(BF16)

        nt_dims = (((1,), (1,)), ((), ()))
        tn_dims = (((0,), (0,)), ((), ()))
        for h in range(N_HEADS):
            lo, hi = h * hd, (h + 1) * hd
            qh = q[:, lo:hi].astype(BF16)
            kh = kv_ref[:, lo:hi]
            vh = kv_ref[:, c + lo:c + hi]
            p = _softmax_rows(lax.dot_general(qh, kh, nt_dims, preferred_element_type=F32) * scale)
            pb = p.astype(BF16)
            doh = dabo_ref[2, :, lo:hi].astype(BF16)
            dpm = lax.dot_general(doh, vh, nt_dims, preferred_element_type=F32)
            ds = (p * (dpm - jnp.sum(dpm * p, axis=-1, keepdims=True)) * scale).astype(BF16)
            dp_ref[:, 5 * c + lo:5 * c + hi] = jnp.dot(ds, kh, preferred_element_type=F32).astype(BF16)
            dkv_ref[:, lo:hi] += lax.dot_general(ds, qh, tn_dims, preferred_element_type=F32)
            dkv_ref[:, c + lo:c + hi] += lax.dot_general(pb, doh, tn_dims, preferred_element_type=F32)

    full = lambda shp: pl.BlockSpec(shp, lambda i: (0,) * len(shp))
    n_in_before_dproj = 21
    shifted = pltpu.VMEM((SUBLANES - 1, ts + HALO_A - SUBLANES, c), F32)
    dp, dkv, dwa, misc = pl.pallas_call(
        body, name="branch_bwd", grid=(nt,),
        in_specs=[col(0), col(1), col(2), col(3), col(4), col(5), prev(0, HALO_A), prev(1, HALO_A),
                  prev(3, HALO_B), prev(4, HALO_B), nxt(HALO_B, ci=2),
                  pl.BlockSpec((ts, c), lambda i: (i, 0)), nxt(HALO_A),
                  pl.BlockSpec((N_BRANCH, ts, c), lambda i: (0, i, 0)), nxt(HALO_A, lead=0), nxt(HALO_B, lead=1),
                  full((mlen, c2)), full(conv_a_w.shape), full((1, c)), full((1, c)), full(conv_b_w.shape),
                  pl.BlockSpec(memory_space=pl.ANY)],
        out_specs=[pl.BlockSpec((ts, 6 * c), lambda i: (i, 0)), full((mlen, c2)), full((HALO_A * SUBLANES, c)),
                   full((8, c))],
        out_shape=[jax.ShapeDtypeStruct(dproj.shape, BF16), jax.ShapeDtypeStruct((mlen, c2), F32),
                   jax.ShapeDtypeStruct((HALO_A * SUBLANES, c), F32), jax.ShapeDtypeStruct((8, c), F32)],
        scratch_shapes=[pltpu.VMEM((HALO_A + ts, c), F32), pltpu.VMEM((ts + HALO_A, c), F32),
                        pltpu.VMEM((HALO_B + ts, c), F32), pltpu.VMEM((ts + HALO_B, c), F32), shifted, shifted],
        input_output_aliases={n_in_before_dproj: 0},
        compiler_params=_params(),
    )(proj, proj, proj, proj, proj, proj, proj, proj, proj, proj, proj, a1, a1, dabo, dabo, dabo, kv, conv_a_w,
      ln_g, ln_b, conv_b_w, dproj)
    return dp, dkv, dwa.reshape(HALO_A, SUBLANES, c).sum(axis=1), misc


def _merge_fwd(abo, w_cat, proj, d, after):
    nb, s, c = abo.shape
    bm, bn = _tile(s, TILE_M), _tile(d, TILE_N // 2)
    gate_col0 = (proj.shape[1] - nb * d) // bn

    def body(a_ref, w_ref, g_ref, after_ref, y_ref, m_ref, acc_ref):
        k = pl.program_id(2)
        y = jnp.dot(a_ref[...], w_ref[...], preferred_element_type=F32)
        y_ref[...] = y.astype(BF16)
        contrib = jax.nn.sigmoid(g_ref[...]) * y

        @pl.when(k == 0)
        def _():
            acc_ref[...] = contrib

        @pl.when(k > 0)
        def _():
            acc_ref[...] += contrib

        @pl.when(k == nb - 1)
        def _():
            m_ref[...] = acc_ref[...].astype(BF16)

    return pl.pallas_call(
        body, name="merge_fwd", grid=(s // bm, d // bn, nb),
        in_specs=[pl.BlockSpec((None, bm, c), lambda i, j, k: (k, i, 0)),
                  pl.BlockSpec((None, c, bn), lambda i, j, k: (k, 0, j)),
                  pl.BlockSpec((bm, bn), lambda i, j, k: (i, gate_col0 + k * (d // bn) + j)), _UNTOUCHED],
        out_specs=[pl.BlockSpec((None, bm, bn), lambda i, j, k: (k, i, j)),
                   pl.BlockSpec((bm, bn), lambda i, j, k: (i, j))],
        out_shape=[jax.ShapeDtypeStruct((nb, s, d), BF16), jax.ShapeDtypeStruct((s, d), BF16)],
        scratch_shapes=[pltpu.VMEM((bm, bn), F32)],
        compiler_params=_params(),
    )(abo, w_cat, proj, after)


def _merge_bwd(dz, w_o, y, proj):
    s, d = dz.shape
    nb = y.shape[0]
    nin = proj.shape[1]
    bm, bn = _tile(s, TILE_M), _tile(d, TILE_N // 2)
    gate_col0 = (nin - nb * d) // bn

    def body(dz_ref, w_ref, y_ref, g_ref, dy_ref, dg_ref, acc_ref):
        @pl.when(pl.program_id(2) == 0)
        def _():
            acc_ref[...] = lax.dot_general(dz_ref[...], w_ref[...], (((1,), (1,)), ((), ())),
                                           preferred_element_type=F32)

        dm = acc_ref[...]
        gt = jax.nn.sigmoid(g_ref[...])
        dy_ref[...] = (dm * gt).astype(BF16)
        dg_ref[...] = (dm * y_ref[...].astype(F32) * gt * (1.0 - gt)).astype(BF16)

    gate = lambda i, j, k: (i, gate_col0 + k * (d // bn) + j)
    return pl.pallas_call(
        body, name="merge_bwd", grid=(s // bm, d // bn, nb),
        in_specs=[pl.BlockSpec((bm, d), lambda i, j, k: (i, 0)), pl.BlockSpec((bn, d), lambda i, j, k: (j, 0)),
                  pl.BlockSpec((None, bm, bn), lambda i, j, k: (k, i, j)), pl.BlockSpec((bm, bn), gate)],
        out_specs=[pl.BlockSpec((None, bm, bn), lambda i, j, k: (k, i, j)), pl.BlockSpec((bm, bn), gate)],
        out_shape=[jax.ShapeDtypeStruct((nb, s, d), BF16), jax.ShapeDtypeStruct((s, nin), BF16)],
        scratch_shapes=[pltpu.VMEM((bm, bn), F32)],
        compiler_params=_params(),
    )(dz, w_o, y, proj)


def _window(ref, axis, who, length):
    idx = [slice(None)] * len(ref.shape)
    idx[axis] = pl.ds(pl.multiple_of(who * length, length), length)
    return ref.at[tuple(idx)]


def _all_gather(name, shards, axes):
    n = len(shards)
    out_shapes = []
    for sh, ax in zip(shards, axes):
        shp = list(sh.shape)
        shp[ax] *= N_DEV
        out_shapes.append(jax.ShapeDtypeStruct(tuple(shp), sh.dtype))

    def body(*refs):
        ins, outs = refs[:n], refs[n:2 * n]
        send, recv, local = refs[2 * n:]
        me = _my_index()
        lens = [ins[a].shape[axes[a]] for a in range(n)]
        mine = [pltpu.make_async_copy(ins[a], _window(outs[a], axes[a], me, lens[a]), local.at[a]) for a in range(n)]
        for cp in mine:
            cp.start()
        pushes = []
        for d in range(1, N_DEV):
            to = (me + d) % N_DEV
            for a in range(n):
                cp = pltpu.make_async_remote_copy(
                    src_ref=ins[a], dst_ref=_window(outs[a], axes[a], me, lens[a]), send_sem=send.at[a, d],
                    recv_sem=recv.at[a, d], device_id=_mesh_id(to), device_id_type=pl.DeviceIdType.MESH)
                cp.start()
                pushes.append(cp)
        for d in range(1, N_DEV):
            frm = (me + N_DEV - d) % N_DEV
            for a in range(n):
                pltpu.make_async_remote_copy(
                    src_ref=ins[a], dst_ref=_window(outs[a], axes[a], frm, lens[a]), send_sem=send.at[a, d],
                    recv_sem=recv.at[a, d], device_id=_mesh_id(frm), device_id_type=pl.DeviceIdType.MESH).wait_recv()
        for cp in pushes:
            cp.wait_send()
        for cp in mine:
            cp.wait()

    hbm = pl.BlockSpec(memory_space=pl.ANY)
    return pl.pallas_call(
        body, name=name, in_specs=[hbm] * n, out_specs=[hbm] * n, out_shape=out_shapes,
        scratch_shapes=[pltpu.SemaphoreType.DMA((n, N_DEV)), pltpu.SemaphoreType.DMA((n, N_DEV)),
                        pltpu.SemaphoreType.DMA((n,))],
    )(*shards)


def _pair(a, d):
    return a * N_DEV + d


def _push_ends(kind, src, land, axis, length, me, to):
    if kind == "gather":
        return src, _window(land, axis, me, length)
    return _window(src, axis, to, length), land.at[me]


def _arrival_ends(kind, src, land, axis, length, me, frm):
    if kind == "gather":
        return src, _window(land, axis, frm, length)
    return _window(src, axis, me, length), land.at[frm]


def _exchange_start(name, kind, srcs, axes, after):
    n = len(srcs)
    if kind == "gather":
        lens = [s.shape[ax] for s, ax in zip(srcs, axes)]
        land_shapes = [s.shape[:ax] + (s.shape[ax] * N_DEV,) + s.shape[ax + 1:] for s, ax in zip(srcs, axes)]
    else:
        lens = [s.shape[ax] // N_DEV for s, ax in zip(srcs, axes)]
        land_shapes = [(N_DEV,) + s.shape[:ax] + (ln,) + s.shape[ax + 1:] for s, ax, ln in zip(srcs, axes, lens)]
    lands = [lax.empty(shp, s.dtype) for shp, s in zip(land_shapes, srcs)]

    def body(*refs):
        ins = refs[:n]
        send, recv = refs[2 * n + 1], refs[2 * n + 2]
        lnd = refs[2 * n + 3 + n:2 * n + 3 + 2 * n]
        token = refs[-1]
        me = _my_index()
        for a in range(n):
            for d in range(1, N_DEV):
                to = (me + d) % N_DEV
                src, dst = _push_ends(kind, ins[a], lnd[a], axes[a], lens[a], me, to)
                pltpu.make_async_remote_copy(src_ref=src, dst_ref=dst, send_sem=send.at[_pair(a, d)],
                                             recv_sem=recv.at[_pair(a, d)], device_id=_mesh_id(to),
                                             device_id_type=pl.DeviceIdType.MESH).start()
        for a in range(n):
            pltpu.make_async_copy(*_push_ends(kind, ins[a], lnd[a], axes[a], lens[a], me, me),
                                  send.at[_pair(a, 0)]).start()
        token[...] = jnp.zeros_like(token)

    hbm = pl.BlockSpec(memory_space=pltpu.HBM)
    sem = pl.BlockSpec(memory_space=pltpu.SEMAPHORE)
    held = [pltpu.with_memory_space_constraint(t, pltpu.HBM) for t in list(srcs) + lands]
    outs = pl.pallas_call(
        body, name=name,
        in_specs=[hbm] * (2 * n) + [_UNTOUCHED],
        out_specs=[sem, sem] + [hbm] * (2 * n) + [pl.BlockSpec(memory_space=pltpu.VMEM)],
        out_shape=[pltpu.SemaphoreType.DMA((n * N_DEV,)), pltpu.SemaphoreType.DMA((n * N_DEV,))]
        + [pltpu.HBM(t.shape, t.dtype) for t in held] + [jax.ShapeDtypeStruct((8, 128), F32)],
        input_output_aliases={i: 2 + i for i in range(2 * n)},
        compiler_params=pltpu.CompilerParams(has_side_effects=pltpu.SideEffectType.DATAFLOW_SIDE_EFFECTING),
    )(*held, after)
    handle = dict(kind=kind, axes=axes, lens=lens, send=outs[0], recv=outs[1], srcs=outs[2:2 + n],
                  lands=outs[2 + n:2 + 2 * n])
    return handle, outs[-1]


def _exchange_wait(name, handle, after):
    kind, axes, lens = handle["kind"], handle["axes"], handle["lens"]
    srcs, lands = handle["srcs"], handle["lands"]
    n = len(srcs)
    n_in = 2 * n + 2 + len(after)

    def body(*refs):
        ins = refs[:n]
        send, recv = refs[2 * n], refs[2 * n + 1]
        got = refs[n_in + n:n_in + 2 * n]
        me = _my_index()
        for d in range(1, N_DEV):
            frm = (me + N_DEV - d) % N_DEV
            for a in range(n):
                src, dst = _arrival_ends(kind, ins[a], got[a], axes[a], lens[a], me, frm)
                pltpu.make_async_remote_copy(src_ref=src, dst_ref=dst, send_sem=send.at[_pair(a, d)],
                                             recv_sem=recv.at[_pair(a, d)], device_id=_mesh_id(frm),
                                             device_id_type=pl.DeviceIdType.MESH).wait_recv()
        for d in range(1, N_DEV):
            to = (me + d) % N_DEV
            for a in range(n):
                src, dst = _push_ends(kind, ins[a], got[a], axes[a], lens[a], me, to)
                pltpu.make_async_remote_copy(src_ref=src, dst_ref=dst, send_sem=send.at[_pair(a, d)],
                                             recv_sem=recv.at[_pair(a, d)], device_id=_mesh_id(to),
                                             device_id_type=pl.DeviceIdType.MESH).wait_send()
        for a in range(n):
            pltpu.make_async_copy(*_push_ends(kind, ins[a], got[a], axes[a], lens[a], me, me),
                                  send.at[_pair(a, 0)]).wait()

    hbm = pl.BlockSpec(memory_space=pltpu.HBM)
    sem = pl.BlockSpec(memory_space=pltpu.SEMAPHORE)
    outs = pl.pallas_call(
        body, name=name,
        in_specs=[hbm] * (2 * n) + [sem, sem] + [_UNTOUCHED] * len(after),
        out_specs=[hbm] * (2 * n),
        out_shape=[pltpu.HBM(t.shape, t.dtype) for t in list(srcs) + list(lands)],
        input_output_aliases={i: i for i in range(2 * n)},
        compiler_params=pltpu.CompilerParams(has_side_effects=pltpu.SideEffectType.DATAFLOW_SIDE_EFFECTING),
    )(*srcs, *lands, handle["send"], handle["recv"], *after)
    return outs[n:]


def _all_reduce_small(packed):
    r, c = packed.shape

    def body(p_ref, o_ref, buf, send, recv):
        me = _my_index()
        buf[me] = p_ref[...]
        pushes = []
        for d in range(1, N_DEV):
            to = (me + d) % N_DEV
            cp = pltpu.make_async_remote_copy(src_ref=p_ref, dst_ref=buf.at[me], send_sem=send.at[d],
                                              recv_sem=recv.at[d], device_id=_mesh_id(to),
                                              device_id_type=pl.DeviceIdType.MESH)
            cp.start()
            pushes.append(cp)
        for d in range(1, N_DEV):
            frm = (me + N_DEV - d) % N_DEV
            pltpu.make_async_remote_copy(src_ref=p_ref, dst_ref=buf.at[frm], send_sem=send.at[d], recv_sem=recv.at[d],
                                         device_id=_mesh_id(frm), device_id_type=pl.DeviceIdType.MESH).wait_recv()
        for cp in pushes:
            cp.wait_send()
        acc = buf[0]
        for s in range(1, N_DEV):
            acc = acc + buf[s]
        o_ref[...] = acc

    vmem = pl.BlockSpec(memory_space=pltpu.VMEM)
    return pl.pallas_call(
        body, name="all_reduce_small", in_specs=[vmem], out_specs=vmem, out_shape=jax.ShapeDtypeStruct((r, c), F32),
        scratch_shapes=[pltpu.VMEM((N_DEV, r, c), F32), pltpu.SemaphoreType.DMA((N_DEV,)),
                        pltpu.SemaphoreType.DMA((N_DEV,))],
        compiler_params=_params(),
    )(packed)


def _layer_fwd_branches(x, mem, sm, wf, after):
    h = _rmsnorm("rms_mix_pre", x, sm["g_mix_pre"], after)
    proj = _mm_nn("proj", h, wf["w_in"], F32)
    mem_n = _rmsnorm("rms_mem", mem, sm["g_mem"], mem)
    kv = _mm_nn("kv", mem_n, wf["w_kv"], BF16)
    a1, abo = _branch_fwd(proj, kv, wf["conv_a_w"], sm["conv_a_b"], sm["ln_a_g"], sm["ln_a_b"], wf["conv_b_w"])
    return dict(x=x, h=h, proj=proj, mem_n=mem_n, kv=kv, a1=a1, abo=abo)


def _layer_fwd_rest(sm, wf, sv, after):
    x = sv["x"]
    y, merged = _merge_fwd(sv["abo"], wf["w_cat"], sv["proj"], x.shape[1], after)
    z = _mm_nn("mix_out", merged, wf["w_o"], F32)
    x1 = _res_norm("mix_out_norm", z, x, sm["g_mix_post"])
    h2 = _rmsnorm("rms_mlp_pre", x1, sm["g_mlp_pre"], x1)
    r, act = _mm_relu2("mlp_up", h2, wf["w_up"])
    f = _mm_nn("mlp_down", act, wf["w_down"], F32)
    x2 = _res_norm("mlp_down_norm", f, x1, sm["g_mlp_post"])
    sv.update(y=y, merged=merged, z=z, x1=x1, h2=h2, r=r, act=act, f=f)
    return x2


def _pad_rows(t):
    return jnp.pad(t, ((0, (-t.shape[0]) % 8), (0, 0)))


def _layer_bwd(dx2, mem, sm, wf, sv, send_off):
    c = sv["a1"].shape[1]
    df, dg_mlp_post = _rms_bwd("rms_bwd_post", dx2, sv["f"], sm["g_mlp_post"], BF16, dx2)
    d_up = _mm_relu2_bwd("mlp_down_bwd", df, wf["w_down"], sv["r"])
    gw_down = _mm_tn("gw_down", sv["act"], df, BF16)
    dh2 = _mm_nt("mlp_up_bwd", d_up, wf["w_up"], F32)
    gw_up = _mm_tn("gw_up", sv["h2"], d_up, BF16)
    sent = send_off(dict(w_up=gw_up, w_down=gw_down))
    dx1, dg_mlp_pre = _rms_bwd("rms_bwd_pre", dh2, sv["x1"], sm["g_mlp_pre"], F32, sent, residual=dx2)
    dz, dg_mix_post = _rms_bwd("rms_bwd_post", dx1, sv["z"], sm["g_mix_post"], BF16, dx1)
    dy, dproj = _merge_bwd(dz, wf["w_o"], sv["y"], sv["proj"])
    gw_o = _mm_tn("gw_o", sv["merged"], dz, BF16)
    dabo = _mm_nt("branch_out_bwd", dy, wf["w_cat"], F32)
    gw_cat = _mm_tn("gw_branch_out", sv["abo"], dy, BF16)
    dproj, dkv, dconv_a_w, misc = _branch_bwd(sv["proj"], sv["a1"], dabo, sv["kv"], wf["conv_a_w"], sm["ln_a_g"],
                                              sm["ln_a_b"], wf["conv_b_w"], dproj)
    dkv = dkv.astype(BF16)
    gw_kv = _mm_tn("gw_kv", sv["mem_n"], dkv, BF16)
    dmem_n = _mm_nt("kv_bwd", dkv, wf["w_kv"], F32)
    _, dg_mem = _rms_bwd("rms_bwd_mem", dmem_n, mem, sm["g_mem"], BF16, dmem_n)
    gw_in = _mm_tn("gw_in", sv["h"], dproj, BF16)
    sent = send_off(dict(w_in=gw_in, w_kv=gw_kv, w_cat=gw_cat, w_o=gw_o))
    dh = _mm_nt("proj_bwd", dproj, wf["w_in"], F32, after=sent)
    dx, dg_mix_pre = _rms_bwd("rms_bwd_pre", dh, sv["x"], sm["g_mix_pre"], F32, dh, residual=dx1)
    rows = [_pad_rows(t.reshape(2, c)) for t in (dg_mix_pre, dg_mem, dg_mix_post, dg_mlp_pre, dg_mlp_post)]
    return dx, jnp.concatenate(rows + [misc, dconv_a_w], axis=0)


_SMALL_D_NAMES = ["g_mix_pre", "g_mem", "g_mix_post", "g_mlp_pre", "g_mlp_post"]
_SMALL_C_NAMES = ["conv_a_b", "ln_a_g", "ln_a_b"]
_SMALL_MISC_ROW = 8 * len(_SMALL_D_NAMES)
_SMALL_CONV_B_ROW = _SMALL_MISC_ROW + len(_SMALL_C_NAMES)
_SMALL_CONV_A_ROW = _SMALL_MISC_ROW + 8
_SMALL_ROWS = _SMALL_CONV_A_ROW + HALO_A


def kernel(x, mem, g_mix_pre, w_in, conv_a_w, conv_a_b, ln_a_g, ln_a_b, w_a_out, conv_b_w, w_b_out, g_mem, w_kv, w_x_out, w_o, g_mix_post, g_mlp_pre, w_up, w_down, g_mlp_post, loss_target, m_g_mix_pre, m_w_in, m_conv_a_w, m_conv_a_b, m_ln_a_g, m_ln_a_b, m_w_a_out, m_conv_b_w, m_w_b_out, m_g_mem, m_w_kv, m_w_x_out, m_w_o, m_g_mix_post, m_g_mlp_pre, m_w_up, m_w_down, m_g_mlp_post, v_g_mix_pre, v_w_in, v_conv_a_w, v_conv_a_b, v_ln_a_g, v_ln_a_b, v_w_a_out, v_conv_b_w, v_w_b_out, v_g_mem, v_w_kv, v_w_x_out, v_w_o, v_g_mix_post, v_g_mlp_pre, v_w_up, v_w_down, v_g_mlp_post):
    names = ["g_mix_pre", "w_in", "conv_a_w", "conv_a_b", "ln_a_g", "ln_a_b", "w_a_out", "conv_b_w", "w_b_out",
             "g_mem", "w_kv", "w_x_out", "w_o", "g_mix_post", "g_mlp_pre", "w_up", "w_down", "g_mlp_post"]
    w = dict(zip(names, [g_mix_pre, w_in, conv_a_w, conv_a_b, ln_a_g, ln_a_b, w_a_out, conv_b_w, w_b_out, g_mem,
                         w_kv, w_x_out, w_o, g_mix_post, g_mlp_pre, w_up, w_down, g_mlp_post]))
    mo = dict(zip(names, [m_g_mix_pre, m_w_in, m_conv_a_w, m_conv_a_b, m_ln_a_g, m_ln_a_b, m_w_a_out, m_conv_b_w,
                          m_w_b_out, m_g_mem, m_w_kv, m_w_x_out, m_w_o, m_g_mix_post, m_g_mlp_pre, m_w_up, m_w_down,
                          m_g_mlp_post]))
    vo = dict(zip(names, [v_g_mix_pre, v_w_in, v_conv_a_w, v_conv_a_b, v_ln_a_g, v_ln_a_b, v_w_a_out, v_conv_b_w,
                          v_w_b_out, v_g_mem, v_w_kv, v_w_x_out, v_w_o, v_g_mix_post, v_g_mlp_pre, v_w_up, v_w_down,
                          v_g_mlp_post]))
    depth = w_in.shape[0]
    c = conv_a_b.shape[1]
    cs = conv_a_w.shape[2]
    me = _my_index()
    xs, mems, tgt = x[0], mem[0], loss_target[0]

    big_names = ["w_in", "w_kv", "w_cat", "w_o", "w_up", "w_down"]
    big_axes = [1, 0, 2, 0, 1, 0]
    n_early = 2
    branch_names = ["w_a_out", "w_b_out", "w_x_out"]
    smalls = [{k: w[k][l][None, :] for k in _SMALL_D_NAMES + _SMALL_C_NAMES} for l in range(depth)]

    def weight_shards(l):
        return [w_in[l].astype(BF16), w_kv[l].astype(BF16), jnp.stack([w[k][l] for k in branch_names]).astype(BF16),
                w_o[l].astype(BF16), w_up[l].astype(BF16), w_down[l].astype(BF16)]

    def start_gather(l, after):
        shards = weight_shards(l)
        early, token = _exchange_start(f"gather_start_{l}_early", "gather", shards[:n_early], big_axes[:n_early],
                                       after)
        late, token = _exchange_start(f"gather_start_{l}_late", "gather", shards[n_early:], big_axes[n_early:], token)
        return early, late, token

    taps_a, taps_b = _all_gather(
        "gather_conv_taps",
        [jnp.pad(conv_a_w, ((0, 0), (0, HALO_A - CONV_A_K), (0, 0))),
         jnp.pad(conv_b_w, ((0, 0), (0, HALO_B - CONV_B_K), (0, 0)))], [2, 2])

    early, late, token = start_gather(0, taps_a)
    fulls = [None] * depth
    saved = []
    xc = xs
    for l in range(depth):
        fulls[l] = dict(zip(big_names[:n_early], _exchange_wait(f"gather_wait_{l}_early", early, [xc])))
        fulls[l]["conv_a_w"], fulls[l]["conv_b_w"] = taps_a[l], taps_b[l]
        sv = _layer_fwd_branches(xc, mems, smalls[l], fulls[l], token)
        fulls[l].update(zip(big_names[n_early:], _exchange_wait(f"gather_wait_{l}_late", late, [sv["abo"]])))
        token = fulls[l]["w_o"]
        if l + 1 < depth:
            early, late, token = start_gather(l + 1, token)
        xc = _layer_fwd_rest(smalls[l], fulls[l], sv, token)
        saved.append(sv)
    loss_part, dx = _loss_head(xc, tgt)
    loss = lax.psum(loss_part, MESH_AXES)

    upd = {}

    def adamw_group(l, group, slots, after):
        for k, sl in zip(group, slots):
            if k == "w_cat":
                for b, nm in enumerate(branch_names):
                    upd[nm] = _adamw("adamw_" + nm, sl, b, w[nm], mo[nm], vo[nm], l, upd.get(nm), after)
            else:
                upd[k] = _adamw("adamw_" + k, sl, None, w[k], mo[k], vo[k], l, upd.get(k), after)

    small_parts = [None] * depth
    in_flight = [None]

    def take_in(after):
        handle, l, group = in_flight[0]
        return _exchange_wait(f"scatter_wait_{l}_{group[0]}", handle, after), l, group

    def send_off_layer(l):
        def send_off(grads):
            group = [k for k in big_names if k in grads]
            arrays = [grads[k] for k in group]
            arrived = None if in_flight[0] is None else take_in([arrays[0]])
            handle, token = _exchange_start(f"scatter_start_{l}_{group[0]}", "scatter", arrays,
                                            [big_axes[big_names.index(k)] for k in group],
                                            arrays[0] if arrived is None else arrived[0][0])
            in_flight[0] = (handle, l, group)
            if arrived is not None:
                adamw_group(arrived[1], arrived[2], arrived[0], token)
            return token
        return send_off

    for l in reversed(range(depth)):
        dx, small_parts[l] = _layer_bwd(dx, mems, smalls[l], fulls[l], saved[l], send_off_layer(l))

    tot = _all_reduce_small(jnp.concatenate(small_parts, axis=0)).reshape(depth, _SMALL_ROWS, c)
    g_small = {}
    for n_, nm in enumerate(_SMALL_D_NAMES):
        g_small[nm] = tot[:, 8 * n_:8 * n_ + 2, :].reshape(depth, 2 * c)
    for n_, nm in enumerate(_SMALL_C_NAMES):
        g_small[nm] = tot[:, _SMALL_MISC_ROW + n_, :]
    g_taps = {"conv_a_w": tot[:, _SMALL_CONV_A_ROW:_SMALL_CONV_A_ROW + CONV_A_K, :],
              "conv_b_w": tot[:, _SMALL_CONV_B_ROW:_SMALL_CONV_B_ROW + CONV_B_K, :]}

    def pack(t):
        return jnp.concatenate([_pad_rows(t[nm].reshape(-1, c)) for nm in _SMALL_D_NAMES + _SMALL_C_NAMES], axis=0)

    res = _adamw("adamw_small", pack(g_small)[None], None, pack(w)[None], pack(mo)[None], pack(vo)[None], 0, None, tot)
    row = 0
    for nm in _SMALL_D_NAMES + _SMALL_C_NAMES:
        n_rows = w[nm].size // c
        upd[nm] = [t[0, row:row + n_rows].reshape(w[nm].shape) for t in res]
        row += n_rows + (-n_rows) % 8
    for nm, g in g_taps.items():
        shp = w[nm].shape
        mine = lax.dynamic_slice_in_dim(g, me * cs, cs, axis=2).reshape(1, -1, cs)
        res = _adamw("adamw_" + nm, mine, None, w[nm].reshape(1, -1, cs), mo[nm].reshape(1, -1, cs),
                     vo[nm].reshape(1, -1, cs), 0, None, tot)
        upd[nm] = [t.reshape(shp) for t in res]
    done = [u[0] for u in upd.values()]
    slots, l, group = take_in(done + [dx])
    adamw_group(l, group, slots, tot)

    return (loss, dx[None], *[upd[nm][0] for nm in names], *[upd[nm][1] for nm in names],
            *[upd[nm][2] for nm in names], *[upd[nm][3] for nm in names])
```

```python
import math

import jax
import jax.numpy as jnp
from jax import lax
from jax.experimental import pallas as pl
from jax.experimental.pallas import tpu as pltpu

F32 = jnp.float32
BF16 = jnp.bfloat16

EPS = 1e-6
N_HEADS = 4
N_BRANCH = 3
CONV_A_K = 31
CONV_B_K = 3
HALO_A = 32
HALO_B = 8
N_DEV = 8
MESH_AXES = ("x", "y", "c")

ADAM_LR = 0.001
ADAM_B1 = 0.9
ADAM_B2 = 0.999
ADAM_EPS = 1e-08
ADAM_WD = 0.01
ADAM_STEP = 10

V7X_VMEM_BYTES = 64 * 1024 * 1024
VMEM_LIMIT = V7X_VMEM_BYTES - 8 * 1024 * 1024

TILE_M = 1024
TILE_N = 1024
TILE_K = 4096
TILE_ROWS_WIDE = 512
TILE_ROWS_BRANCH = 256
TILE_ROWS_BRANCH_BWD = 128


def _params():
    return pltpu.CompilerParams(vmem_limit_bytes=VMEM_LIMIT)


def _tile(n, t):
    t = min(n, t)
    assert n % t == 0, (n, t)
    return t


def _my_index():
    return 4 * lax.axis_index("x") + 2 * lax.axis_index("y") + lax.axis_index("c")


def _mesh_id(p):
    return (p // 4, (p // 2) % 2, p % 2)


def _mm(name, grid, a, a_spec, b, b_spec, *, ta, tb, acc_shape, extras, outs, epilogue):
    nk = grid[-1]
    lead = len(grid) - 3
    ne, no = len(extras), len(outs)
    dn = (((0,) if ta else (1,), (1,) if tb else (0,)), ((), ()))

    def body(a_ref, b_ref, *rest):
        extra_refs = rest[:ne]
        out_refs = rest[ne:ne + no]
        i, j, k = pl.program_id(lead), pl.program_id(lead + 1), pl.program_id(lead + 2)
        prod = lax.dot_general(a_ref[...], b_ref[...], dn, preferred_element_type=F32)
        if nk == 1:
            epilogue(prod, extra_refs, out_refs, i, j)
        else:
            acc_ref = rest[ne + no]

            @pl.when(k == 0)
            def _():
                acc_ref[...] = prod

            @pl.when(k > 0)
            def _():
                acc_ref[...] += prod

            @pl.when(k == nk - 1)
            def _():
                epilogue(acc_ref[...], extra_refs, out_refs, i, j)

    return pl.pallas_call(
        body, name=name, grid=grid,
        in_specs=[a_spec, b_spec] + [s for _, s in extras],
        out_specs=[s for _, s in outs],
        out_shape=[o for o, _ in outs],
        scratch_shapes=[pltpu.VMEM(acc_shape, F32)] if nk > 1 else [],
        compiler_params=_params(),
    )(a, b, *[e for e, _ in extras])


def _store_epilogue(res, extra_refs, out_refs, i, j):
    out_refs[0][...] = res.astype(out_refs[0].dtype)


def _mm_nn(name, a, b, out_dtype):
    m, kd = a.shape
    n = b.shape[1]
    bm, bn, bk = _tile(m, TILE_M), _tile(n, TILE_N), _tile(kd, TILE_K)
    return _mm(name, (m // bm, n // bn, kd // bk), a, pl.BlockSpec((bm, bk), lambda i, j, k: (i, k)),
               b, pl.BlockSpec((bk, bn), lambda i, j, k: (k, j)), ta=False, tb=False, acc_shape=(bm, bn), extras=[],
               outs=[(jax.ShapeDtypeStruct((m, n), out_dtype), pl.BlockSpec((bm, bn), lambda i, j, k: (i, j)))],
               epilogue=_store_epilogue)[0]


def _mm_nt(name, a, b, out_dtype, after=None):
    m, kd = a.shape[-2:]
    n = b.shape[-2]
    bm, bn = _tile(m, TILE_M), _tile(n, TILE_N)
    if a.ndim == 2:
        bk = _tile(kd, TILE_K)
        return _mm(name, (m // bm, n // bn, kd // bk), a, pl.BlockSpec((bm, bk), lambda i, j, k: (i, k)),
                   b, pl.BlockSpec((bn, bk), lambda i, j, k: (j, k)), ta=False, tb=True, acc_shape=(bm, bn),
                   extras=[] if after is None else [(after, _UNTOUCHED)],
                   outs=[(jax.ShapeDtypeStruct((m, n), out_dtype), pl.BlockSpec((bm, bn), lambda i, j, k: (i, j)))],
                   epilogue=_store_epilogue)[0]
    nb = a.shape[0]
    return _mm(name, (nb, m // bm, n // bn, 1), a, pl.BlockSpec((None, bm, kd), lambda s, i, j, k: (s, i, 0)),
               b, pl.BlockSpec((None, bn, kd), lambda s, i, j, k: (s, j, 0)), ta=False, tb=True, acc_shape=(bm, bn),
               extras=[],
               outs=[(jax.ShapeDtypeStruct((nb, m, n), out_dtype),
                      pl.BlockSpec((None, bm, bn), lambda s, i, j, k: (s, i, j)))],
               epilogue=_store_epilogue)[0]


def _mm_tn(name, a, b, out_dtype):
    r, m = a.shape[-2:]
    n = b.shape[-1]
    bm, bn, bk = _tile(m, TILE_M), _tile(n, TILE_N), _tile(r, TILE_K)
    if a.ndim == 2:
        return _mm(name, (m // bm, n // bn, r // bk), a, pl.BlockSpec((bk, bm), lambda i, j, k: (k, i)),
                   b, pl.BlockSpec((bk, bn), lambda i, j, k: (k, j)), ta=True, tb=False, acc_shape=(bm, bn), extras=[],
                   outs=[(jax.ShapeDtypeStruct((m, n), out_dtype), pl.BlockSpec((bm, bn), lambda i, j, k: (i, j)))],
                   epilogue=_store_epilogue)[0]
    nb = a.shape[0]
    return _mm(name, (nb, m // bm, n // bn, r // bk), a, pl.BlockSpec((None, bk, bm), lambda s, i, j, k: (s, k, i)),
               b, pl.BlockSpec((None, bk, bn), lambda s, i, j, k: (s, k, j)), ta=True, tb=False, acc_shape=(bm, bn),
               extras=[],
               outs=[(jax.ShapeDtypeStruct((nb, m, n), out_dtype),
                      pl.BlockSpec((None, bm, bn), lambda s, i, j, k: (s, i, j)))],
               epilogue=_store_epilogue)[0]


def _mm_relu2(name, h, w):
    m, kd = h.shape
    n = w.shape[1]
    bm, bn = _tile(m, TILE_M), _tile(n, TILE_N)

    def epilogue(res, extra_refs, out_refs, i, j):
        r = jnp.maximum(res, 0.0)
        out_refs[0][...] = r.astype(BF16)
        out_refs[1][...] = (r * r).astype(BF16)

    o = jax.ShapeDtypeStruct((m, n), BF16)
    spec = pl.BlockSpec((bm, bn), lambda i, j, k: (i, j))
    return _mm(name, (m // bm, n // bn, 1), h, pl.BlockSpec((bm, kd), lambda i, j, k: (i, 0)),
               w, pl.BlockSpec((kd, bn), lambda i, j, k: (0, j)), ta=False, tb=False, acc_shape=(bm, bn), extras=[],
               outs=[(o, spec), (o, spec)], epilogue=epilogue)


def _mm_relu2_bwd(name, df, w_down, r):
    m, kd = df.shape
    n = w_down.shape[0]
    bm, bn = _tile(m, TILE_M), _tile(n, TILE_N)

    def epilogue(res, extra_refs, out_refs, i, j):
        out_refs[0][...] = (res * (2.0 * extra_refs[0][...].astype(F32))).astype(BF16)

    spec = pl.BlockSpec((bm, bn), lambda i, j, k: (i, j))
    return _mm(name, (m // bm, n // bn, 1), df, pl.BlockSpec((bm, kd), lambda i, j, k: (i, 0)),
               w_down, pl.BlockSpec((bn, kd), lambda i, j, k: (j, 0)), ta=False, tb=True, acc_shape=(bm, bn),
               extras=[(r, spec)], outs=[(jax.ShapeDtypeStruct((m, n), BF16), spec)], epilogue=epilogue)[0]


def _rms_bwd_math(dy, x, g):
    rr = lax.rsqrt(jnp.mean(x * x, axis=-1, keepdims=True) + EPS)
    gy = dy * g
    dx = rr * gy - x * (rr * rr * rr * jnp.mean(x * gy, axis=-1, keepdims=True))
    dg_rows = dy * x * rr
    return dx, dg_rows


_UNTOUCHED = pl.BlockSpec(memory_space=pl.ANY)


def _rmsnorm(name, x, g, after):
    r, d = x.shape
    br = _tile(r, TILE_ROWS_WIDE)

    def body(x_ref, g_ref, after_ref, o_ref):
        xv = x_ref[...]
        rr = lax.rsqrt(jnp.mean(xv * xv, axis=-1, keepdims=True) + EPS)
        o_ref[...] = ((xv * rr) * g_ref[...]).astype(BF16)

    row = pl.BlockSpec((br, d), lambda i: (i, 0))
    return pl.pallas_call(body, name=name, grid=(r // br,),
                          in_specs=[row, pl.BlockSpec((1, d), lambda i: (0, 0)), _UNTOUCHED],
                          out_specs=row, out_shape=jax.ShapeDtypeStruct((r, d), BF16),
                          compiler_params=_params())(x, g, after)


def _res_norm(name, z, x, g):
    r, d = x.shape
    br = _tile(r, TILE_ROWS_WIDE)

    def body(z_ref, x_ref, g_ref, o_ref):
        zv = z_ref[...]
        rr = lax.rsqrt(jnp.mean(zv * zv, axis=-1, keepdims=True) + EPS)
        o_ref[...] = x_ref[...] + (zv * rr) * g_ref[...]

    row = pl.BlockSpec((br, d), lambda i: (i, 0))
    return pl.pallas_call(body, name=name, grid=(r // br,),
                          in_specs=[row, row, pl.BlockSpec((1, d), lambda i: (0, 0))],
                          out_specs=row, out_shape=jax.ShapeDtypeStruct((r, d), F32),
                          compiler_params=_params())(z, x, g)


def _rms_bwd(name, dy, x, g, out_dtype, after, residual=None):
    r, d = x.shape
    br = _tile(r, TILE_ROWS_WIDE)
    n_res = 0 if residual is None else 1

    def body(dy_ref, x_ref, g_ref, after_ref, *rest):
        dx_ref, dg_ref = rest[n_res:]
        dx, dg_rows = _rms_bwd_math(dy_ref[...], x_ref[...], g_ref[...])
        if n_res:
            dx = dx + rest[0][...]
        dx_ref[...] = dx.astype(out_dtype)

        @pl.when(pl.program_id(0) == 0)
        def _():
            dg_ref[...] = jnp.zeros_like(dg_ref)

        dg_ref[...] += jnp.sum(dg_rows, axis=0, keepdims=True)

    row = pl.BlockSpec((br, d), lambda i: (i, 0))
    vec = pl.BlockSpec((1, d), lambda i: (0, 0))
    return pl.pallas_call(body, name=name, grid=(r // br,), in_specs=[row, row, vec, _UNTOUCHED] + [row] * n_res,
                          out_specs=[row, vec],
                          out_shape=[jax.ShapeDtypeStruct((r, d), out_dtype), jax.ShapeDtypeStruct((1, d), F32)],
                          compiler_params=_params())(dy, x, g, after, *([] if residual is None else [residual]))


def _loss_head(y, target):
    r, d = y.shape
    br = _tile(r, TILE_ROWS_WIDE)

    def body(y_ref, t_ref, dy_ref, l_ref):
        diff = y_ref[...] - t_ref[...]
        dy_ref[...] = diff * (1.0 / d)

        @pl.when(pl.program_id(0) == 0)
        def _():
            l_ref[...] = jnp.zeros_like(l_ref)

        l_ref[...] += 0.5 * jnp.sum(jnp.mean(diff * diff, axis=-1, keepdims=True))

    row = pl.BlockSpec((br, d), lambda i: (i, 0))
    one = pl.BlockSpec((8, 128), lambda i: (0, 0))
    dy, l = pl.pallas_call(body, name="loss_head", grid=(r // br,), in_specs=[row, row], out_specs=[row, one],
                           out_shape=[jax.ShapeDtypeStruct((r, d), F32), jax.ShapeDtypeStruct((8, 128), F32)],
                           compiler_params=_params())(y, target)
    return l[0, 0], dy


def _adamw(name, parts, branch, w, m, v, layer, prev, after):
    _, r, c = w.shape
    p = parts.shape[0]
    br = r
    for cand in (512, 256, 128, 64, 32, 16):
        if r % cand == 0 and cand * c * 4 <= 2 * 1024 * 1024:
            br = cand
            break
    bc1 = 1.0 - ADAM_B1 ** ADAM_STEP
    bc2 = 1.0 - ADAM_B2 ** ADAM_STEP
    if prev is None:
        prev = [lax.empty(w.shape, F32) for _ in range(4)]

    def body(p_ref, w_ref, m_ref, v_ref, after_ref, pg, pd, pm, pv, g_out, d_out, m_out, v_out):
        g = p_ref[0].astype(F32)
        for s in range(1, p):
            g = g + p_ref[s].astype(F32)
        m2 = ADAM_B1 * m_ref[...] + (1.0 - ADAM_B1) * g
        v2 = ADAM_B2 * v_ref[...] + (1.0 - ADAM_B2) * (g * g)
        m_hat = m2 / bc1
        v_hat = v2 / bc2
        g_out[...] = g
        d_out[...] = -ADAM_LR * (m_hat / (jnp.sqrt(v_hat) + ADAM_EPS) + ADAM_WD * w_ref[...])
        m_out[...] = m2
        v_out[...] = v2

    if branch is None:
        p_spec = pl.BlockSpec((p, br, c), lambda i: (0, i, 0))
    else:
        p_spec = pl.BlockSpec((p, None, br, c), lambda i: (0, branch, i, 0))
    slab = pl.BlockSpec((None, br, c), lambda i: (layer, i, 0))
    o = jax.ShapeDtypeStruct(w.shape, F32)
    return pl.pallas_call(body, name=name, grid=(r // br,),
                          in_specs=[p_spec, slab, slab, slab, _UNTOUCHED] + [_UNTOUCHED] * 4,
                          out_specs=[slab] * 4, out_shape=[o] * 4, input_output_aliases={5: 0, 6: 1, 7: 2, 8: 3},
                          compiler_params=_params())(parts, w, m, v, after, *prev)


def _layer_norm_parts(a1, ln_g, ln_b):
    mu = jnp.mean(a1, axis=-1, keepdims=True)
    xc = a1 - mu
    rstd = lax.rsqrt(jnp.mean(xc * xc, axis=-1, keepdims=True) + EPS)
    xhat = xc * rstd
    a2 = xhat * ln_g + ln_b
    return xhat, rstd, a2


def _softmax_rows(s):
    e = jnp.exp(s - jnp.max(s, axis=-1, keepdims=True))
    return e / jnp.sum(e, axis=-1, keepdims=True)


SUBLANES = 8
CONV_ROWS = 32


def _shift_copies(ext_ref, sh_ref, rows):
    for r in range(1, SUBLANES):
        sh_ref[r - 1, pl.ds(0, rows), :] = ext_ref[pl.ds(r, rows), :]


def _rows_at(ext_ref, sh_ref, offset, ts):
    q, r = divmod(offset, SUBLANES)
    if r == 0:
        return ext_ref[pl.ds(SUBLANES * q, ts), :]
    return sh_ref[r - 1, pl.ds(SUBLANES * q, ts), :]


def _branch_specs(ts, c):
    def col(ci):
        return pl.BlockSpec((ts, c), lambda i: (i, ci))

    def prev(ci, h):
        return pl.BlockSpec((h, c), lambda i: (jnp.maximum(i * (ts // h) - 1, 0), ci))

    return col, prev


def _branch_fwd(proj, kv, conv_a_w, conv_a_b, ln_g, ln_b, conv_b_w):
    s = proj.shape[0]
    mlen, c2 = kv.shape
    c = c2 // 2
    hd = c // N_HEADS
    ts = _tile(s, TILE_ROWS_BRANCH)
    scale = hd ** -0.5
    col, prev = _branch_specs(ts, c)

    def body(av, ag, sb, sc, sx, q, hav, hag, hsc, hsx, kv_ref, caw, cab, lng, lnb, cbw, a1_ref, abo_ref, exta, extb,
             sha):
        not_first = (pl.program_id(0) > 0).astype(F32)
        exta[pl.ds(0, HALO_A), :] = hav[...] * jax.nn.sigmoid(hag[...]) * not_first
        exta[pl.ds(HALO_A, ts), :] = av[...] * jax.nn.sigmoid(ag[...])
        _shift_copies(exta, sha, ts + HALO_A - SUBLANES)
        acc = jnp.broadcast_to(cab[...], (ts, c))
        for k in range(CONV_A_K):
            acc = acc + caw[pl.ds(k, 1), :] * _rows_at(exta, sha, HALO_A - (CONV_A_K - 1) + k, ts)
        a1_ref[...] = acc
        _, _, a2 = _layer_norm_parts(acc, lng[...], lnb[...])
        abo_ref[0] = (a2 * jax.nn.sigmoid(a2)).astype(BF16)
        extb[pl.ds(0, HALO_B), :] = hsc[...] * hsx[...] * not_first
        extb[pl.ds(HALO_B, ts), :] = sc[...] * sx[...]
        u = cbw[pl.ds(0, 1), :] * extb[pl.ds(HALO_B - (CONV_B_K - 1), ts), :]
        for k in range(1, CONV_B_K):
            u = u + cbw[pl.ds(k, 1), :] * extb[pl.ds(HALO_B - (CONV_B_K - 1) + k, ts), :]
        abo_ref[1] = (sb[...] * u).astype(BF16)
        for h in range(N_HEADS):
            qh = q[:, h * hd:(h + 1) * hd].astype(BF16)
            kh = kv_ref[:, h * hd:(h + 1) * hd]
            vh = kv_ref[:, c + h * hd:c + (h + 1) * hd]
            sc_ = lax.dot_general(qh, kh, (((1,), (1,)), ((), ())), preferred_element_type=F32) * scale
            p = _softmax_rows(sc_).astype(BF16)
            abo_ref[2, :, h * hd:(h + 1) * hd] = jnp.dot(p, vh, preferred_element_type=F32).astype(BF16)

    full = lambda shp: pl.BlockSpec(shp, lambda i: (0,) * len(shp))
    return pl.pallas_call(
        body, name="branch_fwd", grid=(s // ts,),
        in_specs=[col(0), col(1), col(2), col(3), col(4), col(5), prev(0, HALO_A), prev(1, HALO_A),
                  prev(3, HALO_B), prev(4, HALO_B), full((mlen, c2)), full(conv_a_w.shape), full((1, c)),
                  full((1, c)), full((1, c)), full(conv_b_w.shape)],
        out_specs=[pl.BlockSpec((ts, c), lambda i: (i, 0)), pl.BlockSpec((N_BRANCH, ts, c), lambda i: (0, i, 0))],
        out_shape=[jax.ShapeDtypeStruct((s, c), F32), jax.ShapeDtypeStruct((N_BRANCH, s, c), BF16)],
        scratch_shapes=[pltpu.VMEM((HALO_A + ts, c), F32), pltpu.VMEM((HALO_B + ts, c), F32),
                        pltpu.VMEM((SUBLANES - 1, ts + HALO_A - SUBLANES, c), F32)],
        compiler_params=_params(),
    )(proj, proj, proj, proj, proj, proj, proj, proj, proj, proj, kv, conv_a_w, conv_a_b, ln_g, ln_b, conv_b_w)


def _branch_bwd(proj, a1, dabo, kv, conv_a_w, ln_g, ln_b, conv_b_w, dproj):
    s = proj.shape[0]
    mlen, c2 = kv.shape
    c = c2 // 2
    hd = c // N_HEADS
    ts = _tile(s, TILE_ROWS_BRANCH_BWD)
    nt = s // ts
    scale = hd ** -0.5
    col, prev = _branch_specs(ts, c)

    def nxt(h, lead=None, ci=0):
        if lead is None:
            return pl.BlockSpec((h, c), lambda i: (jnp.minimum((i + 1) * (ts // h), s // h - 1), ci))
        return pl.BlockSpec((None, h, c), lambda i: (lead, jnp.minimum((i + 1) * (ts // h), s // h - 1), 0))

    def body(av, ag, sb, sc, sx, q, hav, hag, hsc, hsx, nsb, a1_ref, na1, dabo_ref, nda, ndb, kv_ref, caw, lng, lnb,
             cbw, dproj_in, dp_ref, dkv_ref, dwa_ref, misc_ref, exta, extda, extb, extdu, sha, shda):
        i = pl.program_id(0)
        not_first = (i > 0).astype(F32)
        not_last = (i < nt - 1).astype(F32)

        @pl.when(i == 0)
        def _():
            dkv_ref[...] = jnp.zeros_like(dkv_ref)
            dwa_ref[...] = jnp.zeros_like(dwa_ref)
            misc_ref[...] = jnp.zeros_like(misc_ref)

        def rowsum(t):
            return jnp.sum(t, axis=0, keepdims=True)

        def da1_of(a1v, da3):
            xhat, rstd, a2 = _layer_norm_parts(a1v, lng[...], lnb[...])
            sg = jax.nn.sigmoid(a2)
            da2 = da3 * (sg * (1.0 + a2 * (1.0 - sg)))
            dxh = da2 * lng[...]
            da1 = rstd * (dxh - jnp.mean(dxh, axis=-1, keepdims=True)
                          - xhat * jnp.mean(dxh * xhat, axis=-1, keepdims=True))
            return da1, da2, xhat

        da1, da2, xhat = da1_of(a1_ref[...], dabo_ref[0])
        misc_ref[pl.ds(0, 1), :] += rowsum(da1)
        misc_ref[pl.ds(1, 1), :] += rowsum(da2 * xhat)
        misc_ref[pl.ds(2, 1), :] += rowsum(da2)
        extda[pl.ds(0, ts), :] = da1
        extda[pl.ds(ts, HALO_A), :] = da1_of(na1[...], nda[...])[0] * not_last
        sga = jax.nn.sigmoid(ag[...])
        exta[pl.ds(0, HALO_A), :] = hav[...] * jax.nn.sigmoid(hag[...]) * not_first
        exta[pl.ds(HALO_A, ts), :] = av[...] * sga
        _shift_copies(exta, sha, ts + HALO_A - SUBLANES)
        _shift_copies(extda, shda, ts + HALO_A - SUBLANES)
        for r0 in range(0, ts, CONV_ROWS):
            rows = pl.ds(r0, CONV_ROWS)
            da1_rows = extda[rows, :]
            da0 = jnp.zeros((CONV_ROWS, c), F32)
            for k in range(CONV_A_K):
                tap = da1_rows * _rows_at(exta, sha, HALO_A - (CONV_A_K - 1) + k + r0, CONV_ROWS)
                dwa_ref[pl.ds(SUBLANES * k, SUBLANES), :] += tap.reshape(CONV_ROWS // SUBLANES, SUBLANES, c).sum(axis=0)
                da0 = da0 + caw[pl.ds(CONV_A_K - 1 - k, 1), :] * _rows_at(extda, shda, k + r0, CONV_ROWS)
            sg = sga[r0:r0 + CONV_ROWS]
            dp_ref[rows, 0:c] = (da0 * sg).astype(BF16)
            dp_ref[rows, c:2 * c] = (da0 * av[rows, :] * sg * (1.0 - sg)).astype(BF16)

        extb[pl.ds(0, HALO_B), :] = hsc[...] * hsx[...] * not_first
        extb[pl.ds(HALO_B, ts), :] = sc[...] * sx[...]
        dbu = dabo_ref[1]
        du = dbu * sb[...]
        extdu[pl.ds(0, ts), :] = du
        extdu[pl.ds(ts, HALO_B), :] = ndb[...] * nsb[...] * not_last
        u = jnp.zeros((ts, c), F32)
        dpr = jnp.zeros((ts, c), F32)
        for k in range(CONV_B_K):
            shifted = extb[pl.ds(HALO_B - (CONV_B_K - 1) + k, ts), :]
            u = u + cbw[pl.ds(k, 1), :] * shifted
            misc_ref[pl.ds(3 + k, 1), :] += rowsum(du * shifted)
            dpr = dpr + cbw[pl.ds(CONV_B_K - 1 - k, 1), :] * extdu[pl.ds(k, ts), :]
        dp_ref[:, 2 * c:3 * c] = (dbu * u).astype(BF16)
        dp_ref[:, 3 * c:4 * c] = (dpr * sx[...]).astype(BF16)
        dp_ref[:, 4 * c:5 * c] = (dpr * sc[...]).astype(BF16)

        nt_dims = (((1,), (1,)), ((), ()))
        tn_dims = (((0,), (0,)), ((), ()))
        for h in range(N_HEADS):
            lo, hi = h * hd, (h + 1) * hd
            qh = q[:, lo:hi].astype(BF16)
            kh = kv_ref[:, lo:hi]
            vh = kv_ref[:, c + lo:c + hi]
            p = _softmax_rows(lax.dot_general(qh, kh, nt_dims, preferred_element_type=F32) * scale)
            pb = p.astype(BF16)
            doh = dabo_ref[2, :, lo:hi].astype(BF16)
            dpm = lax.dot_general(doh, vh, nt_dims, preferred_element_type=F32)
            ds = (p * (dpm - jnp.sum(dpm * p, axis=-1, keepdims=True)) * scale).astype(BF16)
            dp_ref[:, 5 * c + lo:5 * c + hi] = jnp.dot(ds, kh, preferred_element_type=F32).astype(BF16)
            dkv_ref[:, lo:hi] += lax.dot_general(ds, qh, tn_dims, preferred_element_type=F32)
            dkv_ref[:, c + lo:c + hi] += lax.dot_general(pb, doh, tn_dims, preferred_element_type=F32)

    full = lambda shp: pl.BlockSpec(shp, lambda i: (0,) * len(shp))
    n_in_before_dproj = 21
    shifted = pltpu.VMEM((SUBLANES - 1, ts + HALO_A - SUBLANES, c), F32)
    dp, dkv, dwa, misc = pl.pallas_call(
        body, name="branch_bwd", grid=(nt,),
        in_specs=[col(0), col(1), col(2), col(3), col(4), col(5), prev(0, HALO_A), prev(1, HALO_A),
                  prev(3, HALO_B), prev(4, HALO_B), nxt(HALO_B, ci=2),
                  pl.BlockSpec((ts, c), lambda i: (i, 0)), nxt(HALO_A),
                  pl.BlockSpec((N_BRANCH, ts, c), lambda i: (0, i, 0)), nxt(HALO_A, lead=0), nxt(HALO_B, lead=1),
                  full((mlen, c2)), full(conv_a_w.shape), full((1, c)), full((1, c)), full(conv_b_w.shape),
                  pl.BlockSpec(memory_space=pl.ANY)],
        out_specs=[pl.BlockSpec((ts, 6 * c), lambda i: (i, 0)), full((mlen, c2)), full((HALO_A * SUBLANES, c)),
                   full((8, c))],
        out_shape=[jax.ShapeDtypeStruct(dproj.shape, BF16), jax.ShapeDtypeStruct((mlen, c2), F32),
                   jax.ShapeDtypeStruct((HALO_A * SUBLANES, c), F32), jax.ShapeDtypeStruct((8, c), F32)],
        scratch_shapes=[pltpu.VMEM((HALO_A + ts, c), F32), pltpu.VMEM((ts + HALO_A, c), F32),
                        pltpu.VMEM((HALO_B + ts, c), F32), pltpu.VMEM((ts + HALO_B, c), F32), shifted, shifted],
        input_output_aliases={n_in_before_dproj: 0},
        compiler_params=_params(),
    )(proj, proj, proj, proj, proj, proj, proj, proj, proj, proj, proj, a1, a1, dabo, dabo, dabo, kv, conv_a_w,
      ln_g, ln_b, conv_b_w, dproj)
    return dp, dkv, dwa.reshape(HALO_A, SUBLANES, c).sum(axis=1), misc


def _merge_fwd(abo, w_cat, proj, d, after):
    nb, s, c = abo.shape
    bm, bn = _tile(s, TILE_M), _tile(d, TILE_N)
    gate_col0 = (proj.shape[1] - nb * d) // bn

    def body(a_ref, w_ref, g_ref, after_ref, y_ref, m_ref, acc_ref):
        k = pl.program_id(2)
        y = jnp.dot(a_ref[...], w_ref[...], preferred_element_type=F32)
        y_ref[...] = y.astype(BF16)
        contrib = jax.nn.sigmoid(g_ref[...]) * y

        @pl.when(k == 0)
        def _():
            acc_ref[...] = contrib

        @pl.when(k > 0)
        def _():
            acc_ref[...] += contrib

        @pl.when(k == nb - 1)
        def _():
            m_ref[...] = acc_ref[...].astype(BF16)

    return pl.pallas_call(
        body, name="merge_fwd", grid=(s // bm, d // bn, nb),
        in_specs=[pl.BlockSpec((None, bm, c), lambda i, j, k: (k, i, 0)),
                  pl.BlockSpec((None, c, bn), lambda i, j, k: (k, 0, j)),
                  pl.BlockSpec((bm, bn), lambda i, j, k: (i, gate_col0 + k * (d // bn) + j)), _UNTOUCHED],
        out_specs=[pl.BlockSpec((None, bm, bn), lambda i, j, k: (k, i, j)),
                   pl.BlockSpec((bm, bn), lambda i, j, k: (i, j))],
        out_shape=[jax.ShapeDtypeStruct((nb, s, d), BF16), jax.ShapeDtypeStruct((s, d), BF16)],
        scratch_shapes=[pltpu.VMEM((bm, bn), F32)],
        compiler_params=_params(),
    )(abo, w_cat, proj, after)


def _merge_bwd(dz, w_o, y, proj):
    s, d = dz.shape
    nb = y.shape[0]
    nin = proj.shape[1]
    bm, bn = _tile(s, TILE_M), _tile(d, TILE_N)
    gate_col0 = (nin - nb * d) // bn

    def body(dz_ref, w_ref, y_ref, g_ref, dy_ref, dg_ref, acc_ref):
        @pl.when(pl.program_id(2) == 0)
        def _():
            acc_ref[...] = lax.dot_general(dz_ref[...], w_ref[...], (((1,), (1,)), ((), ())),
                                           preferred_element_type=F32)

        dm = acc_ref[...]
        gt = jax.nn.sigmoid(g_ref[...])
        dy_ref[...] = (dm * gt).astype(BF16)
        dg_ref[...] = (dm * y_ref[...].astype(F32) * gt * (1.0 - gt)).astype(BF16)

    gate = lambda i, j, k: (i, gate_col0 + k * (d // bn) + j)
    return pl.pallas_call(
        body, name="merge_bwd", grid=(s // bm, d // bn, nb),
        in_specs=[pl.BlockSpec((bm, d), lambda i, j, k: (i, 0)), pl.BlockSpec((bn, d), lambda i, j, k: (j, 0)),
                  pl.BlockSpec((None, bm, bn), lambda i, j, k: (k, i, j)), pl.BlockSpec((bm, bn), gate)],
        out_specs=[pl.BlockSpec((None, bm, bn), lambda i, j, k: (k, i, j)), pl.BlockSpec((bm, bn), gate)],
        out_shape=[jax.ShapeDtypeStruct((nb, s, d), BF16), jax.ShapeDtypeStruct((s, nin), BF16)],
        scratch_shapes=[pltpu.VMEM((bm, bn), F32)],
        compiler_params=_params(),
    )(dz, w_o, y, proj)


def _window(ref, axis, who, length):
    idx = [slice(None)] * len(ref.shape)
    idx[axis] = pl.ds(pl.multiple_of(who * length, length), length)
    return ref.at[tuple(idx)]


def _all_gather(name, shards, axes):
    n = len(shards)
    out_shapes = []
    for sh, ax in zip(shards, axes):
        shp = list(sh.shape)
        shp[ax] *= N_DEV
        out_shapes.append(jax.ShapeDtypeStruct(tuple(shp), sh.dtype))

    def body(*refs):
        ins, outs = refs[:n], refs[n:2 * n]
        send, recv, local = refs[2 * n:]
        me = _my_index()
        lens = [ins[a].shape[axes[a]] for a in range(n)]
        mine = [pltpu.make_async_copy(ins[a], _window(outs[a], axes[a], me, lens[a]), local.at[a]) for a in range(n)]
        for cp in mine:
            cp.start()
        pushes = []
        for d in range(1, N_DEV):
            to = (me + d) % N_DEV
            for a in range(n):
                cp = pltpu.make_async_remote_copy(
                    src_ref=ins[a], dst_ref=_window(outs[a], axes[a], me, lens[a]), send_sem=send.at[a, d],
                    recv_sem=recv.at[a, d], device_id=_mesh_id(to), device_id_type=pl.DeviceIdType.MESH)
                cp.start()
                pushes.append(cp)
        for d in range(1, N_DEV):
            frm = (me + N_DEV - d) % N_DEV
            for a in range(n):
                pltpu.make_async_remote_copy(
                    src_ref=ins[a], dst_ref=_window(outs[a], axes[a], frm, lens[a]), send_sem=send.at[a, d],
                    recv_sem=recv.at[a, d], device_id=_mesh_id(frm), device_id_type=pl.DeviceIdType.MESH).wait_recv()
        for cp in pushes:
            cp.wait_send()
        for cp in mine:
            cp.wait()

    hbm = pl.BlockSpec(memory_space=pl.ANY)
    return pl.pallas_call(
        body, name=name, in_specs=[hbm] * n, out_specs=[hbm] * n, out_shape=out_shapes,
        scratch_shapes=[pltpu.SemaphoreType.DMA((n, N_DEV)), pltpu.SemaphoreType.DMA((n, N_DEV)),
                        pltpu.SemaphoreType.DMA((n,))],
    )(*shards)


def _pair(a, d):
    return a * N_DEV + d


def _push_ends(kind, src, land, axis, length, me, to):
    if kind == "gather":
        return src, _window(land, axis, me, length)
    return _window(src, axis, to, length), land.at[me]


def _arrival_ends(kind, src, land, axis, length, me, frm):
    if kind == "gather":
        return src, _window(land, axis, frm, length)
    return _window(src, axis, me, length), land.at[frm]


def _exchange_start(name, kind, srcs, axes, after):
    n = len(srcs)
    if kind == "gather":
        lens = [s.shape[ax] for s, ax in zip(srcs, axes)]
        land_shapes = [s.shape[:ax] + (s.shape[ax] * N_DEV,) + s.shape[ax + 1:] for s, ax in zip(srcs, axes)]
    else:
        lens = [s.shape[ax] // N_DEV for s, ax in zip(srcs, axes)]
        land_shapes = [(N_DEV,) + s.shape[:ax] + (ln,) + s.shape[ax + 1:] for s, ax, ln in zip(srcs, axes, lens)]
    lands = [lax.empty(shp, s.dtype) for shp, s in zip(land_shapes, srcs)]

    def body(*refs):
        ins = refs[:n]
        send, recv = refs[2 * n + 1], refs[2 * n + 2]
        lnd = refs[2 * n + 3 + n:2 * n + 3 + 2 * n]
        token = refs[-1]
        me = _my_index()
        for a in range(n):
            for d in range(1, N_DEV):
                to = (me + d) % N_DEV
                src, dst = _push_ends(kind, ins[a], lnd[a], axes[a], lens[a], me, to)
                pltpu.make_async_remote_copy(src_ref=src, dst_ref=dst, send_sem=send.at[_pair(a, d)],
                                             recv_sem=recv.at[_pair(a, d)], device_id=_mesh_id(to),
                                             device_id_type=pl.DeviceIdType.MESH).start()
        for a in range(n):
            pltpu.make_async_copy(*_push_ends(kind, ins[a], lnd[a], axes[a], lens[a], me, me),
                                  send.at[_pair(a, 0)]).start()
        token[...] = jnp.zeros_like(token)

    hbm = pl.BlockSpec(memory_space=pltpu.HBM)
    sem = pl.BlockSpec(memory_space=pltpu.SEMAPHORE)
    held = [pltpu.with_memory_space_constraint(t, pltpu.HBM) for t in list(srcs) + lands]
    outs = pl.pallas_call(
        body, name=name,
        in_specs=[hbm] * (2 * n) + [_UNTOUCHED],
        out_specs=[sem, sem] + [hbm] * (2 * n) + [pl.BlockSpec(memory_space=pltpu.VMEM)],
        out_shape=[pltpu.SemaphoreType.DMA((n * N_DEV,)), pltpu.SemaphoreType.DMA((n * N_DEV,))]
        + [pltpu.HBM(t.shape, t.dtype) for t in held] + [jax.ShapeDtypeStruct((8, 128), F32)],
        input_output_aliases={i: 2 + i for i in range(2 * n)},
        compiler_params=pltpu.CompilerParams(has_side_effects=pltpu.SideEffectType.DATAFLOW_SIDE_EFFECTING),
    )(*held, after)
    handle = dict(kind=kind, axes=axes, lens=lens, send=outs[0], recv=outs[1], srcs=outs[2:2 + n],
                  lands=outs[2 + n:2 + 2 * n])
    return handle, outs[-1]


def _exchange_wait(name, handle, after):
    kind, axes, lens = handle["kind"], handle["axes"], handle["lens"]
    srcs, lands = handle["srcs"], handle["lands"]
    n = len(srcs)
    n_in = 2 * n + 2 + len(after)

    def body(*refs):
        ins = refs[:n]
        send, recv = refs[2 * n], refs[2 * n + 1]
        got = refs[n_in + n:n_in + 2 * n]
        me = _my_index()
        for d in range(1, N_DEV):
            frm = (me + N_DEV - d) % N_DEV
            for a in range(n):
                src, dst = _arrival_ends(kind, ins[a], got[a], axes[a], lens[a], me, frm)
                pltpu.make_async_remote_copy(src_ref=src, dst_ref=dst, send_sem=send.at[_pair(a, d)],
                                             recv_sem=recv.at[_pair(a, d)], device_id=_mesh_id(frm),
                                             device_id_type=pl.DeviceIdType.MESH).wait_recv()
        for d in range(1, N_DEV):
            to = (me + d) % N_DEV
            for a in range(n):
                src, dst = _push_ends(kind, ins[a], got[a], axes[a], lens[a], me, to)
                pltpu.make_async_remote_copy(src_ref=src, dst_ref=dst, send_sem=send.at[_pair(a, d)],
                                             recv_sem=recv.at[_pair(a, d)], device_id=_mesh_id(to),
                                             device_id_type=pl.DeviceIdType.MESH).wait_send()
        for a in range(n):
            pltpu.make_async_copy(*_push_ends(kind, ins[a], got[a], axes[a], lens[a], me, me),
                                  send.at[_pair(a, 0)]).wait()

    hbm = pl.BlockSpec(memory_space=pltpu.HBM)
    sem = pl.BlockSpec(memory_space=pltpu.SEMAPHORE)
    outs = pl.pallas_call(
        body, name=name,
        in_specs=[hbm] * (2 * n) + [sem, sem] + [_UNTOUCHED] * len(after),
        out_specs=[hbm] * (2 * n),
        out_shape=[pltpu.HBM(t.shape, t.dtype) for t in list(srcs) + list(lands)],
        input_output_aliases={i: i for i in range(2 * n)},
        compiler_params=pltpu.CompilerParams(has_side_effects=pltpu.SideEffectType.DATAFLOW_SIDE_EFFECTING),
    )(*srcs, *lands, handle["send"], handle["recv"], *after)
    return outs[n:]


def _all_reduce_small(packed):
    r, c = packed.shape

    def body(p_ref, o_ref, buf, send, recv):
        me = _my_index()
        buf[me] = p_ref[...]
        pushes = []
        for d in range(1, N_DEV):
            to = (me + d) % N_DEV
            cp = pltpu.make_async_remote_copy(src_ref=p_ref, dst_ref=buf.at[me], send_sem=send.at[d],
                                              recv_sem=recv.at[d], device_id=_mesh_id(to),
                                              device_id_type=pl.DeviceIdType.MESH)
            cp.start()
            pushes.append(cp)
        for d in range(1, N_DEV):
            frm = (me + N_DEV - d) % N_DEV
            pltpu.make_async_remote_copy(src_ref=p_ref, dst_ref=buf.at[frm], send_sem=send.at[d], recv_sem=recv.at[d],
                                         device_id=_mesh_id(frm), device_id_type=pl.DeviceIdType.MESH).wait_recv()
        for cp in pushes:
            cp.wait_send()
        acc = buf[0]
        for s in range(1, N_DEV):
            acc = acc + buf[s]
        o_ref[...] = acc

    vmem = pl.BlockSpec(memory_space=pltpu.VMEM)
    return pl.pallas_call(
        body, name="all_reduce_small", in_specs=[vmem], out_specs=vmem, out_shape=jax.ShapeDtypeStruct((r, c), F32),
        scratch_shapes=[pltpu.VMEM((N_DEV, r, c), F32), pltpu.SemaphoreType.DMA((N_DEV,)),
                        pltpu.SemaphoreType.DMA((N_DEV,))],
        compiler_params=_params(),
    )(packed)


def _layer_fwd_branches(x, mem, sm, wf, after):
    h = _rmsnorm("rms_mix_pre", x, sm["g_mix_pre"], after)
    proj = _mm_nn("proj", h, wf["w_in"], F32)
    mem_n = _rmsnorm("rms_mem", mem, sm["g_mem"], mem)
    kv = _mm_nn("kv", mem_n, wf["w_kv"], BF16)
    a1, abo = _branch_fwd(proj, kv, wf["conv_a_w"], sm["conv_a_b"], sm["ln_a_g"], sm["ln_a_b"], wf["conv_b_w"])
    return dict(x=x, h=h, proj=proj, mem_n=mem_n, kv=kv, a1=a1, abo=abo)


def _layer_fwd_rest(sm, wf, sv, after):
    x = sv["x"]
    y, merged = _merge_fwd(sv["abo"], wf["w_cat"], sv["proj"], x.shape[1], after)
    z = _mm_nn("mix_out", merged, wf["w_o"], F32)
    x1 = _res_norm("mix_out_norm", z, x, sm["g_mix_post"])
    h2 = _rmsnorm("rms_mlp_pre", x1, sm["g_mlp_pre"], x1)
    r, act = _mm_relu2("mlp_up", h2, wf["w_up"])
    f = _mm_nn("mlp_down", act, wf["w_down"], F32)
    x2 = _res_norm("mlp_down_norm", f, x1, sm["g_mlp_post"])
    sv.update(y=y, merged=merged, z=z, x1=x1, h2=h2, r=r, act=act, f=f)
    return x2


def _pad_rows(t):
    return jnp.pad(t, ((0, (-t.shape[0]) % 8), (0, 0)))


def _layer_bwd(dx2, mem, sm, wf, sv, send_off):
    c = sv["a1"].shape[1]
    df, dg_mlp_post = _rms_bwd("rms_bwd_post", dx2, sv["f"], sm["g_mlp_post"], BF16, dx2)
    d_up = _mm_relu2_bwd("mlp_down_bwd", df, wf["w_down"], sv["r"])
    gw_down = _mm_tn("gw_down", sv["act"], df, BF16)
    dh2 = _mm_nt("mlp_up_bwd", d_up, wf["w_up"], F32)
    gw_up = _mm_tn("gw_up", sv["h2"], d_up, BF16)
    sent = send_off(dict(w_up=gw_up, w_down=gw_down))
    dx1, dg_mlp_pre = _rms_bwd("rms_bwd_pre", dh2, sv["x1"], sm["g_mlp_pre"], F32, sent, residual=dx2)
    dz, dg_mix_post = _rms_bwd("rms_bwd_post", dx1, sv["z"], sm["g_mix_post"], BF16, dx1)
    dy, dproj = _merge_bwd(dz, wf["w_o"], sv["y"], sv["proj"])
    gw_o = _mm_tn("gw_o", sv["merged"], dz, BF16)
    dabo = _mm_nt("branch_out_bwd", dy, wf["w_cat"], F32)
    gw_cat = _mm_tn("gw_branch_out", sv["abo"], dy, BF16)
    dproj, dkv, dconv_a_w, misc = _branch_bwd(sv["proj"], sv["a1"], dabo, sv["kv"], wf["conv_a_w"], sm["ln_a_g"],
                                              sm["ln_a_b"], wf["conv_b_w"], dproj)
    dkv = dkv.astype(BF16)
    gw_kv = _mm_tn("gw_kv", sv["mem_n"], dkv, BF16)
    dmem_n = _mm_nt("kv_bwd", dkv, wf["w_kv"], F32)
    _, dg_mem = _rms_bwd("rms_bwd_mem", dmem_n, mem, sm["g_mem"], BF16, dmem_n)
    gw_in = _mm_tn("gw_in", sv["h"], dproj, BF16)
    sent = send_off(dict(w_in=gw_in, w_kv=gw_kv, w_cat=gw_cat, w_o=gw_o))
    dh = _mm_nt("proj_bwd", dproj, wf["w_in"], F32, after=sent)
    dx, dg_mix_pre = _rms_bwd("rms_bwd_pre", dh, sv["x"], sm["g_mix_pre"], F32, dh, residual=dx1)
    rows = [_pad_rows(t.reshape(2, c)) for t in (dg_mix_pre, dg_mem, dg_mix_post, dg_mlp_pre, dg_mlp_post)]
    return dx, jnp.concatenate(rows + [misc, dconv_a_w], axis=0)


_SMALL_D_NAMES = ["g_mix_pre", "g_mem", "g_mix_post", "g_mlp_pre", "g_mlp_post"]
_SMALL_C_NAMES = ["conv_a_b", "ln_a_g", "ln_a_b"]
_SMALL_MISC_ROW = 8 * len(_SMALL_D_NAMES)
_SMALL_CONV_B_ROW = _SMALL_MISC_ROW + len(_SMALL_C_NAMES)
_SMALL_CONV_A_ROW = _SMALL_MISC_ROW + 8
_SMALL_ROWS = _SMALL_CONV_A_ROW + HALO_A


def kernel(x, mem, g_mix_pre, w_in, conv_a_w, conv_a_b, ln_a_g, ln_a_b, w_a_out, conv_b_w, w_b_out, g_mem, w_kv, w_x_out, w_o, g_mix_post, g_mlp_pre, w_up, w_down, g_mlp_post, loss_target, m_g_mix_pre, m_w_in, m_conv_a_w, m_conv_a_b, m_ln_a_g, m_ln_a_b, m_w_a_out, m_conv_b_w, m_w_b_out, m_g_mem, m_w_kv, m_w_x_out, m_w_o, m_g_mix_post, m_g_mlp_pre, m_w_up, m_w_down, m_g_mlp_post, v_g_mix_pre, v_w_in, v_conv_a_w, v_conv_a_b, v_ln_a_g, v_ln_a_b, v_w_a_out, v_conv_b_w, v_w_b_out, v_g_mem, v_w_kv, v_w_x_out, v_w_o, v_g_mix_post, v_g_mlp_pre, v_w_up, v_w_down, v_g_mlp_post):
    names = ["g_mix_pre", "w_in", "conv_a_w", "conv_a_b", "ln_a_g", "ln_a_b", "w_a_out", "conv_b_w", "w_b_out",
             "g_mem", "w_kv", "w_x_out", "w_o", "g_mix_post", "g_mlp_pre", "w_up", "w_down", "g_mlp_post"]
    w = dict(zip(names, [g_mix_pre, w_in, conv_a_w, conv_a_b, ln_a_g, ln_a_b, w_a_out, conv_b_w, w_b_out, g_mem,
                         w_kv, w_x_out, w_o, g_mix_post, g_mlp_pre, w_up, w_down, g_mlp_post]))
    mo = dict(zip(names, [m_g_mix_pre, m_w_in, m_conv_a_w, m_conv_a_b, m_ln_a_g, m_ln_a_b, m_w_a_out, m_conv_b_w,
                          m_w_b_out, m_g_mem, m_w_kv, m_w_x_out, m_w_o, m_g_mix_post, m_g_mlp_pre, m_w_up, m_w_down,
                          m_g_mlp_post]))
    vo = dict(zip(names, [v_g_mix_pre, v_w_in, v_conv_a_w, v_conv_a_b, v_ln_a_g, v_ln_a_b, v_w_a_out, v_conv_b_w,
                          v_w_b_out, v_g_mem, v_w_kv, v_w_x_out, v_w_o, v_g_mix_post, v_g_mlp_pre, v_w_up, v_w_down,
                          v_g_mlp_post]))
    depth = w_in.shape[0]
    c = conv_a_b.shape[1]
    cs = conv_a_w.shape[2]
    me = _my_index()
    xs, mems, tgt = x[0], mem[0], loss_target[0]

    big_names = ["w_in", "w_kv", "w_cat", "w_o", "w_up", "w_down"]
    big_axes = [1, 0, 2, 0, 1, 0]
    n_early = 2
    branch_names = ["w_a_out", "w_b_out", "w_x_out"]
    smalls = [{k: w[k][l][None, :] for k in _SMALL_D_NAMES + _SMALL_C_NAMES} for l in range(depth)]

    def weight_shards(l):
        return [w_in[l].astype(BF16), w_kv[l].astype(BF16), jnp.stack([w[k][l] for k in branch_names]).astype(BF16),
                w_o[l].astype(BF16), w_up[l].astype(BF16), w_down[l].astype(BF16)]

    def start_gather(l, after):
        shards = weight_shards(l)
        early, token = _exchange_start(f"gather_start_{l}_early", "gather", shards[:n_early], big_axes[:n_early],
                                       after)
        late, token = _exchange_start(f"gather_start_{l}_late", "gather", shards[n_early:], big_axes[n_early:], token)
        return early, late, token

    taps_a, taps_b = _all_gather(
        "gather_conv_taps",
        [jnp.pad(conv_a_w, ((0, 0), (0, HALO_A - CONV_A_K), (0, 0))),
         jnp.pad(conv_b_w, ((0, 0), (0, HALO_B - CONV_B_K), (0, 0)))], [2, 2])

    early, late, token = start_gather(0, taps_a)
    fulls = [None] * depth
    saved = []
    xc = xs
    for l in range(depth):
        fulls[l] = dict(zip(big_names[:n_early], _exchange_wait(f"gather_wait_{l}_early", early, [xc])))
        fulls[l]["conv_a_w"], fulls[l]["conv_b_w"] = taps_a[l], taps_b[l]
        sv = _layer_fwd_branches(xc, mems, smalls[l], fulls[l], token)
        fulls[l].update(zip(big_names[n_early:], _exchange_wait(f"gather_wait_{l}_late", late, [sv["abo"]])))
        token = fulls[l]["w_o"]
        if l + 1 < depth:
            early, late, token = start_gather(l + 1, token)
        xc = _layer_fwd_rest(smalls[l], fulls[l], sv, token)
        saved.append(sv)
    loss_part, dx = _loss_head(xc, tgt)
    loss = lax.psum(loss_part, MESH_AXES)

    upd = {}

    def adamw_group(l, group, slots, after):
        for k, sl in zip(group, slots):
            if k == "w_cat":
                for b, nm in enumerate(branch_names):
                    upd[nm] = _adamw("adamw_" + nm, sl, b, w[nm], mo[nm], vo[nm], l, upd.get(nm), after)
            else:
                upd[k] = _adamw("adamw_" + k, sl, None, w[k], mo[k], vo[k], l, upd.get(k), after)

    small_parts = [None] * depth
    in_flight = [None]

    def take_in(after):
        handle, l, group = in_flight[0]
        return _exchange_wait(f"scatter_wait_{l}_{group[0]}", handle, after), l, group

    def send_off_layer(l):
        def send_off(grads):
            group = [k for k in big_names if k in grads]
            arrays = [grads[k] for k in group]
            arrived = None if in_flight[0] is None else take_in([arrays[0]])
            handle, token = _exchange_start(f"scatter_start_{l}_{group[0]}", "scatter", arrays,
                                            [big_axes[big_names.index(k)] for k in group],
                                            arrays[0] if arrived is None else arrived[0][0])
            in_flight[0] = (handle, l, group)
            if arrived is not None:
                adamw_group(arrived[1], arrived[2], arrived[0], token)
            return token
        return send_off

    for l in reversed(range(depth)):
        dx, small_parts[l] = _layer_bwd(dx, mems, smalls[l], fulls[l], saved[l], send_off_layer(l))

    tot = _all_reduce_small(jnp.concatenate(small_parts, axis=0)).reshape(depth, _SMALL_ROWS, c)
    g_small = {}
    for n_, nm in enumerate(_SMALL_D_NAMES):
        g_small[nm] = tot[:, 8 * n_:8 * n_ + 2, :].reshape(depth, 2 * c)
    for n_, nm in enumerate(_SMALL_C_NAMES):
        g_small[nm] = tot[:, _SMALL_MISC_ROW + n_, :]
    g_taps = {"conv_a_w": tot[:, _SMALL_CONV_A_ROW:_SMALL_CONV_A_ROW + CONV_A_K, :],
              "conv_b_w": tot[:, _SMALL_CONV_B_ROW:_SMALL_CONV_B_ROW + CONV_B_K, :]}

    def pack(t):
        return jnp.concatenate([_pad_rows(t[nm].reshape(-1, c)) for nm in _SMALL_D_NAMES + _SMALL_C_NAMES], axis=0)

    res = _adamw("adamw_small", pack(g_small)[None], None, pack(w)[None], pack(mo)[None], pack(vo)[None], 0, None, tot)
    row = 0
    for nm in _SMALL_D_NAMES + _SMALL_C_NAMES:
        n_rows = w[nm].size // c
        upd[nm] = [t[0, row:row + n_rows].reshape(w[nm].shape) for t in res]
        row += n_rows + (-n_rows) % 8
    for nm, g in g_taps.items():
        shp = w[nm].shape
        mine = lax.dynamic_slice_in_dim(g, me * cs, cs, axis=2).reshape(1, -1, cs)
        res = _adamw("adamw_" + nm, mine, None, w[nm].reshape(1, -1, cs), mo[nm].reshape(1, -1, cs),
                     vo[nm].reshape(1, -1, cs), 0, None, tot)
        upd[nm] = [t.reshape(shp) for t in res]
    done = [u[0] for u in upd.values()]
    slots, l, group = take_in(done + [dx])
    adamw_group(l, group, slots, tot)

    return (loss, dx[None], *[upd[nm][0] for nm in names], *[upd[nm][1] for nm in names],
            *[upd[nm][2] for nm in names], *[upd[nm][3] for nm in names])
```

```python
import math

import jax
import jax.numpy as jnp
from jax import lax
from jax.experimental import pallas as pl
from jax.experimental.pallas import tpu as pltpu

F32 = jnp.float32
BF16 = jnp.bfloat16

EPS = 1e-6
N_HEADS = 4
N_BRANCH = 3
CONV_A_K = 31
CONV_B_K = 3
HALO_A = 32
HALO_B = 8
N_DEV = 8
MESH_AXES = ("x", "y", "c")

ADAM_LR = 0.001
ADAM_B1 = 0.9
ADAM_B2 = 0.999
ADAM_EPS = 1e-08
ADAM_WD = 0.01
ADAM_STEP = 10

V7X_VMEM_BYTES = 64 * 1024 * 1024
VMEM_LIMIT = V7X_VMEM_BYTES - 8 * 1024 * 1024

TILE_M = 1024
TILE_N = 1024
TILE_K = 4096
TILE_ROWS_WIDE = 512
TILE_ROWS_BRANCH = 256
TILE_ROWS_BRANCH_BWD = 128


def _params():
    return pltpu.CompilerParams(vmem_limit_bytes=VMEM_LIMIT)


def _tile(n, t):
    t = min(n, t)
    assert n % t == 0, (n, t)
    return t


def _my_index():
    return 4 * lax.axis_index("x") + 2 * lax.axis_index("y") + lax.axis_index("c")


def _mesh_id(p):
    return (p // 4, (p // 2) % 2, p % 2)


def _mm(name, grid, a, a_spec, b, b_spec, *, ta, tb, acc_shape, extras, outs, epilogue):
    nk = grid[-1]
    lead = len(grid) - 3
    ne, no = len(extras), len(outs)
    dn = (((0,) if ta else (1,), (1,) if tb else (0,)), ((), ()))

    def body(a_ref, b_ref, *rest):
        extra_refs = rest[:ne]
        out_refs = rest[ne:ne + no]
        i, j, k = pl.program_id(lead), pl.program_id(lead + 1), pl.program_id(lead + 2)
        prod = lax.dot_general(a_ref[...], b_ref[...], dn, preferred_element_type=F32)
        if nk == 1:
            epilogue(prod, extra_refs, out_refs, i, j)
        else:
            acc_ref = rest[ne + no]

            @pl.when(k == 0)
            def _():
                acc_ref[...] = prod

            @pl.when(k > 0)
            def _():
                acc_ref[...] += prod

            @pl.when(k == nk - 1)
            def _():
                epilogue(acc_ref[...], extra_refs, out_refs, i, j)

    return pl.pallas_call(
        body, name=name, grid=grid,
        in_specs=[a_spec, b_spec] + [s for _, s in extras],
        out_specs=[s for _, s in outs],
        out_shape=[o for o, _ in outs],
        scratch_shapes=[pltpu.VMEM(acc_shape, F32)] if nk > 1 else [],
        compiler_params=_params(),
    )(a, b, *[e for e, _ in extras])


def _store_epilogue(res, extra_refs, out_refs, i, j):
    out_refs[0][...] = res.astype(out_refs[0].dtype)


def _mm_nn(name, a, b, out_dtype):
    m, kd = a.shape
    n = b.shape[1]
    bm, bn, bk = _tile(m, TILE_M), _tile(n, TILE_N), _tile(kd, TILE_K)
    return _mm(name, (m // bm, n // bn, kd // bk), a, pl.BlockSpec((bm, bk), lambda i, j, k: (i, k)),
               b, pl.BlockSpec((bk, bn), lambda i, j, k: (k, j)), ta=False, tb=False, acc_shape=(bm, bn), extras=[],
               outs=[(jax.ShapeDtypeStruct((m, n), out_dtype), pl.BlockSpec((bm, bn), lambda i, j, k: (i, j)))],
               epilogue=_store_epilogue)[0]


def _mm_nt(name, a, b, out_dtype, after=None):
    m, kd = a.shape[-2:]
    n = b.shape[-2]
    bm, bn = _tile(m, TILE_M), _tile(n, TILE_N)
    if a.ndim == 2:
        bk = _tile(kd, TILE_K)
        return _mm(name, (m // bm, n // bn, kd // bk), a, pl.BlockSpec((bm, bk), lambda i, j, k: (i, k)),
                   b, pl.BlockSpec((bn, bk), lambda i, j, k: (j, k)), ta=False, tb=True, acc_shape=(bm, bn),
                   extras=[] if after is None else [(after, _UNTOUCHED)],
                   outs=[(jax.ShapeDtypeStruct((m, n), out_dtype), pl.BlockSpec((bm, bn), lambda i, j, k: (i, j)))],
                   epilogue=_store_epilogue)[0]
    nb = a.shape[0]
    return _mm(name, (nb, m // bm, n // bn, 1), a, pl.BlockSpec((None, bm, kd), lambda s, i, j, k: (s, i, 0)),
               b, pl.BlockSpec((None, bn, kd), lambda s, i, j, k: (s, j, 0)), ta=False, tb=True, acc_shape=(bm, bn),
               extras=[],
               outs=[(jax.ShapeDtypeStruct((nb, m, n), out_dtype),
                      pl.BlockSpec((None, bm, bn), lambda s, i, j, k: (s, i, j)))],
               epilogue=_store_epilogue)[0]


def _mm_tn(name, a, b, out_dtype):
    r, m = a.shape[-2:]
    n = b.shape[-1]
    bm, bn, bk = _tile(m, TILE_M), _tile(n, TILE_N), _tile(r, TILE_K)
    if a.ndim == 2:
        return _mm(name, (m // bm, n // bn, r // bk), a, pl.BlockSpec((bk, bm), lambda i, j, k: (k, i)),
                   b, pl.BlockSpec((bk, bn), lambda i, j, k: (k, j)), ta=True, tb=False, acc_shape=(bm, bn), extras=[],
                   outs=[(jax.ShapeDtypeStruct((m, n), out_dtype), pl.BlockSpec((bm, bn), lambda i, j, k: (i, j)))],
                   epilogue=_store_epilogue)[0]
    nb = a.shape[0]
    return _mm(name, (nb, m // bm, n // bn, r // bk), a, pl.BlockSpec((None, bk, bm), lambda s, i, j, k: (s, k, i)),
               b, pl.BlockSpec((None, bk, bn), lambda s, i, j, k: (s, k, j)), ta=True, tb=False, acc_shape=(bm, bn),
               extras=[],
               outs=[(jax.ShapeDtypeStruct((nb, m, n), out_dtype),
                      pl.BlockSpec((None, bm, bn), lambda s, i, j, k: (s, i, j)))],
               epilogue=_store_epilogue)[0]


def _mm_relu2(name, h, w):
    m, kd = h.shape
    n = w.shape[1]
    bm, bn = _tile(m, TILE_M), _tile(n, TILE_N)

    def epilogue(res, extra_refs, out_refs, i, j):
        r = jnp.maximum(res, 0.0)
        out_refs[0][...] = r.astype(BF16)
        out_refs[1][...] = (r * r).astype(BF16)

    o = jax.ShapeDtypeStruct((m, n), BF16)
    spec = pl.BlockSpec((bm, bn), lambda i, j, k: (i, j))
    return _mm(name, (m // bm, n // bn, 1), h, pl.BlockSpec((bm, kd), lambda i, j, k: (i, 0)),
               w, pl.BlockSpec((kd, bn), lambda i, j, k: (0, j)), ta=False, tb=False, acc_shape=(bm, bn), extras=[],
               outs=[(o, spec), (o, spec)], epilogue=epilogue)


def _mm_relu2_bwd(name, df, w_down, r):
    m, kd = df.shape
    n = w_down.shape[0]
    bm, bn = _tile(m, TILE_M), _tile(n, TILE_N)

    def epilogue(res, extra_refs, out_refs, i, j):
        out_refs[0][...] = (res * (2.0 * extra_refs[0][...].astype(F32))).astype(BF16)

    spec = pl.BlockSpec((bm, bn), lambda i, j, k: (i, j))
    return _mm(name, (m // bm, n // bn, 1), df, pl.BlockSpec((bm, kd), lambda i, j, k: (i, 0)),
               w_down, pl.BlockSpec((bn, kd), lambda i, j, k: (j, 0)), ta=False, tb=True, acc_shape=(bm, bn),
               extras=[(r, spec)], outs=[(jax.ShapeDtypeStruct((m, n), BF16), spec)], epilogue=epilogue)[0]


def _rms_bwd_math(dy, x, g):
    rr = lax.rsqrt(jnp.mean(x * x, axis=-1, keepdims=True) + EPS)
    gy = dy * g
    dx = rr * gy - x * (rr * rr * rr * jnp.mean(x * gy, axis=-1, keepdims=True))
    dg_rows = dy * x * rr
    return dx, dg_rows


_UNTOUCHED = pl.BlockSpec(memory_space=pl.ANY)


def _rmsnorm(name, x, g, after):
    r, d = x.shape
    br = _tile(r, TILE_ROWS_WIDE)

    def body(x_ref, g_ref, after_ref, o_ref):
        xv = x_ref[...]
        rr = lax.rsqrt(jnp.mean(xv * xv, axis=-1, keepdims=True) + EPS)
        o_ref[...] = ((xv * rr) * g_ref[...]).astype(BF16)

    row = pl.BlockSpec((br, d), lambda i: (i, 0))
    return pl.pallas_call(body, name=name, grid=(r // br,),
                          in_specs=[row, pl.BlockSpec((1, d), lambda i: (0, 0)), _UNTOUCHED],
                          out_specs=row, out_shape=jax.ShapeDtypeStruct((r, d), BF16),
                          compiler_params=_params())(x, g, after)


def _res_norm(name, z, x, g, g_next=None):
    r, d = x.shape
    br = _tile(r, TILE_ROWS_WIDE)
    n_next = 0 if g_next is None else 1

    def body(z_ref, x_ref, g_ref, *rest):
        zv = z_ref[...]
        rr = lax.rsqrt(jnp.mean(zv * zv, axis=-1, keepdims=True) + EPS)
        xn = x_ref[...] + (zv * rr) * g_ref[...]
        rest[n_next][...] = xn
        if n_next:
            r2 = lax.rsqrt(jnp.mean(xn * xn, axis=-1, keepdims=True) + EPS)
            rest[2][...] = ((xn * r2) * rest[0][...]).astype(BF16)

    row = pl.BlockSpec((br, d), lambda i: (i, 0))
    vec = pl.BlockSpec((1, d), lambda i: (0, 0))
    outs = pl.pallas_call(body, name=name, grid=(r // br,), in_specs=[row, row, vec] + [vec] * n_next,
                          out_specs=[row] * (1 + n_next),
                          out_shape=[jax.ShapeDtypeStruct((r, d), F32)] + [jax.ShapeDtypeStruct((r, d), BF16)] * n_next,
                          compiler_params=_params())(z, x, g, *([] if g_next is None else [g_next]))
    return outs if n_next else (outs[0], None)


def _rms_bwd(name, dy, x, g, out_dtype, after, residual=None, below=None):
    r, d = x.shape
    br = _tile(r, TILE_ROWS_WIDE if below is None else TILE_ROWS_WIDE // 2)
    n_res = 0 if residual is None else 1
    n_below = 0 if below is None else 1

    def body(dy_ref, x_ref, g_ref, after_ref, *rest):
        ins, outs = rest[:n_res + 2 * n_below], rest[n_res + 2 * n_below:]
        first = pl.program_id(0) == 0
        dx, dg_rows = _rms_bwd_math(dy_ref[...], x_ref[...], g_ref[...])
        if n_res:
            dx = dx + ins[0][...]
        outs[0][...] = dx.astype(out_dtype)

        @pl.when(first)
        def _():
            outs[1][...] = jnp.zeros_like(outs[1])

        outs[1][...] += jnp.sum(dg_rows, axis=0, keepdims=True)
        if n_below:
            dz, dgz_rows = _rms_bwd_math(dx, ins[n_res][...], ins[n_res + 1][...])
            outs[2][...] = dz.astype(BF16)

            @pl.when(first)
            def _():
                outs[3][...] = jnp.zeros_like(outs[3])

            outs[3][...] += jnp.sum(dgz_rows, axis=0, keepdims=True)

    row = pl.BlockSpec((br, d), lambda i: (i, 0))
    vec = pl.BlockSpec((1, d), lambda i: (0, 0))
    o_row, o_vec = jax.ShapeDtypeStruct((r, d), out_dtype), jax.ShapeDtypeStruct((1, d), F32)
    return pl.pallas_call(
        body, name=name, grid=(r // br,),
        in_specs=[row, row, vec, _UNTOUCHED] + [row] * n_res + [row, vec] * n_below,
        out_specs=[row, vec] + [row, vec] * n_below,
        out_shape=[o_row, o_vec] + [jax.ShapeDtypeStruct((r, d), BF16), o_vec] * n_below,
        compiler_params=_params(),
    )(dy, x, g, after, *([] if residual is None else [residual]), *([] if below is None else below))


def _loss_head(y, target):
    r, d = y.shape
    br = _tile(r, TILE_ROWS_WIDE)

    def body(y_ref, t_ref, dy_ref, l_ref):
        diff = y_ref[...] - t_ref[...]
        dy_ref[...] = diff * (1.0 / d)

        @pl.when(pl.program_id(0) == 0)
        def _():
            l_ref[...] = jnp.zeros_like(l_ref)

        l_ref[...] += 0.5 * jnp.sum(jnp.mean(diff * diff, axis=-1, keepdims=True))

    row = pl.BlockSpec((br, d), lambda i: (i, 0))
    one = pl.BlockSpec((8, 128), lambda i: (0, 0))
    dy, l = pl.pallas_call(body, name="loss_head", grid=(r // br,), in_specs=[row, row], out_specs=[row, one],
                           out_shape=[jax.ShapeDtypeStruct((r, d), F32), jax.ShapeDtypeStruct((8, 128), F32)],
                           compiler_params=_params())(y, target)
    return l[0, 0], dy


def _adamw(name, parts, branch, w, m, v, layer, prev, after):
    _, r, c = w.shape
    p = parts.shape[0]
    br = r
    for cand in (512, 256, 128, 64, 32, 16):
        if r % cand == 0 and cand * c * 4 <= 2 * 1024 * 1024:
            br = cand
            break
    bc1 = 1.0 - ADAM_B1 ** ADAM_STEP
    bc2 = 1.0 - ADAM_B2 ** ADAM_STEP
    if prev is None:
        prev = [lax.empty(w.shape, F32) for _ in range(4)]

    def body(p_ref, w_ref, m_ref, v_ref, after_ref, pg, pd, pm, pv, g_out, d_out, m_out, v_out):
        g = p_ref[0].astype(F32)
        for s in range(1, p):
            g = g + p_ref[s].astype(F32)
        m2 = ADAM_B1 * m_ref[...] + (1.0 - ADAM_B1) * g
        v2 = ADAM_B2 * v_ref[...] + (1.0 - ADAM_B2) * (g * g)
        m_hat = m2 / bc1
        v_hat = v2 / bc2
        g_out[...] = g
        d_out[...] = -ADAM_LR * (m_hat / (jnp.sqrt(v_hat) + ADAM_EPS) + ADAM_WD * w_ref[...])
        m_out[...] = m2
        v_out[...] = v2

    if branch is None:
        p_spec = pl.BlockSpec((p, br, c), lambda i: (0, i, 0))
    else:
        p_spec = pl.BlockSpec((p, None, br, c), lambda i: (0, branch, i, 0))
    slab = pl.BlockSpec((None, br, c), lambda i: (layer, i, 0))
    o = jax.ShapeDtypeStruct(w.shape, F32)
    return pl.pallas_call(body, name=name, grid=(r // br,),
                          in_specs=[p_spec, slab, slab, slab, _UNTOUCHED] + [_UNTOUCHED] * 4,
                          out_specs=[slab] * 4, out_shape=[o] * 4, input_output_aliases={5: 0, 6: 1, 7: 2, 8: 3},
                          compiler_params=_params())(parts, w, m, v, after, *prev)


def _layer_norm_parts(a1, ln_g, ln_b):
    mu = jnp.mean(a1, axis=-1, keepdims=True)
    xc = a1 - mu
    rstd = lax.rsqrt(jnp.mean(xc * xc, axis=-1, keepdims=True) + EPS)
    xhat = xc * rstd
    a2 = xhat * ln_g + ln_b
    return xhat, rstd, a2


def _softmax_rows(s):
    e = jnp.exp(s - jnp.max(s, axis=-1, keepdims=True))
    return e / jnp.sum(e, axis=-1, keepdims=True)


SUBLANES = 8
CONV_ROWS = 32


def _shift_copies(ext_ref, sh_ref, rows):
    for r in range(1, SUBLANES):
        sh_ref[r - 1, pl.ds(0, rows), :] = ext_ref[pl.ds(r, rows), :]


def _fill_tap_tiles(taps_ref, tile_ref):
    @pl.when(pl.program_id(0) == 0)
    def _():
        for k in range(CONV_A_K):
            tile_ref[k] = jnp.broadcast_to(taps_ref[pl.ds(k, 1), :], tile_ref.shape[1:])


def _rows_at(ext_ref, sh_ref, offset, ts):
    q, r = divmod(offset, SUBLANES)
    if r == 0:
        return ext_ref[pl.ds(SUBLANES * q, ts), :]
    return sh_ref[r - 1, pl.ds(SUBLANES * q, ts), :]


def _branch_specs(ts, c):
    def col(ci):
        return pl.BlockSpec((ts, c), lambda i: (i, ci))

    def prev(ci, h):
        return pl.BlockSpec((h, c), lambda i: (jnp.maximum(i * (ts // h) - 1, 0), ci))

    return col, prev


def _branch_fwd(proj, kv, conv_a_w, conv_a_b, ln_g, ln_b, conv_b_w):
    s = proj.shape[0]
    mlen, c2 = kv.shape
    c = c2 // 2
    hd = c // N_HEADS
    ts = _tile(s, TILE_ROWS_BRANCH)
    scale = hd ** -0.5
    col, prev = _branch_specs(ts, c)

    def body(av, ag, sb, sc, sx, q, hav, hag, hsc, hsx, kv_ref, caw, cab, lng, lnb, cbw, a1_ref, abo_ref, exta, extb,
             sha, wtile):
        not_first = (pl.program_id(0) > 0).astype(F32)
        _fill_tap_tiles(caw, wtile)
        exta[pl.ds(0, HALO_A), :] = hav[...] * jax.nn.sigmoid(hag[...]) * not_first
        exta[pl.ds(HALO_A, ts), :] = av[...] * jax.nn.sigmoid(ag[...])
        _shift_copies(exta, sha, ts + HALO_A - SUBLANES)
        chunk = (CONV_ROWS // SUBLANES, SUBLANES, c)
        for r0 in range(0, ts, CONV_ROWS):
            acc = jnp.broadcast_to(cab[...], chunk)
            for k in range(CONV_A_K):
                rows = _rows_at(exta, sha, HALO_A - (CONV_A_K - 1) + k + r0, CONV_ROWS)
                acc = acc + wtile[k][None] * rows.reshape(chunk)
            a1_ref[pl.ds(r0, CONV_ROWS), :] = acc.reshape(CONV_ROWS, c)
        _, _, a2 = _layer_norm_parts(a1_ref[...], lng[...], lnb[...])
        abo_ref[0] = (a2 * jax.nn.sigmoid(a2)).astype(BF16)
        extb[pl.ds(0, HALO_B), :] = hsc[...] * hsx[...] * not_first
        extb[pl.ds(HALO_B, ts), :] = sc[...] * sx[...]
        u = cbw[pl.ds(0, 1), :] * extb[pl.ds(HALO_B - (CONV_B_K - 1), ts), :]
        for k in range(1, CONV_B_K):
            u = u + cbw[pl.ds(k, 1), :] * extb[pl.ds(HALO_B - (CONV_B_K - 1) + k, ts), :]
        abo_ref[1] = (sb[...] * u).astype(BF16)
        for h in range(N_HEADS):
            qh = q[:, h * hd:(h + 1) * hd].astype(BF16)
            kh = kv_ref[:, h * hd:(h + 1) * hd]
            vh = kv_ref[:, c + h * hd:c + (h + 1) * hd]
            sc_ = lax.dot_general(qh, kh, (((1,), (1,)), ((), ())), preferred_element_type=F32) * scale
            p = _softmax_rows(sc_).astype(BF16)
            abo_ref[2, :, h * hd:(h + 1) * hd] = jnp.dot(p, vh, preferred_element_type=F32).astype(BF16)

    full = lambda shp: pl.BlockSpec(shp, lambda i: (0,) * len(shp))
    return pl.pallas_call(
        body, name="branch_fwd", grid=(s // ts,),
        in_specs=[col(0), col(1), col(2), col(3), col(4), col(5), prev(0, HALO_A), prev(1, HALO_A),
                  prev(3, HALO_B), prev(4, HALO_B), full((mlen, c2)), full(conv_a_w.shape), full((1, c)),
                  full((1, c)), full((1, c)), full(conv_b_w.shape)],
        out_specs=[pl.BlockSpec((ts, c), lambda i: (i, 0)), pl.BlockSpec((N_BRANCH, ts, c), lambda i: (0, i, 0))],
        out_shape=[jax.ShapeDtypeStruct((s, c), F32), jax.ShapeDtypeStruct((N_BRANCH, s, c), BF16)],
        scratch_shapes=[pltpu.VMEM((HALO_A + ts, c), F32), pltpu.VMEM((HALO_B + ts, c), F32),
                        pltpu.VMEM((SUBLANES - 1, ts + HALO_A - SUBLANES, c), F32),
                        pltpu.VMEM((CONV_A_K, SUBLANES, c), F32)],
        compiler_params=_params(),
    )(proj, proj, proj, proj, proj, proj, proj, proj, proj, proj, kv, conv_a_w, conv_a_b, ln_g, ln_b, conv_b_w)


def _branch_bwd(proj, a1, dabo, kv, conv_a_w, ln_g, ln_b, conv_b_w, dproj):
    s = proj.shape[0]
    mlen, c2 = kv.shape
    c = c2 // 2
    hd = c // N_HEADS
    ts = _tile(s, TILE_ROWS_BRANCH_BWD)
    nt = s // ts
    scale = hd ** -0.5
    col, prev = _branch_specs(ts, c)

    def nxt(h, lead=None, ci=0):
        if lead is None:
            return pl.BlockSpec((h, c), lambda i: (jnp.minimum((i + 1) * (ts // h), s // h - 1), ci))
        return pl.BlockSpec((None, h, c), lambda i: (lead, jnp.minimum((i + 1) * (ts // h), s // h - 1), 0))

    def body(av, ag, sb, sc, sx, q, hav, hag, hsc, hsx, nsb, a1_ref, na1, dabo_ref, nda, ndb, kv_ref, caw, lng, lnb,
             cbw, dproj_in, dp_ref, dkv_ref, dwa_ref, misc_ref, exta, extda, extb, extdu, sha, shda, wtile):
        i = pl.program_id(0)
        _fill_tap_tiles(caw, wtile)
        not_first = (i > 0).astype(F32)
        not_last = (i < nt - 1).astype(F32)

        @pl.when(i == 0)
        def _():
            dkv_ref[...] = jnp.zeros_like(dkv_ref)
            dwa_ref[...] = jnp.zeros_like(dwa_ref)
            misc_ref[...] = jnp.zeros_like(misc_ref)

        def rowsum(t):
            return jnp.sum(t, axis=0, keepdims=True)

        def da1_of(a1v, da3):
            xhat, rstd, a2 = _layer_norm_parts(a1v, lng[...], lnb[...])
            sg = jax.nn.sigmoid(a2)
            da2 = da3 * (sg * (1.0 + a2 * (1.0 - sg)))
            dxh = da2 * lng[...]
            da1 = rstd * (dxh - jnp.mean(dxh, axis=-1, keepdims=True)
                          - xhat * jnp.mean(dxh * xhat, axis=-1, keepdims=True))
            return da1, da2, xhat

        da1, da2, xhat = da1_of(a1_ref[...], dabo_ref[0])
        misc_ref[pl.ds(0, 1), :] += rowsum(da1)
        misc_ref[pl.ds(1, 1), :] += rowsum(da2 * xhat)
        misc_ref[pl.ds(2, 1), :] += rowsum(da2)
        extda[pl.ds(0, ts), :] = da1
        extda[pl.ds(ts, HALO_A), :] = da1_of(na1[...], nda[...])[0] * not_last
        sga = jax.nn.sigmoid(ag[...])
        exta[pl.ds(0, HALO_A), :] = hav[...] * jax.nn.sigmoid(hag[...]) * not_first
        exta[pl.ds(HALO_A, ts), :] = av[...] * sga
        _shift_copies(exta, sha, ts + HALO_A - SUBLANES)
        _shift_copies(extda, shda, ts + HALO_A - SUBLANES)
        for r0 in range(0, ts, CONV_ROWS):
            rows = pl.ds(r0, CONV_ROWS)
            da1_rows = extda[rows, :]
            chunk = (CONV_ROWS // SUBLANES, SUBLANES, c)
            da0 = jnp.zeros(chunk, F32)
            for k in range(CONV_A_K):
                tap = da1_rows * _rows_at(exta, sha, HALO_A - (CONV_A_K - 1) + k + r0, CONV_ROWS)
                dwa_ref[pl.ds(SUBLANES * k, SUBLANES), :] += tap.reshape(chunk).sum(axis=0)
                da0 = da0 + wtile[CONV_A_K - 1 - k][None] * _rows_at(extda, shda, k + r0, CONV_ROWS).reshape(chunk)
            da0 = da0.reshape(CONV_ROWS, c)
            sg = sga[r0:r0 + CONV_ROWS]
            dp_ref[rows, 0:c] = (da0 * sg).astype(BF16)
            dp_ref[rows, c:2 * c] = (da0 * av[rows, :] * sg * (1.0 - sg)).astype(BF16)

        extb[pl.ds(0, HALO_B), :] = hsc[...] * hsx[...] * not_first
        extb[pl.ds(HALO_B, ts), :] = sc[...] * sx[...]
        dbu = dabo_ref[1]
        du = dbu * sb[...]
        extdu[pl.ds(0, ts), :] = du
        extdu[pl.ds(ts, HALO_B), :] = ndb[...] * nsb[...] * not_last
        u = jnp.zeros((ts, c), F32)
        dpr = jnp.zeros((ts, c), F32)
        for k in range(CONV_B_K):
            shifted = extb[pl.ds(HALO_B - (CONV_B_K - 1) + k, ts), :]
            u = u + cbw[pl.ds(k, 1), :] * shifted
            misc_ref[pl.ds(3 + k, 1), :] += rowsum(du * shifted)
            dpr = dpr + cbw[pl.ds(CONV_B_K - 1 - k, 1), :] * extdu[pl.ds(k, ts), :]
        dp_ref[:, 2 * c:3 * c] = (dbu * u).astype(BF16)
        dp_ref[:, 3 * c:4 * c] = (dpr * sx[...]).astype(BF16)
        dp_ref[:, 4 * c:5 * c] = (dpr * sc[...]).astype(BF16)

        nt_dims = (((1,), (1,)), ((), ()))
        tn_dims = (((0,), (0,)), ((), ()))
        for h in range(N_HEADS):
            lo, hi = h * hd, (h + 1) * hd
            qh = q[:, lo:hi].astype(BF16)
            kh = kv_ref[:, lo:hi]
            vh = kv_ref[:, c + lo:c + hi]
            p = _softmax_rows(lax.dot_general(qh, kh, nt_dims, preferred_element_type=F32) * scale)
            pb = p.astype(BF16)
            doh = dabo_ref[2, :, lo:hi].astype(BF16)
            dpm = lax.dot_general(doh, vh, nt_dims, preferred_element_type=F32)
            ds = (p * (dpm - jnp.sum(dpm * p, axis=-1, keepdims=True)) * scale).astype(BF16)
            dp_ref[:, 5 * c + lo:5 * c + hi] = jnp.dot(ds, kh, preferred_element_type=F32).astype(BF16)
            dkv_ref[:, lo:hi] += lax.dot_general(ds, qh, tn_dims, preferred_element_type=F32)
            dkv_ref[:, c + lo:c + hi] += lax.dot_general(pb, doh, tn_dims, preferred_element_type=F32)

    full = lambda shp: pl.BlockSpec(shp, lambda i: (0,) * len(shp))
    n_in_before_dproj = 21
    shifted = pltpu.VMEM((SUBLANES - 1, ts + HALO_A - SUBLANES, c), F32)
    dp, dkv, dwa, misc = pl.pallas_call(
        body, name="branch_bwd", grid=(nt,),
        in_specs=[col(0), col(1), col(2), col(3), col(4), col(5), prev(0, HALO_A), prev(1, HALO_A),
                  prev(3, HALO_B), prev(4, HALO_B), nxt(HALO_B, ci=2),
                  pl.BlockSpec((ts, c), lambda i: (i, 0)), nxt(HALO_A),
                  pl.BlockSpec((N_BRANCH, ts, c), lambda i: (0, i, 0)), nxt(HALO_A, lead=0), nxt(HALO_B, lead=1),
                  full((mlen, c2)), full(conv_a_w.shape), full((1, c)), full((1, c)), full(conv_b_w.shape),
                  pl.BlockSpec(memory_space=pl.ANY)],
        out_specs=[pl.BlockSpec((ts, 6 * c), lambda i: (i, 0)), full((mlen, c2)), full((HALO_A * SUBLANES, c)),
                   full((8, c))],
        out_shape=[jax.ShapeDtypeStruct(dproj.shape, BF16), jax.ShapeDtypeStruct((mlen, c2), F32),
                   jax.ShapeDtypeStruct((HALO_A * SUBLANES, c), F32), jax.ShapeDtypeStruct((8, c), F32)],
        scratch_shapes=[pltpu.VMEM((HALO_A + ts, c), F32), pltpu.VMEM((ts + HALO_A, c), F32),
                        pltpu.VMEM((HALO_B + ts, c), F32), pltpu.VMEM((ts + HALO_B, c), F32), shifted, shifted,
                        pltpu.VMEM((CONV_A_K, SUBLANES, c), F32)],
        input_output_aliases={n_in_before_dproj: 0},
        compiler_params=_params(),
    )(proj, proj, proj, proj, proj, proj, proj, proj, proj, proj, proj, a1, a1, dabo, dabo, dabo, kv, conv_a_w,
      ln_g, ln_b, conv_b_w, dproj)
    return dp, dkv, dwa.reshape(HALO_A, SUBLANES, c).sum(axis=1), misc


def _merge_fwd(abo, w_cat, proj, d, after):
    nb, s, c = abo.shape
    bm, bn = _tile(s, TILE_M), _tile(d, TILE_N)
    gate_col0 = (proj.shape[1] - nb * d) // bn

    def body(a_ref, w_ref, g_ref, after_ref, y_ref, m_ref, acc_ref):
        k = pl.program_id(2)
        y = jnp.dot(a_ref[...], w_ref[...], preferred_element_type=F32)
        y_ref[...] = y.astype(BF16)
        contrib = jax.nn.sigmoid(g_ref[...]) * y

        @pl.when(k == 0)
        def _():
            acc_ref[...] = contrib

        @pl.when(k > 0)
        def _():
            acc_ref[...] += contrib

        @pl.when(k == nb - 1)
        def _():
            m_ref[...] = acc_ref[...].astype(BF16)

    return pl.pallas_call(
        body, name="merge_fwd", grid=(s // bm, d // bn, nb),
        in_specs=[pl.BlockSpec((None, bm, c), lambda i, j, k: (k, i, 0)),
                  pl.BlockSpec((None, c, bn), lambda i, j, k: (k, 0, j)),
                  pl.BlockSpec((bm, bn), lambda i, j, k: (i, gate_col0 + k * (d // bn) + j)), _UNTOUCHED],
        out_specs=[pl.BlockSpec((None, bm, bn), lambda i, j, k: (k, i, j)),
                   pl.BlockSpec((bm, bn), lambda i, j, k: (i, j))],
        out_shape=[jax.ShapeDtypeStruct((nb, s, d), BF16), jax.ShapeDtypeStruct((s, d), BF16)],
        scratch_shapes=[pltpu.VMEM((bm, bn), F32)],
        compiler_params=_params(),
    )(abo, w_cat, proj, after)


def _merge_bwd(dz, w_o, y, proj):
    s, d = dz.shape
    nb = y.shape[0]
    nin = proj.shape[1]
    bm, bn = _tile(s, TILE_M), _tile(d, TILE_N)
    gate_col0 = (nin - nb * d) // bn

    def body(dz_ref, w_ref, y_ref, g_ref, dy_ref, dg_ref, acc_ref):
        @pl.when(pl.program_id(2) == 0)
        def _():
            acc_ref[...] = lax.dot_general(dz_ref[...], w_ref[...], (((1,), (1,)), ((), ())),
                                           preferred_element_type=F32)

        dm = acc_ref[...]
        gt = jax.nn.sigmoid(g_ref[...])
        dy_ref[...] = (dm * gt).astype(BF16)
        dg_ref[...] = (dm * y_ref[...].astype(F32) * gt * (1.0 - gt)).astype(BF16)

    gate = lambda i, j, k: (i, gate_col0 + k * (d // bn) + j)
    return pl.pallas_call(
        body, name="merge_bwd", grid=(s // bm, d // bn, nb),
        in_specs=[pl.BlockSpec((bm, d), lambda i, j, k: (i, 0)), pl.BlockSpec((bn, d), lambda i, j, k: (j, 0)),
                  pl.BlockSpec((None, bm, bn), lambda i, j, k: (k, i, j)), pl.BlockSpec((bm, bn), gate)],
        out_specs=[pl.BlockSpec((None, bm, bn), lambda i, j, k: (k, i, j)), pl.BlockSpec((bm, bn), gate)],
        out_shape=[jax.ShapeDtypeStruct((nb, s, d), BF16), jax.ShapeDtypeStruct((s, nin), BF16)],
        scratch_shapes=[pltpu.VMEM((bm, bn), F32)],
        compiler_params=_params(),
    )(dz, w_o, y, proj)


def _window(ref, axis, who, length):
    idx = [slice(None)] * len(ref.shape)
    idx[axis] = pl.ds(pl.multiple_of(who * length, length), length)
    return ref.at[tuple(idx)]


def _all_gather(name, shards, axes):
    n = len(shards)
    out_shapes = []
    for sh, ax in zip(shards, axes):
        shp = list(sh.shape)
        shp[ax] *= N_DEV
        out_shapes.append(jax.ShapeDtypeStruct(tuple(shp), sh.dtype))

    def body(*refs):
        ins, outs = refs[:n], refs[n:2 * n]
        send, recv, local = refs[2 * n:]
        me = _my_index()
        lens = [ins[a].shape[axes[a]] for a in range(n)]
        mine = [pltpu.make_async_copy(ins[a], _window(outs[a], axes[a], me, lens[a]), local.at[a]) for a in range(n)]
        for cp in mine:
            cp.start()
        pushes = []
        for d in range(1, N_DEV):
            to = (me + d) % N_DEV
            for a in range(n):
                cp = pltpu.make_async_remote_copy(
                    src_ref=ins[a], dst_ref=_window(outs[a], axes[a], me, lens[a]), send_sem=send.at[a, d],
                    recv_sem=recv.at[a, d], device_id=_mesh_id(to), device_id_type=pl.DeviceIdType.MESH)
                cp.start()
                pushes.append(cp)
        for d in range(1, N_DEV):
            frm = (me + N_DEV - d) % N_DEV
            for a in range(n):
                pltpu.make_async_remote_copy(
                    src_ref=ins[a], dst_ref=_window(outs[a], axes[a], frm, lens[a]), send_sem=send.at[a, d],
                    recv_sem=recv.at[a, d], device_id=_mesh_id(frm), device_id_type=pl.DeviceIdType.MESH).wait_recv()
        for cp in pushes:
            cp.wait_send()
        for cp in mine:
            cp.wait()

    hbm = pl.BlockSpec(memory_space=pl.ANY)
    return pl.pallas_call(
        body, name=name, in_specs=[hbm] * n, out_specs=[hbm] * n, out_shape=out_shapes,
        scratch_shapes=[pltpu.SemaphoreType.DMA((n, N_DEV)), pltpu.SemaphoreType.DMA((n, N_DEV)),
                        pltpu.SemaphoreType.DMA((n,))],
    )(*shards)


def _pair(a, d):
    return a * N_DEV + d


def _push_ends(kind, src, land, axis, length, me, to):
    if kind == "gather":
        return src, _window(land, axis, me, length)
    return _window(src, axis, to, length), land.at[me]


def _arrival_ends(kind, src, land, axis, length, me, frm):
    if kind == "gather":
        return src, _window(land, axis, frm, length)
    return _window(src, axis, me, length), land.at[frm]


def _exchange_start(name, kind, srcs, axes, after):
    n = len(srcs)
    if kind == "gather":
        lens = [s.shape[ax] for s, ax in zip(srcs, axes)]
        land_shapes = [s.shape[:ax] + (s.shape[ax] * N_DEV,) + s.shape[ax + 1:] for s, ax in zip(srcs, axes)]
    else:
        lens = [s.shape[ax] // N_DEV for s, ax in zip(srcs, axes)]
        land_shapes = [(N_DEV,) + s.shape[:ax] + (ln,) + s.shape[ax + 1:] for s, ax, ln in zip(srcs, axes, lens)]
    lands = [lax.empty(shp, s.dtype) for shp, s in zip(land_shapes, srcs)]

    def body(*refs):
        ins = refs[:n]
        send, recv = refs[2 * n + 1], refs[2 * n + 2]
        lnd = refs[2 * n + 3 + n:2 * n + 3 + 2 * n]
        token = refs[-1]
        me = _my_index()
        for a in range(n):
            for d in range(1, N_DEV):
                to = (me + d) % N_DEV
                src, dst = _push_ends(kind, ins[a], lnd[a], axes[a], lens[a], me, to)
                pltpu.make_async_remote_copy(src_ref=src, dst_ref=dst, send_sem=send.at[_pair(a, d)],
                                             recv_sem=recv.at[_pair(a, d)], device_id=_mesh_id(to),
                                             device_id_type=pl.DeviceIdType.MESH).start()
        for a in range(n):
            pltpu.make_async_copy(*_push_ends(kind, ins[a], lnd[a], axes[a], lens[a], me, me),
                                  send.at[_pair(a, 0)]).start()
        token[...] = jnp.zeros_like(token)

    hbm = pl.BlockSpec(memory_space=pltpu.HBM)
    sem = pl.BlockSpec(memory_space=pltpu.SEMAPHORE)
    held = [pltpu.with_memory_space_constraint(t, pltpu.HBM) for t in list(srcs) + lands]
    outs = pl.pallas_call(
        body, name=name,
        in_specs=[hbm] * (2 * n) + [_UNTOUCHED],
        out_specs=[sem, sem] + [hbm] * (2 * n) + [pl.BlockSpec(memory_space=pltpu.VMEM)],
        out_shape=[pltpu.SemaphoreType.DMA((n * N_DEV,)), pltpu.SemaphoreType.DMA((n * N_DEV,))]
        + [pltpu.HBM(t.shape, t.dtype) for t in held] + [jax.ShapeDtypeStruct((8, 128), F32)],
        input_output_aliases={i: 2 + i for i in range(2 * n)},
        compiler_params=pltpu.CompilerParams(has_side_effects=pltpu.SideEffectType.DATAFLOW_SIDE_EFFECTING),
    )(*held, after)
    handle = dict(kind=kind, axes=axes, lens=lens, send=outs[0], recv=outs[1], srcs=outs[2:2 + n],
                  lands=outs[2 + n:2 + 2 * n])
    return handle, outs[-1]


def _exchange_wait(name, handle, after):
    kind, axes, lens = handle["kind"], handle["axes"], handle["lens"]
    srcs, lands = handle["srcs"], handle["lands"]
    n = len(srcs)
    n_in = 2 * n + 2 + len(after)

    def body(*refs):
        ins = refs[:n]
        send, recv = refs[2 * n], refs[2 * n + 1]
        got = refs[n_in + n:n_in + 2 * n]
        me = _my_index()
        for d in range(1, N_DEV):
            frm = (me + N_DEV - d) % N_DEV
            for a in range(n):
                src, dst = _arrival_ends(kind, ins[a], got[a], axes[a], lens[a], me, frm)
                pltpu.make_async_remote_copy(src_ref=src, dst_ref=dst, send_sem=send.at[_pair(a, d)],
                                             recv_sem=recv.at[_pair(a, d)], device_id=_mesh_id(frm),
                                             device_id_type=pl.DeviceIdType.MESH).wait_recv()
        for d in range(1, N_DEV):
            to = (me + d) % N_DEV
            for a in range(n):
                src, dst = _push_ends(kind, ins[a], got[a], axes[a], lens[a], me, to)
                pltpu.make_async_remote_copy(src_ref=src, dst_ref=dst, send_sem=send.at[_pair(a, d)],
                                             recv_sem=recv.at[_pair(a, d)], device_id=_mesh_id(to),
                                             device_id_type=pl.DeviceIdType.MESH).wait_send()
        for a in range(n):
            pltpu.make_async_copy(*_push_ends(kind, ins[a], got[a], axes[a], lens[a], me, me),
                                  send.at[_pair(a, 0)]).wait()

    hbm = pl.BlockSpec(memory_space=pltpu.HBM)
    sem = pl.BlockSpec(memory_space=pltpu.SEMAPHORE)
    outs = pl.pallas_call(
        body, name=name,
        in_specs=[hbm] * (2 * n) + [sem, sem] + [_UNTOUCHED] * len(after),
        out_specs=[hbm] * (2 * n),
        out_shape=[pltpu.HBM(t.shape, t.dtype) for t in list(srcs) + list(lands)],
        input_output_aliases={i: i for i in range(2 * n)},
        compiler_params=pltpu.CompilerParams(has_side_effects=pltpu.SideEffectType.DATAFLOW_SIDE_EFFECTING),
    )(*srcs, *lands, handle["send"], handle["recv"], *after)
    return outs[n:]


def _all_reduce_small(packed, after):
    r, c = packed.shape

    def body(p_ref, *rest):
        o_ref, buf, send, recv = rest[len(after):]
        me = _my_index()
        buf[me] = p_ref[...]
        pushes = []
        for d in range(1, N_DEV):
            to = (me + d) % N_DEV
            cp = pltpu.make_async_remote_copy(src_ref=p_ref, dst_ref=buf.at[me], send_sem=send.at[d],
                                              recv_sem=recv.at[d], device_id=_mesh_id(to),
                                              device_id_type=pl.DeviceIdType.MESH)
            cp.start()
            pushes.append(cp)
        for d in range(1, N_DEV):
            frm = (me + N_DEV - d) % N_DEV
            pltpu.make_async_remote_copy(src_ref=p_ref, dst_ref=buf.at[frm], send_sem=send.at[d], recv_sem=recv.at[d],
                                         device_id=_mesh_id(frm), device_id_type=pl.DeviceIdType.MESH).wait_recv()
        for cp in pushes:
            cp.wait_send()
        acc = buf[0]
        for s in range(1, N_DEV):
            acc = acc + buf[s]
        o_ref[...] = acc

    vmem = pl.BlockSpec(memory_space=pltpu.VMEM)
    return pl.pallas_call(
        body, name="all_reduce_small", in_specs=[vmem] + [_UNTOUCHED] * len(after), out_specs=vmem,
        out_shape=jax.ShapeDtypeStruct((r, c), F32),
        scratch_shapes=[pltpu.VMEM((N_DEV, r, c), F32), pltpu.SemaphoreType.DMA((N_DEV,)),
                        pltpu.SemaphoreType.DMA((N_DEV,))],
        compiler_params=_params(),
    )(packed, *after)


def _layer_fwd_branches(x, h, mem, sm, wf):
    proj = _mm_nn("proj", h, wf["w_in"], F32)
    mem_n = _rmsnorm("rms_mem", mem, sm["g_mem"], mem)
    kv = _mm_nn("kv", mem_n, wf["w_kv"], BF16)
    a1, abo = _branch_fwd(proj, kv, wf["conv_a_w"], sm["conv_a_b"], sm["ln_a_g"], sm["ln_a_b"], wf["conv_b_w"])
    return dict(x=x, h=h, proj=proj, mem_n=mem_n, kv=kv, a1=a1, abo=abo)


def _layer_fwd_rest(sm, wf, sv, after, g_next):
    x = sv["x"]
    y, merged = _merge_fwd(sv["abo"], wf["w_cat"], sv["proj"], x.shape[1], after)
    z = _mm_nn("mix_out", merged, wf["w_o"], F32)
    x1, h2 = _res_norm("mix_out_norm", z, x, sm["g_mix_post"], sm["g_mlp_pre"])
    r, act = _mm_relu2("mlp_up", h2, wf["w_up"])
    f = _mm_nn("mlp_down", act, wf["w_down"], F32)
    x2, h_next = _res_norm("mlp_down_norm", f, x1, sm["g_mlp_post"], g_next)
    sv.update(y=y, merged=merged, z=z, x1=x1, h2=h2, r=r, act=act, f=f)
    return x2, h_next


def _pad_rows(t):
    return jnp.pad(t, ((0, (-t.shape[0]) % 8), (0, 0)))


def _layer_bwd(dx2, df, mem, sm, wf, sv, send_off, below):
    c = sv["a1"].shape[1]
    d_up = _mm_relu2_bwd("mlp_down_bwd", df, wf["w_down"], sv["r"])
    gw_down = _mm_tn("gw_down", sv["act"], df, BF16)
    dh2 = _mm_nt("mlp_up_bwd", d_up, wf["w_up"], F32)
    gw_up = _mm_tn("gw_up", sv["h2"], d_up, BF16)
    sent = send_off(dict(w_up=gw_up, w_down=gw_down))
    dx1, dg_mlp_pre, dz, dg_mix_post = _rms_bwd("rms_bwd_mlp_pre", dh2, sv["x1"], sm["g_mlp_pre"], F32, sent,
                                                residual=dx2, below=(sv["z"], sm["g_mix_post"]))
    dy, dproj = _merge_bwd(dz, wf["w_o"], sv["y"], sv["proj"])
    gw_o = _mm_tn("gw_o", sv["merged"], dz, BF16)
    dabo = _mm_nt("branch_out_bwd", dy, wf["w_cat"], F32)
    gw_cat = _mm_tn("gw_branch_out", sv["abo"], dy, BF16)
    dproj, dkv, dconv_a_w, misc = _branch_bwd(sv["proj"], sv["a1"], dabo, sv["kv"], wf["conv_a_w"], sm["ln_a_g"],
                                              sm["ln_a_b"], wf["conv_b_w"], dproj)
    dkv = dkv.astype(BF16)
    gw_kv = _mm_tn("gw_kv", sv["mem_n"], dkv, BF16)
    dmem_n = _mm_nt("kv_bwd", dkv, wf["w_kv"], F32)
    _, dg_mem = _rms_bwd("rms_bwd_mem", dmem_n, mem, sm["g_mem"], BF16, dmem_n)
    gw_in = _mm_tn("gw_in", sv["h"], dproj, BF16)
    sent = send_off(dict(w_in=gw_in, w_kv=gw_kv, w_cat=gw_cat, w_o=gw_o))
    dh = _mm_nt("proj_bwd", dproj, wf["w_in"], F32, after=sent)
    dx, dg_mix_pre, *rest = _rms_bwd("rms_bwd_mix_pre", dh, sv["x"], sm["g_mix_pre"], F32, dh, residual=dx1,
                                     below=below)
    small = dict(g_mix_pre=dg_mix_pre, g_mem=dg_mem, g_mix_post=dg_mix_post, g_mlp_pre=dg_mlp_pre, misc=misc,
                 conv_a_w=dconv_a_w)
    return dx, small, rest


def _pack_small(small, c):
    rows = [_pad_rows(small[k].reshape(2, c)) for k in _SMALL_D_NAMES]
    return jnp.concatenate(rows + [small["misc"], small["conv_a_w"]], axis=0)


_SMALL_D_NAMES = ["g_mix_pre", "g_mem", "g_mix_post", "g_mlp_pre", "g_mlp_post"]
_SMALL_C_NAMES = ["conv_a_b", "ln_a_g", "ln_a_b"]
_SMALL_MISC_ROW = 8 * len(_SMALL_D_NAMES)
_SMALL_CONV_B_ROW = _SMALL_MISC_ROW + len(_SMALL_C_NAMES)
_SMALL_CONV_A_ROW = _SMALL_MISC_ROW + 8
_SMALL_ROWS = _SMALL_CONV_A_ROW + HALO_A


def kernel(x, mem, g_mix_pre, w_in, conv_a_w, conv_a_b, ln_a_g, ln_a_b, w_a_out, conv_b_w, w_b_out, g_mem, w_kv, w_x_out, w_o, g_mix_post, g_mlp_pre, w_up, w_down, g_mlp_post, loss_target, m_g_mix_pre, m_w_in, m_conv_a_w, m_conv_a_b, m_ln_a_g, m_ln_a_b, m_w_a_out, m_conv_b_w, m_w_b_out, m_g_mem, m_w_kv, m_w_x_out, m_w_o, m_g_mix_post, m_g_mlp_pre, m_w_up, m_w_down, m_g_mlp_post, v_g_mix_pre, v_w_in, v_conv_a_w, v_conv_a_b, v_ln_a_g, v_ln_a_b, v_w_a_out, v_conv_b_w, v_w_b_out, v_g_mem, v_w_kv, v_w_x_out, v_w_o, v_g_mix_post, v_g_mlp_pre, v_w_up, v_w_down, v_g_mlp_post):
    names = ["g_mix_pre", "w_in", "conv_a_w", "conv_a_b", "ln_a_g", "ln_a_b", "w_a_out", "conv_b_w", "w_b_out",
             "g_mem", "w_kv", "w_x_out", "w_o", "g_mix_post", "g_mlp_pre", "w_up", "w_down", "g_mlp_post"]
    w = dict(zip(names, [g_mix_pre, w_in, conv_a_w, conv_a_b, ln_a_g, ln_a_b, w_a_out, conv_b_w, w_b_out, g_mem,
                         w_kv, w_x_out, w_o, g_mix_post, g_mlp_pre, w_up, w_down, g_mlp_post]))
    mo = dict(zip(names, [m_g_mix_pre, m_w_in, m_conv_a_w, m_conv_a_b, m_ln_a_g, m_ln_a_b, m_w_a_out, m_conv_b_w,
                          m_w_b_out, m_g_mem, m_w_kv, m_w_x_out, m_w_o, m_g_mix_post, m_g_mlp_pre, m_w_up, m_w_down,
                          m_g_mlp_post]))
    vo = dict(zip(names, [v_g_mix_pre, v_w_in, v_conv_a_w, v_conv_a_b, v_ln_a_g, v_ln_a_b, v_w_a_out, v_conv_b_w,
                          v_w_b_out, v_g_mem, v_w_kv, v_w_x_out, v_w_o, v_g_mix_post, v_g_mlp_pre, v_w_up, v_w_down,
                          v_g_mlp_post]))
    depth = w_in.shape[0]
    c = conv_a_b.shape[1]
    cs = conv_a_w.shape[2]
    me = _my_index()
    xs, mems, tgt = x[0], mem[0], loss_target[0]

    big_names = ["w_in", "w_kv", "w_cat", "w_o", "w_up", "w_down"]
    big_axes = [1, 0, 2, 0, 1, 0]
    n_early = 2
    branch_names = ["w_a_out", "w_b_out", "w_x_out"]
    smalls = [{k: w[k][l][None, :] for k in _SMALL_D_NAMES + _SMALL_C_NAMES} for l in range(depth)]

    def weight_shards(l):
        return [w_in[l].astype(BF16), w_kv[l].astype(BF16), jnp.stack([w[k][l] for k in branch_names]).astype(BF16),
                w_o[l].astype(BF16), w_up[l].astype(BF16), w_down[l].astype(BF16)]

    def start_gather(l, after):
        shards = weight_shards(l)
        early, token = _exchange_start(f"gather_start_{l}_early", "gather", shards[:n_early], big_axes[:n_early],
                                       after)
        late, token = _exchange_start(f"gather_start_{l}_late", "gather", shards[n_early:], big_axes[n_early:], token)
        return early, late, token

    taps_a, taps_b = _all_gather(
        "gather_conv_taps",
        [jnp.pad(conv_a_w, ((0, 0), (0, HALO_A - CONV_A_K), (0, 0))),
         jnp.pad(conv_b_w, ((0, 0), (0, HALO_B - CONV_B_K), (0, 0)))], [2, 2])

    early, late, token = start_gather(0, taps_a)
    fulls = [None] * depth
    saved = []
    xc = xs
    h = _rmsnorm("rms_mix_pre", xs, smalls[0]["g_mix_pre"], token)
    for l in range(depth):
        fulls[l] = dict(zip(big_names[:n_early], _exchange_wait(f"gather_wait_{l}_early", early, [xc])))
        fulls[l]["conv_a_w"], fulls[l]["conv_b_w"] = taps_a[l], taps_b[l]
        sv = _layer_fwd_branches(xc, h, mems, smalls[l], fulls[l])
        fulls[l].update(zip(big_names[n_early:], _exchange_wait(f"gather_wait_{l}_late", late, [sv["abo"]])))
        token = fulls[l]["w_o"]
        if l + 1 < depth:
            early, late, token = start_gather(l + 1, token)
        xc, h = _layer_fwd_rest(smalls[l], fulls[l], sv, token,
                                smalls[l + 1]["g_mix_pre"] if l + 1 < depth else None)
        saved.append(sv)
    loss_part, dx = _loss_head(xc, tgt)
    loss = lax.psum(loss_part, MESH_AXES)

    upd = {}

    def adamw_group(l, group, slots, after):
        for k, sl in zip(group, slots):
            if k == "w_cat":
                for b, nm in enumerate(branch_names):
                    upd[nm] = _adamw("adamw_" + nm, sl, b, w[nm], mo[nm], vo[nm], l, upd.get(nm), after)
            else:
                upd[k] = _adamw("adamw_" + k, sl, None, w[k], mo[k], vo[k], l, upd.get(k), after)

    in_flight = [None]

    def take_in(after):
        handle, l, group = in_flight[0]
        return _exchange_wait(f"scatter_wait_{l}_{group[0]}", handle, after), l, group

    def send_off_layer(l):
        def send_off(grads):
            group = [k for k in big_names if k in grads]
            arrays = [grads[k] for k in group]
            arrived = None if in_flight[0] is None else take_in([arrays[0]])
            handle, token = _exchange_start(f"scatter_start_{l}_{group[0]}", "scatter", arrays,
                                            [big_axes[big_names.index(k)] for k in group],
                                            arrays[0] if arrived is None else arrived[0][0])
            in_flight[0] = (handle, l, group)
            if arrived is not None:
                adamw_group(arrived[1], arrived[2], arrived[0], token)
            return token
        return send_off

    small_g = [dict() for _ in range(depth)]
    df, small_g[depth - 1]["g_mlp_post"] = _rms_bwd("rms_bwd_top", dx, saved[depth - 1]["f"],
                                                    smalls[depth - 1]["g_mlp_post"], BF16, dx)
    for l in reversed(range(depth)):
        below = (saved[l - 1]["f"], smalls[l - 1]["g_mlp_post"]) if l > 0 else None
        dx, small, rest = _layer_bwd(dx, df, mems, smalls[l], fulls[l], saved[l], send_off_layer(l), below)
        small_g[l].update(small)
        if l > 0:
            df, small_g[l - 1]["g_mlp_post"] = rest
    small_parts = [_pack_small(small_g[l], c) for l in range(depth)]

    slots, l, group = take_in([u[0] for u in upd.values()] + [dx])
    adamw_group(l, group, slots, dx)

    tot = _all_reduce_small(jnp.concatenate(small_parts, axis=0), [upd[k][0] for k in group if k in upd])
    tot = tot.reshape(depth, _SMALL_ROWS, c)
    g_small = {}
    for n_, nm in enumerate(_SMALL_D_NAMES):
        g_small[nm] = tot[:, 8 * n_:8 * n_ + 2, :].reshape(depth, 2 * c)
    for n_, nm in enumerate(_SMALL_C_NAMES):
        g_small[nm] = tot[:, _SMALL_MISC_ROW + n_, :]
    g_taps = {"conv_a_w": tot[:, _SMALL_CONV_A_ROW:_SMALL_CONV_A_ROW + CONV_A_K, :],
              "conv_b_w": tot[:, _SMALL_CONV_B_ROW:_SMALL_CONV_B_ROW + CONV_B_K, :]}

    def pack(t):
        return jnp.concatenate([_pad_rows(t[nm].reshape(-1, c)) for nm in _SMALL_D_NAMES + _SMALL_C_NAMES], axis=0)

    res = _adamw("adamw_small", pack(g_small)[None], None, pack(w)[None], pack(mo)[None], pack(vo)[None], 0, None, tot)
    row = 0
    for nm in _SMALL_D_NAMES + _SMALL_C_NAMES:
        n_rows = w[nm].size // c
        upd[nm] = [t[0, row:row + n_rows].reshape(w[nm].shape) for t in res]
        row += n_rows + (-n_rows) % 8
    for nm, g in g_taps.items():
        shp = w[nm].shape
        mine = lax.dynamic_slice_in_dim(g, me * cs, cs, axis=2).reshape(1, -1, cs)
        res = _adamw("adamw_" + nm, mine, None, w[nm].reshape(1, -1, cs), mo[nm].reshape(1, -1, cs),
                     vo[nm].reshape(1, -1, cs), 0, None, tot)
        upd[nm] = [t.reshape(shp) for t in res]

    return (loss, dx[None], *[upd[nm][0] for nm in names], *[upd[nm][1] for nm in names],
            *[upd[nm][2] for nm in names], *[upd[nm][3] for nm in names])
```

```python
import math

import jax
import jax.numpy as jnp
from jax import lax
from jax.experimental import pallas as pl
from jax.experimental.pallas import tpu as pltpu

F32 = jnp.float32
BF16 = jnp.bfloat16

EPS = 1e-6
N_HEADS = 4
N_BRANCH = 3
CONV_A_K = 31
CONV_B_K = 3
HALO_A = 32
HALO_B = 8
N_DEV = 8
MESH_AXES = ("x", "y", "c")

ADAM_LR = 0.001
ADAM_B1 = 0.9
ADAM_B2 = 0.999
ADAM_EPS = 1e-08
ADAM_WD = 0.01
ADAM_STEP = 10

V7X_VMEM_BYTES = 64 * 1024 * 1024
VMEM_LIMIT = V7X_VMEM_BYTES - 8 * 1024 * 1024

TILE_M = 1024
TILE_N = 1024
TILE_K = 4096
TILE_ROWS_WIDE = 512
TILE_ROWS_BRANCH = 256
TILE_ROWS_BRANCH_BWD = 128


def _params():
    return pltpu.CompilerParams(vmem_limit_bytes=VMEM_LIMIT)


def _tile(n, t):
    t = min(n, t)
    assert n % t == 0, (n, t)
    return t


def _my_index():
    return 4 * lax.axis_index("x") + 2 * lax.axis_index("y") + lax.axis_index("c")


def _mesh_id(p):
    return (p // 4, (p // 2) % 2, p % 2)


def _mm(name, grid, a, a_spec, b, b_spec, *, ta, tb, acc_shape, extras, outs, epilogue):
    nk = grid[-1]
    lead = len(grid) - 3
    ne, no = len(extras), len(outs)
    dn = (((0,) if ta else (1,), (1,) if tb else (0,)), ((), ()))

    def body(a_ref, b_ref, *rest):
        extra_refs = rest[:ne]
        out_refs = rest[ne:ne + no]
        i, j, k = pl.program_id(lead), pl.program_id(lead + 1), pl.program_id(lead + 2)
        prod = lax.dot_general(a_ref[...], b_ref[...], dn, preferred_element_type=F32)
        if nk == 1:
            epilogue(prod, extra_refs, out_refs, i, j)
        else:
            acc_ref = rest[ne + no]

            @pl.when(k == 0)
            def _():
                acc_ref[...] = prod

            @pl.when(k > 0)
            def _():
                acc_ref[...] += prod

            @pl.when(k == nk - 1)
            def _():
                epilogue(acc_ref[...], extra_refs, out_refs, i, j)

    return pl.pallas_call(
        body, name=name, grid=grid,
        in_specs=[a_spec, b_spec] + [s for _, s in extras],
        out_specs=[s for _, s in outs],
        out_shape=[o for o, _ in outs],
        scratch_shapes=[pltpu.VMEM(acc_shape, F32)] if nk > 1 else [],
        compiler_params=_params(),
    )(a, b, *[e for e, _ in extras])


def _store_epilogue(res, extra_refs, out_refs, i, j):
    out_refs[0][...] = res.astype(out_refs[0].dtype)


def _mm_nn(name, a, b, out_dtype):
    m, kd = a.shape
    n = b.shape[1]
    bm, bn, bk = _tile(m, TILE_M), _tile(n, TILE_N), _tile(kd, TILE_K)
    return _mm(name, (m // bm, n // bn, kd // bk), a, pl.BlockSpec((bm, bk), lambda i, j, k: (i, k)),
               b, pl.BlockSpec((bk, bn), lambda i, j, k: (k, j)), ta=False, tb=False, acc_shape=(bm, bn), extras=[],
               outs=[(jax.ShapeDtypeStruct((m, n), out_dtype), pl.BlockSpec((bm, bn), lambda i, j, k: (i, j)))],
               epilogue=_store_epilogue)[0]


def _mm_nt(name, a, b, out_dtype, after=None):
    m, kd = a.shape[-2:]
    n = b.shape[-2]
    bm, bn = _tile(m, TILE_M), _tile(n, TILE_N)
    if a.ndim == 2:
        bk = _tile(kd, TILE_K)
        return _mm(name, (m // bm, n // bn, kd // bk), a, pl.BlockSpec((bm, bk), lambda i, j, k: (i, k)),
                   b, pl.BlockSpec((bn, bk), lambda i, j, k: (j, k)), ta=False, tb=True, acc_shape=(bm, bn),
                   extras=[] if after is None else [(after, _UNTOUCHED)],
                   outs=[(jax.ShapeDtypeStruct((m, n), out_dtype), pl.BlockSpec((bm, bn), lambda i, j, k: (i, j)))],
                   epilogue=_store_epilogue)[0]
    nb = a.shape[0]
    return _mm(name, (nb, m // bm, n // bn, 1), a, pl.BlockSpec((None, bm, kd), lambda s, i, j, k: (s, i, 0)),
               b, pl.BlockSpec((None, bn, kd), lambda s, i, j, k: (s, j, 0)), ta=False, tb=True, acc_shape=(bm, bn),
               extras=[],
               outs=[(jax.ShapeDtypeStruct((nb, m, n), out_dtype),
                      pl.BlockSpec((None, bm, bn), lambda s, i, j, k: (s, i, j)))],
               epilogue=_store_epilogue)[0]


def _mm_tn(name, a, b, out_dtype):
    r, m = a.shape[-2:]
    n = b.shape[-1]
    bm, bn, bk = _tile(m, TILE_M), _tile(n, TILE_N), _tile(r, TILE_K)
    if a.ndim == 2:
        return _mm(name, (m // bm, n // bn, r // bk), a, pl.BlockSpec((bk, bm), lambda i, j, k: (k, i)),
                   b, pl.BlockSpec((bk, bn), lambda i, j, k: (k, j)), ta=True, tb=False, acc_shape=(bm, bn), extras=[],
                   outs=[(jax.ShapeDtypeStruct((m, n), out_dtype), pl.BlockSpec((bm, bn), lambda i, j, k: (i, j)))],
                   epilogue=_store_epilogue)[0]
    nb = a.shape[0]
    return _mm(name, (nb, m // bm, n // bn, r // bk), a, pl.BlockSpec((None, bk, bm), lambda s, i, j, k: (s, k, i)),
               b, pl.BlockSpec((None, bk, bn), lambda s, i, j, k: (s, k, j)), ta=True, tb=False, acc_shape=(bm, bn),
               extras=[],
               outs=[(jax.ShapeDtypeStruct((nb, m, n), out_dtype),
                      pl.BlockSpec((None, bm, bn), lambda s, i, j, k: (s, i, j)))],
               epilogue=_store_epilogue)[0]


def _mm_relu2(name, h, w):
    m, kd = h.shape
    n = w.shape[1]
    bm, bn = _tile(m, TILE_M), _tile(n, TILE_N)

    def epilogue(res, extra_refs, out_refs, i, j):
        r = jnp.maximum(res, 0.0)
        out_refs[0][...] = r.astype(BF16)
        out_refs[1][...] = (r * r).astype(BF16)

    o = jax.ShapeDtypeStruct((m, n), BF16)
    spec = pl.BlockSpec((bm, bn), lambda i, j, k: (i, j))
    return _mm(name, (m // bm, n // bn, 1), h, pl.BlockSpec((bm, kd), lambda i, j, k: (i, 0)),
               w, pl.BlockSpec((kd, bn), lambda i, j, k: (0, j)), ta=False, tb=False, acc_shape=(bm, bn), extras=[],
               outs=[(o, spec), (o, spec)], epilogue=epilogue)


def _mm_relu2_bwd(name, df, w_down, r):
    m, kd = df.shape
    n = w_down.shape[0]
    bm, bn = _tile(m, TILE_M), _tile(n, TILE_N)

    def epilogue(res, extra_refs, out_refs, i, j):
        out_refs[0][...] = (res * (2.0 * extra_refs[0][...].astype(F32))).astype(BF16)

    spec = pl.BlockSpec((bm, bn), lambda i, j, k: (i, j))
    return _mm(name, (m // bm, n // bn, 1), df, pl.BlockSpec((bm, kd), lambda i, j, k: (i, 0)),
               w_down, pl.BlockSpec((bn, kd), lambda i, j, k: (j, 0)), ta=False, tb=True, acc_shape=(bm, bn),
               extras=[(r, spec)], outs=[(jax.ShapeDtypeStruct((m, n), BF16), spec)], epilogue=epilogue)[0]


def _rms_bwd_math(dy, x, g):
    rr = lax.rsqrt(jnp.mean(x * x, axis=-1, keepdims=True) + EPS)
    gy = dy * g
    dx = rr * gy - x * (rr * rr * rr * jnp.mean(x * gy, axis=-1, keepdims=True))
    dg_rows = dy * x * rr
    return dx, dg_rows


_UNTOUCHED = pl.BlockSpec(memory_space=pl.ANY)


def _rmsnorm(name, x, g, after):
    r, d = x.shape
    br = _tile(r, TILE_ROWS_WIDE)

    def body(x_ref, g_ref, after_ref, o_ref):
        xv = x_ref[...]
        rr = lax.rsqrt(jnp.mean(xv * xv, axis=-1, keepdims=True) + EPS)
        o_ref[...] = ((xv * rr) * g_ref[...]).astype(BF16)

    row = pl.BlockSpec((br, d), lambda i: (i, 0))
    return pl.pallas_call(body, name=name, grid=(r // br,),
                          in_specs=[row, pl.BlockSpec((1, d), lambda i: (0, 0)), _UNTOUCHED],
                          out_specs=row, out_shape=jax.ShapeDtypeStruct((r, d), BF16),
                          compiler_params=_params())(x, g, after)


def _res_norm(name, z, x, g, g_next=None):
    r, d = x.shape
    br = _tile(r, TILE_ROWS_WIDE)
    n_next = 0 if g_next is None else 1

    def body(z_ref, x_ref, g_ref, *rest):
        zv = z_ref[...]
        rr = lax.rsqrt(jnp.mean(zv * zv, axis=-1, keepdims=True) + EPS)
        xn = x_ref[...] + (zv * rr) * g_ref[...]
        rest[n_next][...] = xn
        if n_next:
            r2 = lax.rsqrt(jnp.mean(xn * xn, axis=-1, keepdims=True) + EPS)
            rest[2][...] = ((xn * r2) * rest[0][...]).astype(BF16)

    row = pl.BlockSpec((br, d), lambda i: (i, 0))
    vec = pl.BlockSpec((1, d), lambda i: (0, 0))
    outs = pl.pallas_call(body, name=name, grid=(r // br,), in_specs=[row, row, vec] + [vec] * n_next,
                          out_specs=[row] * (1 + n_next),
                          out_shape=[jax.ShapeDtypeStruct((r, d), F32)] + [jax.ShapeDtypeStruct((r, d), BF16)] * n_next,
                          compiler_params=_params())(z, x, g, *([] if g_next is None else [g_next]))
    return outs if n_next else (outs[0], None)


def _rms_bwd(name, dy, x, g, out_dtype, after, residual=None, below=None):
    r, d = x.shape
    br = _tile(r, TILE_ROWS_WIDE if below is None else TILE_ROWS_WIDE // 2)
    n_res = 0 if residual is None else 1
    n_below = 0 if below is None else 1

    def body(dy_ref, x_ref, g_ref, after_ref, *rest):
        ins, outs = rest[:n_res + 2 * n_below], rest[n_res + 2 * n_below:]
        first = pl.program_id(0) == 0
        dx, dg_rows = _rms_bwd_math(dy_ref[...], x_ref[...], g_ref[...])
        if n_res:
            dx = dx + ins[0][...]
        outs[0][...] = dx.astype(out_dtype)

        @pl.when(first)
        def _():
            outs[1][...] = jnp.zeros_like(outs[1])

        outs[1][...] += jnp.sum(dg_rows, axis=0, keepdims=True)
        if n_below:
            dz, dgz_rows = _rms_bwd_math(dx, ins[n_res][...], ins[n_res + 1][...])
            outs[2][...] = dz.astype(BF16)

            @pl.when(first)
            def _():
                outs[3][...] = jnp.zeros_like(outs[3])

            outs[3][...] += jnp.sum(dgz_rows, axis=0, keepdims=True)

    row = pl.BlockSpec((br, d), lambda i: (i, 0))
    vec = pl.BlockSpec((1, d), lambda i: (0, 0))
    o_row, o_vec = jax.ShapeDtypeStruct((r, d), out_dtype), jax.ShapeDtypeStruct((1, d), F32)
    return pl.pallas_call(
        body, name=name, grid=(r // br,),
        in_specs=[row, row, vec, _UNTOUCHED] + [row] * n_res + [row, vec] * n_below,
        out_specs=[row, vec] + [row, vec] * n_below,
        out_shape=[o_row, o_vec] + [jax.ShapeDtypeStruct((r, d), BF16), o_vec] * n_below,
        compiler_params=_params(),
    )(dy, x, g, after, *([] if residual is None else [residual]), *([] if below is None else below))


def _loss_head(y, target):
    r, d = y.shape
    br = _tile(r, TILE_ROWS_WIDE)

    def body(y_ref, t_ref, dy_ref, l_ref):
        diff = y_ref[...] - t_ref[...]
        dy_ref[...] = diff * (1.0 / d)

        @pl.when(pl.program_id(0) == 0)
        def _():
            l_ref[...] = jnp.zeros_like(l_ref)

        l_ref[...] += 0.5 * jnp.sum(jnp.mean(diff * diff, axis=-1, keepdims=True))

    row = pl.BlockSpec((br, d), lambda i: (i, 0))
    one = pl.BlockSpec((8, 128), lambda i: (0, 0))
    dy, l = pl.pallas_call(body, name="loss_head", grid=(r // br,), in_specs=[row, row], out_specs=[row, one],
                           out_shape=[jax.ShapeDtypeStruct((r, d), F32), jax.ShapeDtypeStruct((8, 128), F32)],
                           compiler_params=_params())(y, target)
    return l[0, 0], dy


def _adamw(name, parts, branch, w, m, v, layer, prev, after):
    _, r, c = w.shape
    p = parts.shape[0]
    br = r
    for cand in (512, 256, 128, 64, 32, 16):
        if r % cand == 0 and cand * c * 4 <= 2 * 1024 * 1024:
            br = cand
            break
    bc1 = 1.0 - ADAM_B1 ** ADAM_STEP
    bc2 = 1.0 - ADAM_B2 ** ADAM_STEP
    if prev is None:
        prev = [lax.empty(w.shape, F32) for _ in range(4)]

    def body(p_ref, w_ref, m_ref, v_ref, after_ref, pg, pd, pm, pv, g_out, d_out, m_out, v_out):
        g = p_ref[0].astype(F32)
        for s in range(1, p):
            g = g + p_ref[s].astype(F32)
        m2 = ADAM_B1 * m_ref[...] + (1.0 - ADAM_B1) * g
        v2 = ADAM_B2 * v_ref[...] + (1.0 - ADAM_B2) * (g * g)
        m_hat = m2 / bc1
        v_hat = v2 / bc2
        g_out[...] = g
        d_out[...] = -ADAM_LR * (m_hat / (jnp.sqrt(v_hat) + ADAM_EPS) + ADAM_WD * w_ref[...])
        m_out[...] = m2
        v_out[...] = v2

    if branch is None:
        p_spec = pl.BlockSpec((p, br, c), lambda i: (0, i, 0))
    else:
        p_spec = pl.BlockSpec((p, None, br, c), lambda i: (0, branch, i, 0))
    slab = pl.BlockSpec((None, br, c), lambda i: (layer, i, 0))
    o = jax.ShapeDtypeStruct(w.shape, F32)
    return pl.pallas_call(body, name=name, grid=(r // br,),
                          in_specs=[p_spec, slab, slab, slab, _UNTOUCHED] + [_UNTOUCHED] * 4,
                          out_specs=[slab] * 4, out_shape=[o] * 4, input_output_aliases={5: 0, 6: 1, 7: 2, 8: 3},
                          compiler_params=_params())(parts, w, m, v, after, *prev)


def _layer_norm_parts(a1, ln_g, ln_b):
    mu = jnp.mean(a1, axis=-1, keepdims=True)
    xc = a1 - mu
    rstd = lax.rsqrt(jnp.mean(xc * xc, axis=-1, keepdims=True) + EPS)
    xhat = xc * rstd
    a2 = xhat * ln_g + ln_b
    return xhat, rstd, a2


def _softmax_rows(s):
    e = jnp.exp(s - jnp.max(s, axis=-1, keepdims=True))
    return e / jnp.sum(e, axis=-1, keepdims=True)


SUBLANES = 8
CONV_ROWS = 32


def _shift_copies(ext_ref, sh_ref, rows):
    for r in range(1, SUBLANES):
        sh_ref[r - 1, pl.ds(0, rows), :] = ext_ref[pl.ds(r, rows), :]


def _fill_tap_tiles(taps_ref, tile_ref):
    @pl.when(pl.program_id(0) == 0)
    def _():
        for k in range(CONV_A_K):
            tile_ref[k] = jnp.broadcast_to(taps_ref[pl.ds(k, 1), :], tile_ref.shape[1:])


def _rows_at(ext_ref, sh_ref, offset, ts):
    q, r = divmod(offset, SUBLANES)
    if r == 0:
        return ext_ref[pl.ds(SUBLANES * q, ts), :]
    return sh_ref[r - 1, pl.ds(SUBLANES * q, ts), :]


def _branch_specs(ts, c):
    def col(ci):
        return pl.BlockSpec((ts, c), lambda i: (i, ci))

    def prev(ci, h):
        return pl.BlockSpec((h, c), lambda i: (jnp.maximum(i * (ts // h) - 1, 0), ci))

    return col, prev


def _branch_fwd(proj, kv, conv_a_w, conv_a_b, ln_g, ln_b, conv_b_w):
    s = proj.shape[0]
    mlen, c2 = kv.shape
    c = c2 // 2
    hd = c // N_HEADS
    ts = _tile(s, TILE_ROWS_BRANCH)
    scale = hd ** -0.5
    col, prev = _branch_specs(ts, c)

    def body(av, ag, sb, sc, sx, q, hav, hag, hsc, hsx, kv_ref, caw, cab, lng, lnb, cbw, a1_ref, abo_ref, exta, extb,
             sha, wtile):
        not_first = (pl.program_id(0) > 0).astype(F32)
        _fill_tap_tiles(caw, wtile)
        exta[pl.ds(0, HALO_A), :] = hav[...] * jax.nn.sigmoid(hag[...]) * not_first
        exta[pl.ds(HALO_A, ts), :] = av[...] * jax.nn.sigmoid(ag[...])
        _shift_copies(exta, sha, ts + HALO_A - SUBLANES)
        chunk = (CONV_ROWS // SUBLANES, SUBLANES, c)
        for r0 in range(0, ts, CONV_ROWS):
            acc = jnp.broadcast_to(cab[...], chunk)
            for k in range(CONV_A_K):
                rows = _rows_at(exta, sha, HALO_A - (CONV_A_K - 1) + k + r0, CONV_ROWS)
                acc = acc + wtile[k][None] * rows.reshape(chunk)
            a1_ref[pl.ds(r0, CONV_ROWS), :] = acc.reshape(CONV_ROWS, c)
        _, _, a2 = _layer_norm_parts(a1_ref[...], lng[...], lnb[...])
        abo_ref[0] = (a2 * jax.nn.sigmoid(a2)).astype(BF16)
        extb[pl.ds(0, HALO_B), :] = hsc[...] * hsx[...] * not_first
        extb[pl.ds(HALO_B, ts), :] = sc[...] * sx[...]
        u = cbw[pl.ds(0, 1), :] * extb[pl.ds(HALO_B - (CONV_B_K - 1), ts), :]
        for k in range(1, CONV_B_K):
            u = u + cbw[pl.ds(k, 1), :] * extb[pl.ds(HALO_B - (CONV_B_K - 1) + k, ts), :]
        abo_ref[1] = (sb[...] * u).astype(BF16)
        for h in range(N_HEADS):
            qh = q[:, h * hd:(h + 1) * hd].astype(BF16)
            kh = kv_ref[:, h * hd:(h + 1) * hd]
            vh = kv_ref[:, c + h * hd:c + (h + 1) * hd]
            sc_ = lax.dot_general(qh, kh, (((1,), (1,)), ((), ())), preferred_element_type=F32) * scale
            p = _softmax_rows(sc_).astype(BF16)
            abo_ref[2, :, h * hd:(h + 1) * hd] = jnp.dot(p, vh, preferred_element_type=F32).astype(BF16)

    full = lambda shp: pl.BlockSpec(shp, lambda i: (0,) * len(shp))
    return pl.pallas_call(
        body, name="branch_fwd", grid=(s // ts,),
        in_specs=[col(0), col(1), col(2), col(3), col(4), col(5), prev(0, HALO_A), prev(1, HALO_A),
                  prev(3, HALO_B), prev(4, HALO_B), full((mlen, c2)), full(conv_a_w.shape), full((1, c)),
                  full((1, c)), full((1, c)), full(conv_b_w.shape)],
        out_specs=[pl.BlockSpec((ts, c), lambda i: (i, 0)), pl.BlockSpec((N_BRANCH, ts, c), lambda i: (0, i, 0))],
        out_shape=[jax.ShapeDtypeStruct((s, c), F32), jax.ShapeDtypeStruct((N_BRANCH, s, c), BF16)],
        scratch_shapes=[pltpu.VMEM((HALO_A + ts, c), F32), pltpu.VMEM((HALO_B + ts, c), F32),
                        pltpu.VMEM((SUBLANES - 1, ts + HALO_A - SUBLANES, c), F32),
                        pltpu.VMEM((CONV_A_K, SUBLANES, c), F32)],
        compiler_params=_params(),
    )(proj, proj, proj, proj, proj, proj, proj, proj, proj, proj, kv, conv_a_w, conv_a_b, ln_g, ln_b, conv_b_w)


def _branch_bwd(proj, a1, dabo, kv, conv_a_w, ln_g, ln_b, conv_b_w, dproj):
    s = proj.shape[0]
    mlen, c2 = kv.shape
    c = c2 // 2
    hd = c // N_HEADS
    ts = _tile(s, TILE_ROWS_BRANCH_BWD)
    nt = s // ts
    scale = hd ** -0.5
    col, prev = _branch_specs(ts, c)

    def nxt(h, lead=None, ci=0):
        if lead is None:
            return pl.BlockSpec((h, c), lambda i: (jnp.minimum((i + 1) * (ts // h), s // h - 1), ci))
        return pl.BlockSpec((None, h, c), lambda i: (lead, jnp.minimum((i + 1) * (ts // h), s // h - 1), 0))

    def body(av, ag, sb, sc, sx, q, hav, hag, hsc, hsx, nsb, a1_ref, na1, dabo_ref, nda, ndb, kv_ref, caw, lng, lnb,
             cbw, dproj_in, dp_ref, dkv_ref, dwa_ref, misc_ref, exta, extda, extb, extdu, sha, shda, wtile):
        i = pl.program_id(0)
        _fill_tap_tiles(caw, wtile)
        not_first = (i > 0).astype(F32)
        not_last = (i < nt - 1).astype(F32)

        @pl.when(i == 0)
        def _():
            dkv_ref[...] = jnp.zeros_like(dkv_ref)
            dwa_ref[...] = jnp.zeros_like(dwa_ref)
            misc_ref[...] = jnp.zeros_like(misc_ref)

        def rowsum(t):
            return jnp.sum(t, axis=0, keepdims=True)

        def da1_of(a1v, da3):
            xhat, rstd, a2 = _layer_norm_parts(a1v, lng[...], lnb[...])
            sg = jax.nn.sigmoid(a2)
            da2 = da3 * (sg * (1.0 + a2 * (1.0 - sg)))
            dxh = da2 * lng[...]
            da1 = rstd * (dxh - jnp.mean(dxh, axis=-1, keepdims=True)
                          - xhat * jnp.mean(dxh * xhat, axis=-1, keepdims=True))
            return da1, da2, xhat

        da1, da2, xhat = da1_of(a1_ref[...], dabo_ref[0])
        misc_ref[pl.ds(0, 1), :] += rowsum(da1)
        misc_ref[pl.ds(1, 1), :] += rowsum(da2 * xhat)
        misc_ref[pl.ds(2, 1), :] += rowsum(da2)
        extda[pl.ds(0, ts), :] = da1
        extda[pl.ds(ts, HALO_A), :] = da1_of(na1[...], nda[...])[0] * not_last
        sga = jax.nn.sigmoid(ag[...])
        exta[pl.ds(0, HALO_A), :] = hav[...] * jax.nn.sigmoid(hag[...]) * not_first
        exta[pl.ds(HALO_A, ts), :] = av[...] * sga
        _shift_copies(exta, sha, ts + HALO_A - SUBLANES)
        _shift_copies(extda, shda, ts + HALO_A - SUBLANES)
        for r0 in range(0, ts, CONV_ROWS):
            rows = pl.ds(r0, CONV_ROWS)
            da1_rows = extda[rows, :]
            chunk = (CONV_ROWS // SUBLANES, SUBLANES, c)
            da0 = jnp.zeros(chunk, F32)
            for k in range(CONV_A_K):
                tap = da1_rows * _rows_at(exta, sha, HALO_A - (CONV_A_K - 1) + k + r0, CONV_ROWS)
                dwa_ref[pl.ds(SUBLANES * k, SUBLANES), :] += tap.reshape(chunk).sum(axis=0)
                da0 = da0 + wtile[CONV_A_K - 1 - k][None] * _rows_at(extda, shda, k + r0, CONV_ROWS).reshape(chunk)
            da0 = da0.reshape(CONV_ROWS, c)
            sg = sga[r0:r0 + CONV_ROWS]
            dp_ref[rows, 0:c] = (da0 * sg).astype(BF16)
            dp_ref[rows, c:2 * c] = (da0 * av[rows, :] * sg * (1.0 - sg)).astype(BF16)

        extb[pl.ds(0, HALO_B), :] = hsc[...] * hsx[...] * not_first
        extb[pl.ds(HALO_B, ts), :] = sc[...] * sx[...]
        dbu = dabo_ref[1]
        du = dbu * sb[...]
        extdu[pl.ds(0, ts), :] = du
        extdu[pl.ds(ts, HALO_B), :] = ndb[...] * nsb[...] * not_last
        u = jnp.zeros((ts, c), F32)
        dpr = jnp.zeros((ts, c), F32)
        for k in range(CONV_B_K):
            shifted = extb[pl.ds(HALO_B - (CONV_B_K - 1) + k, ts), :]
            u = u + cbw[pl.ds(k, 1), :] * shifted
            misc_ref[pl.ds(3 + k, 1), :] += rowsum(du * shifted)
            dpr = dpr + cbw[pl.ds(CONV_B_K - 1 - k, 1), :] * extdu[pl.ds(k, ts), :]
        dp_ref[:, 2 * c:3 * c] = (dbu * u).astype(BF16)
        dp_ref[:, 3 * c:4 * c] = (dpr * sx[...]).astype(BF16)
        dp_ref[:, 4 * c:5 * c] = (dpr * sc[...]).astype(BF16)

        nt_dims = (((1,), (1,)), ((), ()))
        tn_dims = (((0,), (0,)), ((), ()))
        for h in range(N_HEADS):
            lo, hi = h * hd, (h + 1) * hd
            qh = q[:, lo:hi].astype(BF16)
            kh = kv_ref[:, lo:hi]
            vh = kv_ref[:, c + lo:c + hi]
            p = _softmax_rows(lax.dot_general(qh, kh, nt_dims, preferred_element_type=F32) * scale)
            pb = p.astype(BF16)
            doh = dabo_ref[2, :, lo:hi].astype(BF16)
            dpm = lax.dot_general(doh, vh, nt_dims, preferred_element_type=F32)
            ds = (p * (dpm - jnp.sum(dpm * p, axis=-1, keepdims=True)) * scale).astype(BF16)
            dp_ref[:, 5 * c + lo:5 * c + hi] = jnp.dot(ds, kh, preferred_element_type=F32).astype(BF16)
            dkv_ref[:, lo:hi] += lax.dot_general(ds, qh, tn_dims, preferred_element_type=F32)
            dkv_ref[:, c + lo:c + hi] += lax.dot_general(pb, doh, tn_dims, preferred_element_type=F32)

    full = lambda shp: pl.BlockSpec(shp, lambda i: (0,) * len(shp))
    n_in_before_dproj = 21
    shifted = pltpu.VMEM((SUBLANES - 1, ts + HALO_A - SUBLANES, c), F32)
    dp, dkv, dwa, misc = pl.pallas_call(
        body, name="branch_bwd", grid=(nt,),
        in_specs=[col(0), col(1), col(2), col(3), col(4), col(5), prev(0, HALO_A), prev(1, HALO_A),
                  prev(3, HALO_B), prev(4, HALO_B), nxt(HALO_B, ci=2),
                  pl.BlockSpec((ts, c), lambda i: (i, 0)), nxt(HALO_A),
                  pl.BlockSpec((N_BRANCH, ts, c), lambda i: (0, i, 0)), nxt(HALO_A, lead=0), nxt(HALO_B, lead=1),
                  full((mlen, c2)), full(conv_a_w.shape), full((1, c)), full((1, c)), full(conv_b_w.shape),
                  pl.BlockSpec(memory_space=pl.ANY)],
        out_specs=[pl.BlockSpec((ts, 6 * c), lambda i: (i, 0)), full((mlen, c2)), full((HALO_A * SUBLANES, c)),
                   full((8, c))],
        out_shape=[jax.ShapeDtypeStruct(dproj.shape, BF16), jax.ShapeDtypeStruct((mlen, c2), F32),
                   jax.ShapeDtypeStruct((HALO_A * SUBLANES, c), F32), jax.ShapeDtypeStruct((8, c), F32)],
        scratch_shapes=[pltpu.VMEM((HALO_A + ts, c), F32), pltpu.VMEM((ts + HALO_A, c), F32),
                        pltpu.VMEM((HALO_B + ts, c), F32), pltpu.VMEM((ts + HALO_B, c), F32), shifted, shifted,
                        pltpu.VMEM((CONV_A_K, SUBLANES, c), F32)],
        input_output_aliases={n_in_before_dproj: 0},
        compiler_params=_params(),
    )(proj, proj, proj, proj, proj, proj, proj, proj, proj, proj, proj, a1, a1, dabo, dabo, dabo, kv, conv_a_w,
      ln_g, ln_b, conv_b_w, dproj)
    return dp, dkv, dwa.reshape(HALO_A, SUBLANES, c).sum(axis=1), misc


def _merge_fwd(abo, w_cat, proj, d, after):
    nb, s, c = abo.shape
    bm, bn = _tile(s, TILE_M), _tile(d, TILE_N)
    gate_col0 = (proj.shape[1] - nb * d) // bn

    def body(a_ref, w_ref, g_ref, after_ref, y_ref, m_ref, acc_ref):
        k = pl.program_id(2)
        y = jnp.dot(a_ref[...], w_ref[...], preferred_element_type=F32)
        y_ref[...] = y.astype(BF16)
        contrib = jax.nn.sigmoid(g_ref[...]) * y

        @pl.when(k == 0)
        def _():
            acc_ref[...] = contrib

        @pl.when(k > 0)
        def _():
            acc_ref[...] += contrib

        @pl.when(k == nb - 1)
        def _():
            m_ref[...] = acc_ref[...].astype(BF16)

    return pl.pallas_call(
        body, name="merge_fwd", grid=(s // bm, d // bn, nb),
        in_specs=[pl.BlockSpec((None, bm, c), lambda i, j, k: (k, i, 0)),
                  pl.BlockSpec((None, c, bn), lambda i, j, k: (k, 0, j)),
                  pl.BlockSpec((bm, bn), lambda i, j, k: (i, gate_col0 + k * (d // bn) + j)), _UNTOUCHED],
        out_specs=[pl.BlockSpec((None, bm, bn), lambda i, j, k: (k, i, j)),
                   pl.BlockSpec((bm, bn), lambda i, j, k: (i, j))],
        out_shape=[jax.ShapeDtypeStruct((nb, s, d), BF16), jax.ShapeDtypeStruct((s, d), BF16)],
        scratch_shapes=[pltpu.VMEM((bm, bn), F32)],
        compiler_params=_params(),
    )(abo, w_cat, proj, after)


def _merge_bwd(dz, w_o, y, proj):
    s, d = dz.shape
    nb = y.shape[0]
    nin = proj.shape[1]
    bm, bn = _tile(s, TILE_M), _tile(d, TILE_N)
    gate_col0 = (nin - nb * d) // bn

    def body(dz_ref, w_ref, y_ref, g_ref, dy_ref, dg_ref, acc_ref):
        @pl.when(pl.program_id(2) == 0)
        def _():
            acc_ref[...] = lax.dot_general(dz_ref[...], w_ref[...], (((1,), (1,)), ((), ())),
                                           preferred_element_type=F32)

        dm = acc_ref[...]
        gt = jax.nn.sigmoid(g_ref[...])
        dy_ref[...] = (dm * gt).astype(BF16)
        dg_ref[...] = (dm * y_ref[...].astype(F32) * gt * (1.0 - gt)).astype(BF16)

    gate = lambda i, j, k: (i, gate_col0 + k * (d // bn) + j)
    return pl.pallas_call(
        body, name="merge_bwd", grid=(s // bm, d // bn, nb),
        in_specs=[pl.BlockSpec((bm, d), lambda i, j, k: (i, 0)), pl.BlockSpec((bn, d), lambda i, j, k: (j, 0)),
                  pl.BlockSpec((None, bm, bn), lambda i, j, k: (k, i, j)), pl.BlockSpec((bm, bn), gate)],
        out_specs=[pl.BlockSpec((None, bm, bn), lambda i, j, k: (k, i, j)), pl.BlockSpec((bm, bn), gate)],
        out_shape=[jax.ShapeDtypeStruct((nb, s, d), BF16), jax.ShapeDtypeStruct((s, nin), BF16)],
        scratch_shapes=[pltpu.VMEM((bm, bn), F32)],
        compiler_params=_params(),
    )(dz, w_o, y, proj)


def _window(ref, axis, who, length):
    idx = [slice(None)] * len(ref.shape)
    idx[axis] = pl.ds(pl.multiple_of(who * length, length), length)
    return ref.at[tuple(idx)]


def _all_gather(name, shards, axes, after):
    n = len(shards)
    n_copies = 7
    out_shapes = []
    for sh, ax in zip(shards, axes):
        shp = list(sh.shape)
        shp[ax] *= N_DEV
        out_shapes.append(jax.ShapeDtypeStruct(tuple(shp), sh.dtype))

    def body(*refs):
        ins = refs[:n]
        outs = refs[n + len(after):2 * n + len(after)]
        send, recv, local = refs[2 * n + len(after):]
        x, y, c = lax.axis_index("x"), lax.axis_index("y"), lax.axis_index("c")
        sibling = (x, y, 1 - c)
        chips = [(1 - x, y), (x, 1 - y), (1 - x, 1 - y)]
        lens = [ins[a].shape[axes[a]] for a in range(n)]

        def block(a, px, py, pc):
            return _window(outs[a], axes[a], 4 * px + 2 * py + pc, lens[a])

        def copy(a, k, src, dst, to):
            return pltpu.make_async_remote_copy(src_ref=src, dst_ref=dst, send_sem=send.at[a, k],
                                                recv_sem=recv.at[a, k], device_id=to,
                                                device_id_type=pl.DeviceIdType.MESH)

        mine = [pltpu.make_async_copy(ins[a], block(a, x, y, c), local.at[a]) for a in range(n)]
        for cp in mine:
            cp.start()
        started = []
        for j, chip in enumerate(chips):
            for a in range(n):
                started.append(copy(a, 1 + j, ins[a], block(a, x, y, c), (*chip, c)))
                started[-1].start()
        for a in range(n):
            started.append(copy(a, 0, ins[a], block(a, x, y, c), sibling))
            started[-1].start()
        for j, chip in enumerate(chips):
            for a in range(n):
                landed = block(a, *chip, c)
                copy(a, 1 + j, landed, landed, (*chip, c)).wait_recv()
                started.append(copy(a, 4 + j, landed, landed, sibling))
                started[-1].start()
        for a in range(n):
            copy(a, 0, ins[a], block(a, x, y, 1 - c), sibling).wait_recv()
            for j, chip in enumerate(chips):
                copy(a, 4 + j, ins[a], block(a, *chip, 1 - c), sibling).wait_recv()
        for cp in started:
            cp.wait_send()
        for cp in mine:
            cp.wait()

    hbm = pl.BlockSpec(memory_space=pl.ANY)
    return pl.pallas_call(
        body, name=name, in_specs=[hbm] * (n + len(after)), out_specs=[hbm] * n, out_shape=out_shapes,
        scratch_shapes=[pltpu.SemaphoreType.DMA((n, n_copies)), pltpu.SemaphoreType.DMA((n, n_copies)),
                        pltpu.SemaphoreType.DMA((n,))],
    )(*shards, *after)


def _pair(a, d):
    return a * N_DEV + d


def _push_ends(kind, src, land, axis, length, me, to):
    if kind == "gather":
        return src, _window(land, axis, me, length)
    return _window(src, axis, to, length), land.at[me]


def _arrival_ends(kind, src, land, axis, length, me, frm):
    if kind == "gather":
        return src, _window(land, axis, frm, length)
    return _window(src, axis, me, length), land.at[frm]


def _exchange_start(name, kind, srcs, axes, after):
    n = len(srcs)
    if kind == "gather":
        lens = [s.shape[ax] for s, ax in zip(srcs, axes)]
        land_shapes = [s.shape[:ax] + (s.shape[ax] * N_DEV,) + s.shape[ax + 1:] for s, ax in zip(srcs, axes)]
    else:
        lens = [s.shape[ax] // N_DEV for s, ax in zip(srcs, axes)]
        land_shapes = [(N_DEV,) + s.shape[:ax] + (ln,) + s.shape[ax + 1:] for s, ax, ln in zip(srcs, axes, lens)]
    lands = [lax.empty(shp, s.dtype) for shp, s in zip(land_shapes, srcs)]

    def body(*refs):
        ins = refs[:n]
        send, recv = refs[2 * n + 1], refs[2 * n + 2]
        lnd = refs[2 * n + 3 + n:2 * n + 3 + 2 * n]
        token = refs[-1]
        me = _my_index()
        for a in range(n):
            for d in range(1, N_DEV):
                to = (me + d) % N_DEV
                src, dst = _push_ends(kind, ins[a], lnd[a], axes[a], lens[a], me, to)
                pltpu.make_async_remote_copy(src_ref=src, dst_ref=dst, send_sem=send.at[_pair(a, d)],
                                             recv_sem=recv.at[_pair(a, d)], device_id=_mesh_id(to),
                                             device_id_type=pl.DeviceIdType.MESH).start()
        for a in range(n):
            pltpu.make_async_copy(*_push_ends(kind, ins[a], lnd[a], axes[a], lens[a], me, me),
                                  send.at[_pair(a, 0)]).start()
        token[...] = jnp.zeros_like(token)

    hbm = pl.BlockSpec(memory_space=pltpu.HBM)
    sem = pl.BlockSpec(memory_space=pltpu.SEMAPHORE)
    held = [pltpu.with_memory_space_constraint(t, pltpu.HBM) for t in list(srcs) + lands]
    outs = pl.pallas_call(
        body, name=name,
        in_specs=[hbm] * (2 * n) + [_UNTOUCHED],
        out_specs=[sem, sem] + [hbm] * (2 * n) + [pl.BlockSpec(memory_space=pltpu.VMEM)],
        out_shape=[pltpu.SemaphoreType.DMA((n * N_DEV,)), pltpu.SemaphoreType.DMA((n * N_DEV,))]
        + [pltpu.HBM(t.shape, t.dtype) for t in held] + [jax.ShapeDtypeStruct((8, 128), F32)],
        input_output_aliases={i: 2 + i for i in range(2 * n)},
        compiler_params=pltpu.CompilerParams(has_side_effects=pltpu.SideEffectType.DATAFLOW_SIDE_EFFECTING),
    )(*held, after)
    handle = dict(kind=kind, axes=axes, lens=lens, send=outs[0], recv=outs[1], srcs=outs[2:2 + n],
                  lands=outs[2 + n:2 + 2 * n])
    return handle, outs[-1]


def _exchange_wait(name, handle, after):
    kind, axes, lens = handle["kind"], handle["axes"], handle["lens"]
    srcs, lands = handle["srcs"], handle["lands"]
    n = len(srcs)
    n_in = 2 * n + 2 + len(after)

    def body(*refs):
        ins = refs[:n]
        send, recv = refs[2 * n], refs[2 * n + 1]
        got = refs[n_in + n:n_in + 2 * n]
        me = _my_index()
        for d in range(1, N_DEV):
            frm = (me + N_DEV - d) % N_DEV
            for a in range(n):
                src, dst = _arrival_ends(kind, ins[a], got[a], axes[a], lens[a], me, frm)
                pltpu.make_async_remote_copy(src_ref=src, dst_ref=dst, send_sem=send.at[_pair(a, d)],
                                             recv_sem=recv.at[_pair(a, d)], device_id=_mesh_id(frm),
                                             device_id_type=pl.DeviceIdType.MESH).wait_recv()
        for d in range(1, N_DEV):
            to = (me + d) % N_DEV
            for a in range(n):
                src, dst = _push_ends(kind, ins[a], got[a], axes[a], lens[a], me, to)
                pltpu.make_async_remote_copy(src_ref=src, dst_ref=dst, send_sem=send.at[_pair(a, d)],
                                             recv_sem=recv.at[_pair(a, d)], device_id=_mesh_id(to),
                                             device_id_type=pl.DeviceIdType.MESH).wait_send()
        for a in range(n):
            pltpu.make_async_copy(*_push_ends(kind, ins[a], got[a], axes[a], lens[a], me, me),
                                  send.at[_pair(a, 0)]).wait()

    hbm = pl.BlockSpec(memory_space=pltpu.HBM)
    sem = pl.BlockSpec(memory_space=pltpu.SEMAPHORE)
    outs = pl.pallas_call(
        body, name=name,
        in_specs=[hbm] * (2 * n) + [sem, sem] + [_UNTOUCHED] * len(after),
        out_specs=[hbm] * (2 * n),
        out_shape=[pltpu.HBM(t.shape, t.dtype) for t in list(srcs) + list(lands)],
        input_output_aliases={i: i for i in range(2 * n)},
        compiler_params=pltpu.CompilerParams(has_side_effects=pltpu.SideEffectType.DATAFLOW_SIDE_EFFECTING),
    )(*srcs, *lands, handle["send"], handle["recv"], *after)
    return outs[n:]


def _all_reduce_small(packed, after):
    r, c = packed.shape

    def body(p_ref, *rest):
        o_ref, buf, send, recv = rest[len(after):]
        me = _my_index()
        buf[me] = p_ref[...]
        pushes = []
        for d in range(1, N_DEV):
            to = (me + d) % N_DEV
            cp = pltpu.make_async_remote_copy(src_ref=p_ref, dst_ref=buf.at[me], send_sem=send.at[d],
                                              recv_sem=recv.at[d], device_id=_mesh_id(to),
                                              device_id_type=pl.DeviceIdType.MESH)
            cp.start()
            pushes.append(cp)
        for d in range(1, N_DEV):
            frm = (me + N_DEV - d) % N_DEV
            pltpu.make_async_remote_copy(src_ref=p_ref, dst_ref=buf.at[frm], send_sem=send.at[d], recv_sem=recv.at[d],
                                         device_id=_mesh_id(frm), device_id_type=pl.DeviceIdType.MESH).wait_recv()
        for cp in pushes:
            cp.wait_send()
        acc = buf[0]
        for s in range(1, N_DEV):
            acc = acc + buf[s]
        o_ref[...] = acc

    vmem = pl.BlockSpec(memory_space=pltpu.VMEM)
    return pl.pallas_call(
        body, name="all_reduce_small", in_specs=[vmem] + [_UNTOUCHED] * len(after), out_specs=vmem,
        out_shape=jax.ShapeDtypeStruct((r, c), F32),
        scratch_shapes=[pltpu.VMEM((N_DEV, r, c), F32), pltpu.SemaphoreType.DMA((N_DEV,)),
                        pltpu.SemaphoreType.DMA((N_DEV,))],
        compiler_params=_params(),
    )(packed, *after)


def _layer_fwd_branches(x, h, mem, sm, wf):
    proj = _mm_nn("proj", h, wf["w_in"], F32)
    mem_n = _rmsnorm("rms_mem", mem, sm["g_mem"], mem)
    kv = _mm_nn("kv", mem_n, wf["w_kv"], BF16)
    a1, abo = _branch_fwd(proj, kv, wf["conv_a_w"], sm["conv_a_b"], sm["ln_a_g"], sm["ln_a_b"], wf["conv_b_w"])
    return dict(x=x, h=h, proj=proj, mem_n=mem_n, kv=kv, a1=a1, abo=abo)


def _layer_fwd_rest(sm, wf, sv, after, g_next):
    x = sv["x"]
    y, merged = _merge_fwd(sv["abo"], wf["w_cat"], sv["proj"], x.shape[1], after)
    z = _mm_nn("mix_out", merged, wf["w_o"], F32)
    x1, h2 = _res_norm("mix_out_norm", z, x, sm["g_mix_post"], sm["g_mlp_pre"])
    r, act = _mm_relu2("mlp_up", h2, wf["w_up"])
    f = _mm_nn("mlp_down", act, wf["w_down"], F32)
    x2, h_next = _res_norm("mlp_down_norm", f, x1, sm["g_mlp_post"], g_next)
    sv.update(y=y, merged=merged, z=z, x1=x1, h2=h2, r=r, act=act, f=f)
    return x2, h_next


def _pad_rows(t):
    return jnp.pad(t, ((0, (-t.shape[0]) % 8), (0, 0)))


def _layer_bwd(dx2, df, mem, sm, wf, sv, send_off, below):
    c = sv["a1"].shape[1]
    d_up = _mm_relu2_bwd("mlp_down_bwd", df, wf["w_down"], sv["r"])
    gw_down = _mm_tn("gw_down", sv["act"], df, BF16)
    dh2 = _mm_nt("mlp_up_bwd", d_up, wf["w_up"], F32)
    gw_up = _mm_tn("gw_up", sv["h2"], d_up, BF16)
    sent = send_off(dict(w_up=gw_up, w_down=gw_down))
    dx1, dg_mlp_pre, dz, dg_mix_post = _rms_bwd("rms_bwd_mlp_pre", dh2, sv["x1"], sm["g_mlp_pre"], F32, sent,
                                                residual=dx2, below=(sv["z"], sm["g_mix_post"]))
    dy, dproj = _merge_bwd(dz, wf["w_o"], sv["y"], sv["proj"])
    gw_o = _mm_tn("gw_o", sv["merged"], dz, BF16)
    dabo = _mm_nt("branch_out_bwd", dy, wf["w_cat"], F32)
    gw_cat = _mm_tn("gw_branch_out", sv["abo"], dy, BF16)
    dproj, dkv, dconv_a_w, misc = _branch_bwd(sv["proj"], sv["a1"], dabo, sv["kv"], wf["conv_a_w"], sm["ln_a_g"],
                                              sm["ln_a_b"], wf["conv_b_w"], dproj)
    dkv = dkv.astype(BF16)
    gw_kv = _mm_tn("gw_kv", sv["mem_n"], dkv, BF16)
    dmem_n = _mm_nt("kv_bwd", dkv, wf["w_kv"], F32)
    _, dg_mem = _rms_bwd("rms_bwd_mem", dmem_n, mem, sm["g_mem"], BF16, dmem_n)
    gw_in = _mm_tn("gw_in", sv["h"], dproj, BF16)
    sent = send_off(dict(w_in=gw_in, w_kv=gw_kv, w_cat=gw_cat, w_o=gw_o))
    dh = _mm_nt("proj_bwd", dproj, wf["w_in"], F32, after=sent)
    dx, dg_mix_pre, *rest = _rms_bwd("rms_bwd_mix_pre", dh, sv["x"], sm["g_mix_pre"], F32, dh, residual=dx1,
                                     below=below)
    small = dict(g_mix_pre=dg_mix_pre, g_mem=dg_mem, g_mix_post=dg_mix_post, g_mlp_pre=dg_mlp_pre, misc=misc,
                 conv_a_w=dconv_a_w)
    return dx, small, rest


def _pack_small(small, c):
    rows = [_pad_rows(small[k].reshape(2, c)) for k in _SMALL_D_NAMES]
    return jnp.concatenate(rows + [small["misc"], small["conv_a_w"]], axis=0)


_SMALL_D_NAMES = ["g_mix_pre", "g_mem", "g_mix_post", "g_mlp_pre", "g_mlp_post"]
_SMALL_C_NAMES = ["conv_a_b", "ln_a_g", "ln_a_b"]
_SMALL_MISC_ROW = 8 * len(_SMALL_D_NAMES)
_SMALL_CONV_B_ROW = _SMALL_MISC_ROW + len(_SMALL_C_NAMES)
_SMALL_CONV_A_ROW = _SMALL_MISC_ROW + 8
_SMALL_ROWS = _SMALL_CONV_A_ROW + HALO_A


def kernel(x, mem, g_mix_pre, w_in, conv_a_w, conv_a_b, ln_a_g, ln_a_b, w_a_out, conv_b_w, w_b_out, g_mem, w_kv, w_x_out, w_o, g_mix_post, g_mlp_pre, w_up, w_down, g_mlp_post, loss_target, m_g_mix_pre, m_w_in, m_conv_a_w, m_conv_a_b, m_ln_a_g, m_ln_a_b, m_w_a_out, m_conv_b_w, m_w_b_out, m_g_mem, m_w_kv, m_w_x_out, m_w_o, m_g_mix_post, m_g_mlp_pre, m_w_up, m_w_down, m_g_mlp_post, v_g_mix_pre, v_w_in, v_conv_a_w, v_conv_a_b, v_ln_a_g, v_ln_a_b, v_w_a_out, v_conv_b_w, v_w_b_out, v_g_mem, v_w_kv, v_w_x_out, v_w_o, v_g_mix_post, v_g_mlp_pre, v_w_up, v_w_down, v_g_mlp_post):
    names = ["g_mix_pre", "w_in", "conv_a_w", "conv_a_b", "ln_a_g", "ln_a_b", "w_a_out", "conv_b_w", "w_b_out",
             "g_mem", "w_kv", "w_x_out", "w_o", "g_mix_post", "g_mlp_pre", "w_up", "w_down", "g_mlp_post"]
    w = dict(zip(names, [g_mix_pre, w_in, conv_a_w, conv_a_b, ln_a_g, ln_a_b, w_a_out, conv_b_w, w_b_out, g_mem,
                         w_kv, w_x_out, w_o, g_mix_post, g_mlp_pre, w_up, w_down, g_mlp_post]))
    mo = dict(zip(names, [m_g_mix_pre, m_w_in, m_conv_a_w, m_conv_a_b, m_ln_a_g, m_ln_a_b, m_w_a_out, m_conv_b_w,
                          m_w_b_out, m_g_mem, m_w_kv, m_w_x_out, m_w_o, m_g_mix_post, m_g_mlp_pre, m_w_up, m_w_down,
                          m_g_mlp_post]))
    vo = dict(zip(names, [v_g_mix_pre, v_w_in, v_conv_a_w, v_conv_a_b, v_ln_a_g, v_ln_a_b, v_w_a_out, v_conv_b_w,
                          v_w_b_out, v_g_mem, v_w_kv, v_w_x_out, v_w_o, v_g_mix_post, v_g_mlp_pre, v_w_up, v_w_down,
                          v_g_mlp_post]))
    depth = w_in.shape[0]
    c = conv_a_b.shape[1]
    cs = conv_a_w.shape[2]
    me = _my_index()
    xs, mems, tgt = x[0], mem[0], loss_target[0]

    big_names = ["w_in", "w_kv", "w_cat", "w_o", "w_up", "w_down"]
    big_axes = [1, 0, 2, 0, 1, 0]
    n_early = 2
    branch_names = ["w_a_out", "w_b_out", "w_x_out"]
    smalls = [{k: w[k][l][None, :] for k in _SMALL_D_NAMES + _SMALL_C_NAMES} for l in range(depth)]

    def weight_shards(l):
        return [w_in[l].astype(BF16), w_kv[l].astype(BF16), jnp.stack([w[k][l] for k in branch_names]).astype(BF16),
                w_o[l].astype(BF16), w_up[l].astype(BF16), w_down[l].astype(BF16)]

    def start_gather(l, after):
        shards = weight_shards(l)
        early, token = _exchange_start(f"gather_start_{l}_early", "gather", shards[:n_early], big_axes[:n_early],
                                       after)
        late, token = _exchange_start(f"gather_start_{l}_late", "gather", shards[n_early:], big_axes[n_early:], token)
        return early, late, token

    taps_a, taps_b = _all_gather(
        "gather_conv_taps",
        [jnp.pad(conv_a_w, ((0, 0), (0, HALO_A - CONV_A_K), (0, 0))),
         jnp.pad(conv_b_w, ((0, 0), (0, HALO_B - CONV_B_K), (0, 0)))], [2, 2], [])

    shards = weight_shards(0)
    first = _all_gather("gather_0_early", shards[:n_early], big_axes[:n_early], [taps_a])
    late, token = _exchange_start("gather_start_0_late", "gather", shards[n_early:], big_axes[n_early:], first[0])
    fulls = [None] * depth
    saved = []
    xc = xs
    h = _rmsnorm("rms_mix_pre", xs, smalls[0]["g_mix_pre"], token)
    for l in range(depth):
        arrived = first if l == 0 else _exchange_wait(f"gather_wait_{l}_early", early, [xc])
        fulls[l] = dict(zip(big_names[:n_early], arrived))
        fulls[l]["conv_a_w"], fulls[l]["conv_b_w"] = taps_a[l], taps_b[l]
        sv = _layer_fwd_branches(xc, h, mems, smalls[l], fulls[l])
        fulls[l].update(zip(big_names[n_early:], _exchange_wait(f"gather_wait_{l}_late", late, [sv["abo"]])))
        token = fulls[l]["w_o"]
        if l + 1 < depth:
            early, late, token = start_gather(l + 1, token)
        xc, h = _layer_fwd_rest(smalls[l], fulls[l], sv, token,
                                smalls[l + 1]["g_mix_pre"] if l + 1 < depth else None)
        saved.append(sv)
    loss_part, dx = _loss_head(xc, tgt)
    loss = lax.psum(loss_part, MESH_AXES)

    upd = {}

    def adamw_group(l, group, slots, after):
        for k, sl in zip(group, slots):
            if k == "w_cat":
                for b, nm in enumerate(branch_names):
                    upd[nm] = _adamw("adamw_" + nm, sl, b, w[nm], mo[nm], vo[nm], l, upd.get(nm), after)
            else:
                upd[k] = _adamw("adamw_" + k, sl, None, w[k], mo[k], vo[k], l, upd.get(k), after)

    in_flight = [None]

    def take_in(after):
        handle, l, group = in_flight[0]
        return _exchange_wait(f"scatter_wait_{l}_{group[0]}", handle, after), l, group

    def send_off_layer(l):
        def send_off(grads):
            group = [k for k in big_names if k in grads]
            arrays = [grads[k] for k in group]
            arrived = None if in_flight[0] is None else take_in([arrays[0]])
            handle, token = _exchange_start(f"scatter_start_{l}_{group[0]}", "scatter", arrays,
                                            [big_axes[big_names.index(k)] for k in group],
                                            arrays[0] if arrived is None else arrived[0][0])
            in_flight[0] = (handle, l, group)
            if arrived is not None:
                adamw_group(arrived[1], arrived[2], arrived[0], token)
            return token
        return send_off

    small_g = [dict() for _ in range(depth)]
    df, small_g[depth - 1]["g_mlp_post"] = _rms_bwd("rms_bwd_top", dx, saved[depth - 1]["f"],
                                                    smalls[depth - 1]["g_mlp_post"], BF16, dx)
    for l in reversed(range(depth)):
        below = (saved[l - 1]["f"], smalls[l - 1]["g_mlp_post"]) if l > 0 else None
        dx, small, rest = _layer_bwd(dx, df, mems, smalls[l], fulls[l], saved[l], send_off_layer(l), below)
        small_g[l].update(small)
        if l > 0:
            df, small_g[l - 1]["g_mlp_post"] = rest
    small_parts = [_pack_small(small_g[l], c) for l in range(depth)]

    slots, l, group = take_in([u[0] for u in upd.values()] + [dx])
    adamw_group(l, group, slots, dx)

    tot = _all_reduce_small(jnp.concatenate(small_parts, axis=0), [upd[k][0] for k in group if k in upd])
    tot = tot.reshape(depth, _SMALL_ROWS, c)
    g_small = {}
    for n_, nm in enumerate(_SMALL_D_NAMES):
        g_small[nm] = tot[:, 8 * n_:8 * n_ + 2, :].reshape(depth, 2 * c)
    for n_, nm in enumerate(_SMALL_C_NAMES):
        g_small[nm] = tot[:, _SMALL_MISC_ROW + n_, :]
    g_taps = {"conv_a_w": tot[:, _SMALL_CONV_A_ROW:_SMALL_CONV_A_ROW + CONV_A_K, :],
              "conv_b_w": tot[:, _SMALL_CONV_B_ROW:_SMALL_CONV_B_ROW + CONV_B_K, :]}

    def pack(t):
        return jnp.concatenate([_pad_rows(t[nm].reshape(-1, c)) for nm in _SMALL_D_NAMES + _SMALL_C_NAMES], axis=0)

    res = _adamw("adamw_small", pack(g_small)[None], None, pack(w)[None], pack(mo)[None], pack(vo)[None], 0, None, tot)
    row = 0
    for nm in _SMALL_D_NAMES + _SMALL_C_NAMES:
        n_rows = w[nm].size // c
        upd[nm] = [t[0, row:row + n_rows].reshape(w[nm].shape) for t in res]
        row += n_rows + (-n_rows) % 8
    for nm, g in g_taps.items():
        shp = w[nm].shape
        mine = lax.dynamic_slice_in_dim(g, me * cs, cs, axis=2).reshape(1, -1, cs)
        res = _adamw("adamw_" + nm, mine, None, w[nm].reshape(1, -1, cs), mo[nm].reshape(1, -1, cs),
                     vo[nm].reshape(1, -1, cs), 0, None, tot)
        upd[nm] = [t.reshape(shp) for t in res]

    return (loss, dx[None], *[upd[nm][0] for nm in names], *[upd[nm][1] for nm in names],
            *[upd[nm][2] for nm in names], *[upd[nm][3] for nm in names])
```

```python
import jax
import jax.numpy as jnp
from jax import lax
from jax.experimental import pallas as pl
from jax.experimental.pallas import tpu as pltpu

F32 = jnp.float32
BF16 = jnp.bfloat16

EPS = 1e-6
N_HEADS = 4
N_BRANCH = 3
CONV_A_K = 31
CONV_B_K = 3
HALO_A = 32
HALO_B = 8
N_DEV = 8
MESH_AXES = ("x", "y", "c")

ADAM_LR = 0.001
ADAM_B1 = 0.9
ADAM_B2 = 0.999
ADAM_EPS = 1e-08
ADAM_WD = 0.01
ADAM_STEP = 10

V7X_VMEM_BYTES = 64 * 1024 * 1024
VMEM_LIMIT = V7X_VMEM_BYTES - 8 * 1024 * 1024

TILE_M = 1024
TILE_N = 1024
TILE_K = 4096
TILE_N_SHALLOW = 2048
TILE_K_SHALLOW = 2048
TILE_ROWS_WIDE = 512
TILE_ROWS_BRANCH = 256
TILE_ROWS_BRANCH_BWD = 128


def _params():
    return pltpu.CompilerParams(vmem_limit_bytes=VMEM_LIMIT)


def _tile(n, t):
    t = min(n, t)
    assert n % t == 0, (n, t)
    return t


def _my_index():
    return 4 * lax.axis_index("x") + 2 * lax.axis_index("y") + lax.axis_index("c")


def _mesh_id(p):
    return (p // 4, (p // 2) % 2, p % 2)


def _mm(name, grid, a, a_spec, b, b_spec, *, ta, tb, acc_shape, extras, outs, epilogue):
    nk = grid[-1]
    lead = len(grid) - 3
    ne, no = len(extras), len(outs)
    dn = (((0,) if ta else (1,), (1,) if tb else (0,)), ((), ()))

    def body(a_ref, b_ref, *rest):
        extra_refs = rest[:ne]
        out_refs = rest[ne:ne + no]
        i, j, k = pl.program_id(lead), pl.program_id(lead + 1), pl.program_id(lead + 2)
        prod = lax.dot_general(a_ref[...], b_ref[...], dn, preferred_element_type=F32)
        if nk == 1:
            epilogue(prod, extra_refs, out_refs, i, j)
        else:
            acc_ref = rest[ne + no]

            @pl.when(k == 0)
            def _():
                acc_ref[...] = prod

            @pl.when(k > 0)
            def _():
                acc_ref[...] += prod

            @pl.when(k == nk - 1)
            def _():
                epilogue(acc_ref[...], extra_refs, out_refs, i, j)

    return pl.pallas_call(
        body, name=name, grid=grid,
        in_specs=[a_spec, b_spec] + [s for _, s in extras],
        out_specs=[s for _, s in outs],
        out_shape=[o for o, _ in outs],
        scratch_shapes=[pltpu.VMEM(acc_shape, F32)] if nk > 1 else [],
        compiler_params=_params(),
    )(a, b, *[e for e, _ in extras])


def _tile_n(n, kd):
    return _tile(n, TILE_N_SHALLOW if kd <= TILE_K_SHALLOW else TILE_N)


def _store_epilogue(res, extra_refs, out_refs, i, j):
    out_refs[0][...] = res.astype(out_refs[0].dtype)


def _mm_nn(name, a, b, out_dtype):
    m, kd = a.shape
    n = b.shape[1]
    bm, bn, bk = _tile(m, TILE_M), _tile_n(n, kd), _tile(kd, TILE_K)
    return _mm(name, (m // bm, n // bn, kd // bk), a, pl.BlockSpec((bm, bk), lambda i, j, k: (i, k)),
               b, pl.BlockSpec((bk, bn), lambda i, j, k: (k, j)), ta=False, tb=False, acc_shape=(bm, bn), extras=[],
               outs=[(jax.ShapeDtypeStruct((m, n), out_dtype), pl.BlockSpec((bm, bn), lambda i, j, k: (i, j)))],
               epilogue=_store_epilogue)[0]


def _mm_nt(name, a, b, out_dtype, after=None):
    m, kd = a.shape[-2:]
    n = b.shape[-2]
    bm, bn = _tile(m, TILE_M), _tile(n, TILE_N)
    if a.ndim == 2:
        bk = _tile(kd, TILE_K)
        return _mm(name, (m // bm, n // bn, kd // bk), a, pl.BlockSpec((bm, bk), lambda i, j, k: (i, k)),
                   b, pl.BlockSpec((bn, bk), lambda i, j, k: (j, k)), ta=False, tb=True, acc_shape=(bm, bn),
                   extras=[] if after is None else [(after, _UNTOUCHED)],
                   outs=[(jax.ShapeDtypeStruct((m, n), out_dtype), pl.BlockSpec((bm, bn), lambda i, j, k: (i, j)))],
                   epilogue=_store_epilogue)[0]
    nb = a.shape[0]
    return _mm(name, (nb, m // bm, n // bn, 1), a, pl.BlockSpec((None, bm, kd), lambda s, i, j, k: (s, i, 0)),
               b, pl.BlockSpec((None, bn, kd), lambda s, i, j, k: (s, j, 0)), ta=False, tb=True, acc_shape=(bm, bn),
               extras=[],
               outs=[(jax.ShapeDtypeStruct((nb, m, n), out_dtype),
                      pl.BlockSpec((None, bm, bn), lambda s, i, j, k: (s, i, j)))],
               epilogue=_store_epilogue)[0]


def _mm_tn(name, a, b, out_dtype):
    r, m = a.shape[-2:]
    n = b.shape[-1]
    bm, bn, bk = _tile(m, TILE_M), _tile(n, TILE_N), _tile(r, TILE_K)
    if a.ndim == 2:
        return _mm(name, (m // bm, n // bn, r // bk), a, pl.BlockSpec((bk, bm), lambda i, j, k: (k, i)),
                   b, pl.BlockSpec((bk, bn), lambda i, j, k: (k, j)), ta=True, tb=False, acc_shape=(bm, bn), extras=[],
                   outs=[(jax.ShapeDtypeStruct((m, n), out_dtype), pl.BlockSpec((bm, bn), lambda i, j, k: (i, j)))],
                   epilogue=_store_epilogue)[0]
    nb = a.shape[0]
    return _mm(name, (nb, m // bm, n // bn, r // bk), a, pl.BlockSpec((None, bk, bm), lambda s, i, j, k: (s, k, i)),
               b, pl.BlockSpec((None, bk, bn), lambda s, i, j, k: (s, k, j)), ta=True, tb=False, acc_shape=(bm, bn),
               extras=[],
               outs=[(jax.ShapeDtypeStruct((nb, m, n), out_dtype),
                      pl.BlockSpec((None, bm, bn), lambda s, i, j, k: (s, i, j)))],
               epilogue=_store_epilogue)[0]


def _mm_relu2(name, h, w):
    m, kd = h.shape
    n = w.shape[1]
    bm, bn = _tile(m, TILE_M), _tile_n(n, kd)

    def epilogue(res, extra_refs, out_refs, i, j):
        r = jnp.maximum(res, 0.0)
        out_refs[0][...] = r.astype(BF16)
        out_refs[1][...] = (r * r).astype(BF16)

    o = jax.ShapeDtypeStruct((m, n), BF16)
    spec = pl.BlockSpec((bm, bn), lambda i, j, k: (i, j))
    return _mm(name, (m // bm, n // bn, 1), h, pl.BlockSpec((bm, kd), lambda i, j, k: (i, 0)),
               w, pl.BlockSpec((kd, bn), lambda i, j, k: (0, j)), ta=False, tb=False, acc_shape=(bm, bn), extras=[],
               outs=[(o, spec), (o, spec)], epilogue=epilogue)


def _mm_relu2_bwd(name, df, w_down, r):
    m, kd = df.shape
    n = w_down.shape[0]
    bm, bn = _tile(m, TILE_M), _tile_n(n, kd)

    def epilogue(res, extra_refs, out_refs, i, j):
        out_refs[0][...] = (res * (2.0 * extra_refs[0][...].astype(F32))).astype(BF16)

    spec = pl.BlockSpec((bm, bn), lambda i, j, k: (i, j))
    return _mm(name, (m // bm, n // bn, 1), df, pl.BlockSpec((bm, kd), lambda i, j, k: (i, 0)),
               w_down, pl.BlockSpec((bn, kd), lambda i, j, k: (j, 0)), ta=False, tb=True, acc_shape=(bm, bn),
               extras=[(r, spec)], outs=[(jax.ShapeDtypeStruct((m, n), BF16), spec)], epilogue=epilogue)[0]


def _rms_bwd_math(dy, x, g):
    rr = lax.rsqrt(jnp.mean(x * x, axis=-1, keepdims=True) + EPS)
    gy = dy * g
    dx = rr * gy - x * (rr * rr * rr * jnp.mean(x * gy, axis=-1, keepdims=True))
    dg_rows = dy * x * rr
    return dx, dg_rows


_UNTOUCHED = pl.BlockSpec(memory_space=pl.ANY)


def _rmsnorm(name, x, g, after):
    r, d = x.shape
    br = _tile(r, TILE_ROWS_WIDE)

    def body(x_ref, g_ref, after_ref, o_ref):
        xv = x_ref[...]
        rr = lax.rsqrt(jnp.mean(xv * xv, axis=-1, keepdims=True) + EPS)
        o_ref[...] = ((xv * rr) * g_ref[...]).astype(BF16)

    row = pl.BlockSpec((br, d), lambda i: (i, 0))
    return pl.pallas_call(body, name=name, grid=(r // br,),
                          in_specs=[row, pl.BlockSpec((1, d), lambda i: (0, 0)), _UNTOUCHED],
                          out_specs=row, out_shape=jax.ShapeDtypeStruct((r, d), BF16),
                          compiler_params=_params())(x, g, after)


def _res_norm(name, z, x, g, g_next=None):
    r, d = x.shape
    br = _tile(r, TILE_ROWS_WIDE)
    n_next = 0 if g_next is None else 1

    def body(z_ref, x_ref, g_ref, *rest):
        zv = z_ref[...]
        rr = lax.rsqrt(jnp.mean(zv * zv, axis=-1, keepdims=True) + EPS)
        xn = x_ref[...] + (zv * rr) * g_ref[...]
        rest[n_next][...] = xn
        if n_next:
            r2 = lax.rsqrt(jnp.mean(xn * xn, axis=-1, keepdims=True) + EPS)
            rest[2][...] = ((xn * r2) * rest[0][...]).astype(BF16)

    row = pl.BlockSpec((br, d), lambda i: (i, 0))
    vec = pl.BlockSpec((1, d), lambda i: (0, 0))
    outs = pl.pallas_call(body, name=name, grid=(r // br,), in_specs=[row, row, vec] + [vec] * n_next,
                          out_specs=[row] * (1 + n_next),
                          out_shape=[jax.ShapeDtypeStruct((r, d), F32)] + [jax.ShapeDtypeStruct((r, d), BF16)] * n_next,
                          compiler_params=_params())(z, x, g, *([] if g_next is None else [g_next]))
    return outs if n_next else (outs[0], None)


def _rms_bwd(name, dy, x, g, out_dtype, after, residual=None, below=None):
    r, d = x.shape
    br = _tile(r, TILE_ROWS_WIDE if below is None else TILE_ROWS_WIDE // 2)
    n_res = 0 if residual is None else 1
    n_below = 0 if below is None else 1

    def body(dy_ref, x_ref, g_ref, after_ref, *rest):
        ins, outs = rest[:n_res + 2 * n_below], rest[n_res + 2 * n_below:]
        first = pl.program_id(0) == 0
        dx, dg_rows = _rms_bwd_math(dy_ref[...], x_ref[...], g_ref[...])
        if n_res:
            dx = dx + ins[0][...]
        outs[0][...] = dx.astype(out_dtype)

        @pl.when(first)
        def _():
            outs[1][...] = jnp.zeros_like(outs[1])

        outs[1][...] += jnp.sum(dg_rows, axis=0, keepdims=True)
        if n_below:
            dz, dgz_rows = _rms_bwd_math(dx, ins[n_res][...], ins[n_res + 1][...])
            outs[2][...] = dz.astype(BF16)

            @pl.when(first)
            def _():
                outs[3][...] = jnp.zeros_like(outs[3])

            outs[3][...] += jnp.sum(dgz_rows, axis=0, keepdims=True)

    row = pl.BlockSpec((br, d), lambda i: (i, 0))
    vec = pl.BlockSpec((1, d), lambda i: (0, 0))
    o_row, o_vec = jax.ShapeDtypeStruct((r, d), out_dtype), jax.ShapeDtypeStruct((1, d), F32)
    return pl.pallas_call(
        body, name=name, grid=(r // br,),
        in_specs=[row, row, vec, _UNTOUCHED] + [row] * n_res + [row, vec] * n_below,
        out_specs=[row, vec] + [row, vec] * n_below,
        out_shape=[o_row, o_vec] + [jax.ShapeDtypeStruct((r, d), BF16), o_vec] * n_below,
        compiler_params=_params(),
    )(dy, x, g, after, *([] if residual is None else [residual]), *([] if below is None else below))


def _loss_head(y, target):
    r, d = y.shape
    br = _tile(r, TILE_ROWS_WIDE)

    def body(y_ref, t_ref, dy_ref, l_ref):
        diff = y_ref[...] - t_ref[...]
        dy_ref[...] = diff * (1.0 / d)

        @pl.when(pl.program_id(0) == 0)
        def _():
            l_ref[...] = jnp.zeros_like(l_ref)

        l_ref[...] += 0.5 * jnp.sum(jnp.mean(diff * diff, axis=-1, keepdims=True))

    row = pl.BlockSpec((br, d), lambda i: (i, 0))
    one = pl.BlockSpec((8, 128), lambda i: (0, 0))
    dy, l = pl.pallas_call(body, name="loss_head", grid=(r // br,), in_specs=[row, row], out_specs=[row, one],
                           out_shape=[jax.ShapeDtypeStruct((r, d), F32), jax.ShapeDtypeStruct((8, 128), F32)],
                           compiler_params=_params())(y, target)
    return l[0, 0], dy


def _adamw(name, parts, branch, w, m, v, layer, prev, after):
    _, r, c = w.shape
    p = parts.shape[0]
    br = r
    for cand in (512, 256, 128, 64, 32, 16):
        if r % cand == 0 and cand * c * 4 <= 2 * 1024 * 1024:
            br = cand
            break
    bc1 = 1.0 - ADAM_B1 ** ADAM_STEP
    bc2 = 1.0 - ADAM_B2 ** ADAM_STEP
    if prev is None:
        prev = [lax.empty(w.shape, F32) for _ in range(4)]

    def body(p_ref, w_ref, m_ref, v_ref, after_ref, pg, pd, pm, pv, g_out, d_out, m_out, v_out):
        g = p_ref[0].astype(F32)
        for s in range(1, p):
            g = g + p_ref[s].astype(F32)
        m2 = ADAM_B1 * m_ref[...] + (1.0 - ADAM_B1) * g
        v2 = ADAM_B2 * v_ref[...] + (1.0 - ADAM_B2) * (g * g)
        m_hat = m2 / bc1
        v_hat = v2 / bc2
        g_out[...] = g
        d_out[...] = -ADAM_LR * (m_hat / (jnp.sqrt(v_hat) + ADAM_EPS) + ADAM_WD * w_ref[...])
        m_out[...] = m2
        v_out[...] = v2

    if branch is None:
        p_spec = pl.BlockSpec((p, br, c), lambda i: (0, i, 0))
    else:
        p_spec = pl.BlockSpec((p, None, br, c), lambda i: (0, branch, i, 0))
    slab = pl.BlockSpec((None, br, c), lambda i: (layer, i, 0))
    o = jax.ShapeDtypeStruct(w.shape, F32)
    return pl.pallas_call(body, name=name, grid=(r // br,),
                          in_specs=[p_spec, slab, slab, slab, _UNTOUCHED] + [_UNTOUCHED] * 4,
                          out_specs=[slab] * 4, out_shape=[o] * 4, input_output_aliases={5: 0, 6: 1, 7: 2, 8: 3},
                          compiler_params=_params())(parts, w, m, v, after, *prev)


def _layer_norm_parts(a1, ln_g, ln_b):
    mu = jnp.mean(a1, axis=-1, keepdims=True)
    xc = a1 - mu
    rstd = lax.rsqrt(jnp.mean(xc * xc, axis=-1, keepdims=True) + EPS)
    xhat = xc * rstd
    a2 = xhat * ln_g + ln_b
    return xhat, rstd, a2


def _softmax_rows(s):
    e = jnp.exp(s - jnp.max(s, axis=-1, keepdims=True))
    return e / jnp.sum(e, axis=-1, keepdims=True)


SUBLANES = 8
CONV_ROWS = 32


def _shift_copies(ext_ref, sh_ref, rows):
    for r in range(1, SUBLANES):
        sh_ref[r - 1, pl.ds(0, rows), :] = ext_ref[pl.ds(r, rows), :]


def _fill_tap_tiles(taps_ref, tile_ref):
    @pl.when(pl.program_id(0) == 0)
    def _():
        for k in range(CONV_A_K):
            tile_ref[k] = jnp.broadcast_to(taps_ref[pl.ds(k, 1), :], tile_ref.shape[1:])


def _rows_at(ext_ref, sh_ref, offset, ts):
    q, r = divmod(offset, SUBLANES)
    if r == 0:
        return ext_ref[pl.ds(SUBLANES * q, ts), :]
    return sh_ref[r - 1, pl.ds(SUBLANES * q, ts), :]


def _branch_specs(ts, c):
    def col(ci):
        return pl.BlockSpec((ts, c), lambda i: (i, ci))

    def prev(ci, h):
        return pl.BlockSpec((h, c), lambda i: (jnp.maximum(i * (ts // h) - 1, 0), ci))

    return col, prev


def _branch_fwd(proj, kv, conv_a_w, conv_a_b, ln_g, ln_b, conv_b_w):
    s = proj.shape[0]
    mlen, c2 = kv.shape
    c = c2 // 2
    hd = c // N_HEADS
    ts = _tile(s, TILE_ROWS_BRANCH)
    scale = hd ** -0.5
    col, prev = _branch_specs(ts, c)

    def body(av, ag, sb, sc, sx, q, hav, hag, hsc, hsx, kv_ref, caw, cab, lng, lnb, cbw, a1_ref, abo_ref, exta, extb,
             sha, wtile):
        not_first = (pl.program_id(0) > 0).astype(F32)
        _fill_tap_tiles(caw, wtile)
        exta[pl.ds(0, HALO_A), :] = hav[...] * jax.nn.sigmoid(hag[...]) * not_first
        exta[pl.ds(HALO_A, ts), :] = av[...] * jax.nn.sigmoid(ag[...])
        _shift_copies(exta, sha, ts + HALO_A - SUBLANES)
        chunk = (CONV_ROWS // SUBLANES, SUBLANES, c)
        for r0 in range(0, ts, CONV_ROWS):
            acc = jnp.broadcast_to(cab[...], chunk)
            for k in range(CONV_A_K):
                rows = _rows_at(exta, sha, HALO_A - (CONV_A_K - 1) + k + r0, CONV_ROWS)
                acc = acc + wtile[k][None] * rows.reshape(chunk)
            a1_ref[pl.ds(r0, CONV_ROWS), :] = acc.reshape(CONV_ROWS, c)
        _, _, a2 = _layer_norm_parts(a1_ref[...], lng[...], lnb[...])
        abo_ref[0] = (a2 * jax.nn.sigmoid(a2)).astype(BF16)
        extb[pl.ds(0, HALO_B), :] = hsc[...] * hsx[...] * not_first
        extb[pl.ds(HALO_B, ts), :] = sc[...] * sx[...]
        u = cbw[pl.ds(0, 1), :] * extb[pl.ds(HALO_B - (CONV_B_K - 1), ts), :]
        for k in range(1, CONV_B_K):
            u = u + cbw[pl.ds(k, 1), :] * extb[pl.ds(HALO_B - (CONV_B_K - 1) + k, ts), :]
        abo_ref[1] = (sb[...] * u).astype(BF16)
        for h in range(N_HEADS):
            qh = q[:, h * hd:(h + 1) * hd].astype(BF16)
            kh = kv_ref[:, h * hd:(h + 1) * hd]
            vh = kv_ref[:, c + h * hd:c + (h + 1) * hd]
            sc_ = lax.dot_general(qh, kh, (((1,), (1,)), ((), ())), preferred_element_type=F32) * scale
            p = _softmax_rows(sc_).astype(BF16)
            abo_ref[2, :, h * hd:(h + 1) * hd] = jnp.dot(p, vh, preferred_element_type=F32).astype(BF16)

    full = lambda shp: pl.BlockSpec(shp, lambda i: (0,) * len(shp))
    return pl.pallas_call(
        body, name="branch_fwd", grid=(s // ts,),
        in_specs=[col(0), col(1), col(2), col(3), col(4), col(5), prev(0, HALO_A), prev(1, HALO_A),
                  prev(3, HALO_B), prev(4, HALO_B), full((mlen, c2)), full(conv_a_w.shape), full((1, c)),
                  full((1, c)), full((1, c)), full(conv_b_w.shape)],
        out_specs=[pl.BlockSpec((ts, c), lambda i: (i, 0)), pl.BlockSpec((N_BRANCH, ts, c), lambda i: (0, i, 0))],
        out_shape=[jax.ShapeDtypeStruct((s, c), F32), jax.ShapeDtypeStruct((N_BRANCH, s, c), BF16)],
        scratch_shapes=[pltpu.VMEM((HALO_A + ts, c), F32), pltpu.VMEM((HALO_B + ts, c), F32),
                        pltpu.VMEM((SUBLANES - 1, ts + HALO_A - SUBLANES, c), F32),
                        pltpu.VMEM((CONV_A_K, SUBLANES, c), F32)],
        compiler_params=_params(),
    )(proj, proj, proj, proj, proj, proj, proj, proj, proj, proj, kv, conv_a_w, conv_a_b, ln_g, ln_b, conv_b_w)


def _branch_bwd(proj, a1, dabo, kv, conv_a_w, ln_g, ln_b, conv_b_w, dproj):
    s = proj.shape[0]
    mlen, c2 = kv.shape
    c = c2 // 2
    hd = c // N_HEADS
    ts = _tile(s, TILE_ROWS_BRANCH_BWD)
    nt = s // ts
    scale = hd ** -0.5
    col, prev = _branch_specs(ts, c)

    def nxt(h, lead=None, ci=0):
        if lead is None:
            return pl.BlockSpec((h, c), lambda i: (jnp.minimum((i + 1) * (ts // h), s // h - 1), ci))
        return pl.BlockSpec((None, h, c), lambda i: (lead, jnp.minimum((i + 1) * (ts // h), s // h - 1), 0))

    def body(av, ag, sb, sc, sx, q, hav, hag, hsc, hsx, nsb, a1_ref, na1, dabo_ref, nda, ndb, kv_ref, caw, lng, lnb,
             cbw, dproj_in, dp_ref, dkv_ref, dwa_ref, misc_ref, exta, extda, extb, extdu, sha, shda, wtile):
        i = pl.program_id(0)
        _fill_tap_tiles(caw, wtile)
        not_first = (i > 0).astype(F32)
        not_last = (i < nt - 1).astype(F32)

        @pl.when(i == 0)
        def _():
            dkv_ref[...] = jnp.zeros_like(dkv_ref)
            dwa_ref[...] = jnp.zeros_like(dwa_ref)
            misc_ref[...] = jnp.zeros_like(misc_ref)

        def rowsum(t):
            return jnp.sum(t, axis=0, keepdims=True)

        def da1_of(a1v, da3):
            xhat, rstd, a2 = _layer_norm_parts(a1v, lng[...], lnb[...])
            sg = jax.nn.sigmoid(a2)
            da2 = da3 * (sg * (1.0 + a2 * (1.0 - sg)))
            dxh = da2 * lng[...]
            da1 = rstd * (dxh - jnp.mean(dxh, axis=-1, keepdims=True)
                          - xhat * jnp.mean(dxh * xhat, axis=-1, keepdims=True))
            return da1, da2, xhat

        da1, da2, xhat = da1_of(a1_ref[...], dabo_ref[0])
        misc_ref[pl.ds(0, 1), :] += rowsum(da1)
        misc_ref[pl.ds(1, 1), :] += rowsum(da2 * xhat)
        misc_ref[pl.ds(2, 1), :] += rowsum(da2)
        extda[pl.ds(0, ts), :] = da1
        extda[pl.ds(ts, HALO_A), :] = da1_of(na1[...], nda[...])[0] * not_last
        sga = jax.nn.sigmoid(ag[...])
        exta[pl.ds(0, HALO_A), :] = hav[...] * jax.nn.sigmoid(hag[...]) * not_first
        exta[pl.ds(HALO_A, ts), :] = av[...] * sga
        _shift_copies(exta, sha, ts + HALO_A - SUBLANES)
        _shift_copies(extda, shda, ts + HALO_A - SUBLANES)
        for r0 in range(0, ts, CONV_ROWS):
            rows = pl.ds(r0, CONV_ROWS)
            da1_rows = extda[rows, :]
            chunk = (CONV_ROWS // SUBLANES, SUBLANES, c)
            da0 = jnp.zeros(chunk, F32)
            for k in range(CONV_A_K):
                tap = da1_rows * _rows_at(exta, sha, HALO_A - (CONV_A_K - 1) + k + r0, CONV_ROWS)
                dwa_ref[pl.ds(SUBLANES * k, SUBLANES), :] += tap.reshape(chunk).sum(axis=0)
                da0 = da0 + wtile[CONV_A_K - 1 - k][None] * _rows_at(extda, shda, k + r0, CONV_ROWS).reshape(chunk)
            da0 = da0.reshape(CONV_ROWS, c)
            sg = sga[r0:r0 + CONV_ROWS]
            dp_ref[rows, 0:c] = (da0 * sg).astype(BF16)
            dp_ref[rows, c:2 * c] = (da0 * av[rows, :] * sg * (1.0 - sg)).astype(BF16)

        extb[pl.ds(0, HALO_B), :] = hsc[...] * hsx[...] * not_first
        extb[pl.ds(HALO_B, ts), :] = sc[...] * sx[...]
        dbu = dabo_ref[1]
        du = dbu * sb[...]
        extdu[pl.ds(0, ts), :] = du
        extdu[pl.ds(ts, HALO_B), :] = ndb[...] * nsb[...] * not_last
        u = jnp.zeros((ts, c), F32)
        dpr = jnp.zeros((ts, c), F32)
        for k in range(CONV_B_K):
            shifted = extb[pl.ds(HALO_B - (CONV_B_K - 1) + k, ts), :]
            u = u + cbw[pl.ds(k, 1), :] * shifted
            misc_ref[pl.ds(3 + k, 1), :] += rowsum(du * shifted)
            dpr = dpr + cbw[pl.ds(CONV_B_K - 1 - k, 1), :] * extdu[pl.ds(k, ts), :]
        dp_ref[:, 2 * c:3 * c] = (dbu * u).astype(BF16)
        dp_ref[:, 3 * c:4 * c] = (dpr * sx[...]).astype(BF16)
        dp_ref[:, 4 * c:5 * c] = (dpr * sc[...]).astype(BF16)

        nt_dims = (((1,), (1,)), ((), ()))
        tn_dims = (((0,), (0,)), ((), ()))
        for h in range(N_HEADS):
            lo, hi = h * hd, (h + 1) * hd
            qh = q[:, lo:hi].astype(BF16)
            kh = kv_ref[:, lo:hi]
            vh = kv_ref[:, c + lo:c + hi]
            p = _softmax_rows(lax.dot_general(qh, kh, nt_dims, preferred_element_type=F32) * scale)
            pb = p.astype(BF16)
            doh = dabo_ref[2, :, lo:hi].astype(BF16)
            dpm = lax.dot_general(doh, vh, nt_dims, preferred_element_type=F32)
            ds = (p * (dpm - jnp.sum(dpm * p, axis=-1, keepdims=True)) * scale).astype(BF16)
            dp_ref[:, 5 * c + lo:5 * c + hi] = jnp.dot(ds, kh, preferred_element_type=F32).astype(BF16)
            dkv_ref[:, lo:hi] += lax.dot_general(ds, qh, tn_dims, preferred_element_type=F32)
            dkv_ref[:, c + lo:c + hi] += lax.dot_general(pb, doh, tn_dims, preferred_element_type=F32)

    full = lambda shp: pl.BlockSpec(shp, lambda i: (0,) * len(shp))
    n_in_before_dproj = 21
    shifted = pltpu.VMEM((SUBLANES - 1, ts + HALO_A - SUBLANES, c), F32)
    dp, dkv, dwa, misc = pl.pallas_call(
        body, name="branch_bwd", grid=(nt,),
        in_specs=[col(0), col(1), col(2), col(3), col(4), col(5), prev(0, HALO_A), prev(1, HALO_A),
                  prev(3, HALO_B), prev(4, HALO_B), nxt(HALO_B, ci=2),
                  pl.BlockSpec((ts, c), lambda i: (i, 0)), nxt(HALO_A),
                  pl.BlockSpec((N_BRANCH, ts, c), lambda i: (0, i, 0)), nxt(HALO_A, lead=0), nxt(HALO_B, lead=1),
                  full((mlen, c2)), full(conv_a_w.shape), full((1, c)), full((1, c)), full(conv_b_w.shape),
                  pl.BlockSpec(memory_space=pl.ANY)],
        out_specs=[pl.BlockSpec((ts, 6 * c), lambda i: (i, 0)), full((mlen, c2)), full((HALO_A * SUBLANES, c)),
                   full((8, c))],
        out_shape=[jax.ShapeDtypeStruct(dproj.shape, BF16), jax.ShapeDtypeStruct((mlen, c2), F32),
                   jax.ShapeDtypeStruct((HALO_A * SUBLANES, c), F32), jax.ShapeDtypeStruct((8, c), F32)],
        scratch_shapes=[pltpu.VMEM((HALO_A + ts, c), F32), pltpu.VMEM((ts + HALO_A, c), F32),
                        pltpu.VMEM((HALO_B + ts, c), F32), pltpu.VMEM((ts + HALO_B, c), F32), shifted, shifted,
                        pltpu.VMEM((CONV_A_K, SUBLANES, c), F32)],
        input_output_aliases={n_in_before_dproj: 0},
        compiler_params=_params(),
    )(proj, proj, proj, proj, proj, proj, proj, proj, proj, proj, proj, a1, a1, dabo, dabo, dabo, kv, conv_a_w,
      ln_g, ln_b, conv_b_w, dproj)
    return dp, dkv, dwa.reshape(HALO_A, SUBLANES, c).sum(axis=1), misc


def _merge_fwd(abo, w_cat, proj, d, after):
    nb, s, c = abo.shape
    bm, bn = _tile(s, TILE_M), _tile(d, TILE_N)
    gate_col0 = (proj.shape[1] - nb * d) // bn

    def body(a_ref, w_ref, g_ref, after_ref, y_ref, m_ref, acc_ref):
        k = pl.program_id(2)
        y = jnp.dot(a_ref[...], w_ref[...], preferred_element_type=F32)
        y_ref[...] = y.astype(BF16)
        contrib = jax.nn.sigmoid(g_ref[...]) * y

        @pl.when(k == 0)
        def _():
            acc_ref[...] = contrib

        @pl.when(k > 0)
        def _():
            acc_ref[...] += contrib

        @pl.when(k == nb - 1)
        def _():
            m_ref[...] = acc_ref[...].astype(BF16)

    return pl.pallas_call(
        body, name="merge_fwd", grid=(s // bm, d // bn, nb),
        in_specs=[pl.BlockSpec((None, bm, c), lambda i, j, k: (k, i, 0)),
                  pl.BlockSpec((None, c, bn), lambda i, j, k: (k, 0, j)),
                  pl.BlockSpec((bm, bn), lambda i, j, k: (i, gate_col0 + k * (d // bn) + j)), _UNTOUCHED],
        out_specs=[pl.BlockSpec((None, bm, bn), lambda i, j, k: (k, i, j)),
                   pl.BlockSpec((bm, bn), lambda i, j, k: (i, j))],
        out_shape=[jax.ShapeDtypeStruct((nb, s, d), BF16), jax.ShapeDtypeStruct((s, d), BF16)],
        scratch_shapes=[pltpu.VMEM((bm, bn), F32)],
        compiler_params=_params(),
    )(abo, w_cat, proj, after)


def _merge_bwd(dz, w_o, y, proj):
    s, d = dz.shape
    nb = y.shape[0]
    nin = proj.shape[1]
    bm, bn = _tile(s, TILE_M), _tile(d, TILE_N)
    gate_col0 = (nin - nb * d) // bn

    def body(dz_ref, w_ref, y_ref, g_ref, dy_ref, dg_ref, acc_ref):
        @pl.when(pl.program_id(2) == 0)
        def _():
            acc_ref[...] = lax.dot_general(dz_ref[...], w_ref[...], (((1,), (1,)), ((), ())),
                                           preferred_element_type=F32)

        dm = acc_ref[...]
        gt = jax.nn.sigmoid(g_ref[...])
        dy_ref[...] = (dm * gt).astype(BF16)
        dg_ref[...] = (dm * y_ref[...].astype(F32) * gt * (1.0 - gt)).astype(BF16)

    gate = lambda i, j, k: (i, gate_col0 + k * (d // bn) + j)
    return pl.pallas_call(
        body, name="merge_bwd", grid=(s // bm, d // bn, nb),
        in_specs=[pl.BlockSpec((bm, d), lambda i, j, k: (i, 0)), pl.BlockSpec((bn, d), lambda i, j, k: (j, 0)),
                  pl.BlockSpec((None, bm, bn), lambda i, j, k: (k, i, j)), pl.BlockSpec((bm, bn), gate)],
        out_specs=[pl.BlockSpec((None, bm, bn), lambda i, j, k: (k, i, j)), pl.BlockSpec((bm, bn), gate)],
        out_shape=[jax.ShapeDtypeStruct((nb, s, d), BF16), jax.ShapeDtypeStruct((s, nin), BF16)],
        scratch_shapes=[pltpu.VMEM((bm, bn), F32)],
        compiler_params=_params(),
    )(dz, w_o, y, proj)


def _window(ref, axis, who, length):
    idx = [slice(None)] * len(ref.shape)
    idx[axis] = pl.ds(pl.multiple_of(who * length, length), length)
    return ref.at[tuple(idx)]


def _all_gather(name, shards, axes, after):
    n = len(shards)
    n_copies = 7
    out_shapes = []
    for sh, ax in zip(shards, axes):
        shp = list(sh.shape)
        shp[ax] *= N_DEV
        out_shapes.append(jax.ShapeDtypeStruct(tuple(shp), sh.dtype))

    def body(*refs):
        ins = refs[:n]
        outs = refs[n + len(after):2 * n + len(after)]
        send, recv, local = refs[2 * n + len(after):]
        x, y, c = lax.axis_index("x"), lax.axis_index("y"), lax.axis_index("c")
        sibling = (x, y, 1 - c)
        chips = [(1 - x, y), (x, 1 - y), (1 - x, 1 - y)]
        lens = [ins[a].shape[axes[a]] for a in range(n)]

        def block(a, px, py, pc):
            return _window(outs[a], axes[a], 4 * px + 2 * py + pc, lens[a])

        def copy(a, k, src, dst, to):
            return pltpu.make_async_remote_copy(src_ref=src, dst_ref=dst, send_sem=send.at[a, k],
                                                recv_sem=recv.at[a, k], device_id=to,
                                                device_id_type=pl.DeviceIdType.MESH)

        mine = [pltpu.make_async_copy(ins[a], block(a, x, y, c), local.at[a]) for a in range(n)]
        for cp in mine:
            cp.start()
        started = []
        for j, chip in enumerate(chips):
            for a in range(n):
                started.append(copy(a, 1 + j, ins[a], block(a, x, y, c), (*chip, c)))
                started[-1].start()
        for a in range(n):
            started.append(copy(a, 0, ins[a], block(a, x, y, c), sibling))
            started[-1].start()
        for j, chip in enumerate(chips):
            for a in range(n):
                landed = block(a, *chip, c)
                copy(a, 1 + j, landed, landed, (*chip, c)).wait_recv()
                started.append(copy(a, 4 + j, landed, landed, sibling))
                started[-1].start()
        for a in range(n):
            copy(a, 0, ins[a], block(a, x, y, 1 - c), sibling).wait_recv()
            for j, chip in enumerate(chips):
                copy(a, 4 + j, ins[a], block(a, *chip, 1 - c), sibling).wait_recv()
        for cp in started:
            cp.wait_send()
        for cp in mine:
            cp.wait()

    hbm = pl.BlockSpec(memory_space=pl.ANY)
    return pl.pallas_call(
        body, name=name, in_specs=[hbm] * (n + len(after)), out_specs=[hbm] * n, out_shape=out_shapes,
        scratch_shapes=[pltpu.SemaphoreType.DMA((n, n_copies)), pltpu.SemaphoreType.DMA((n, n_copies)),
                        pltpu.SemaphoreType.DMA((n,))],
    )(*shards, *after)


def _pair(a, d):
    return a * N_DEV + d


def _push_ends(kind, src, land, axis, length, me, to):
    if kind == "gather":
        return src, _window(land, axis, me, length)
    return _window(src, axis, to, length), land.at[me]


def _arrival_ends(kind, src, land, axis, length, me, frm):
    if kind == "gather":
        return src, _window(land, axis, frm, length)
    return _window(src, axis, me, length), land.at[frm]


def _exchange_start(name, kind, srcs, axes, after):
    n = len(srcs)
    if kind == "gather":
        lens = [s.shape[ax] for s, ax in zip(srcs, axes)]
        land_shapes = [s.shape[:ax] + (s.shape[ax] * N_DEV,) + s.shape[ax + 1:] for s, ax in zip(srcs, axes)]
    else:
        lens = [s.shape[ax] // N_DEV for s, ax in zip(srcs, axes)]
        land_shapes = [(N_DEV,) + s.shape[:ax] + (ln,) + s.shape[ax + 1:] for s, ax, ln in zip(srcs, axes, lens)]
    lands = [lax.empty(shp, s.dtype) for shp, s in zip(land_shapes, srcs)]

    def body(*refs):
        ins = refs[:n]
        send, recv = refs[2 * n + 1], refs[2 * n + 2]
        lnd = refs[2 * n + 3 + n:2 * n + 3 + 2 * n]
        token = refs[-1]
        me = _my_index()
        for a in range(n):
            for d in range(1, N_DEV):
                to = (me + d) % N_DEV
                src, dst = _push_ends(kind, ins[a], lnd[a], axes[a], lens[a], me, to)
                pltpu.make_async_remote_copy(src_ref=src, dst_ref=dst, send_sem=send.at[_pair(a, d)],
                                             recv_sem=recv.at[_pair(a, d)], device_id=_mesh_id(to),
                                             device_id_type=pl.DeviceIdType.MESH).start()
        for a in range(n):
            pltpu.make_async_copy(*_push_ends(kind, ins[a], lnd[a], axes[a], lens[a], me, me),
                                  send.at[_pair(a, 0)]).start()
        token[...] = jnp.zeros_like(token)

    hbm = pl.BlockSpec(memory_space=pltpu.HBM)
    sem = pl.BlockSpec(memory_space=pltpu.SEMAPHORE)
    held = [pltpu.with_memory_space_constraint(t, pltpu.HBM) for t in list(srcs) + lands]
    outs = pl.pallas_call(
        body, name=name,
        in_specs=[hbm] * (2 * n) + [_UNTOUCHED],
        out_specs=[sem, sem] + [hbm] * (2 * n) + [pl.BlockSpec(memory_space=pltpu.VMEM)],
        out_shape=[pltpu.SemaphoreType.DMA((n * N_DEV,)), pltpu.SemaphoreType.DMA((n * N_DEV,))]
        + [pltpu.HBM(t.shape, t.dtype) for t in held] + [jax.ShapeDtypeStruct((8, 128), F32)],
        input_output_aliases={i: 2 + i for i in range(2 * n)},
        compiler_params=pltpu.CompilerParams(has_side_effects=pltpu.SideEffectType.DATAFLOW_SIDE_EFFECTING),
    )(*held, after)
    handle = dict(kind=kind, axes=axes, lens=lens, send=outs[0], recv=outs[1], srcs=outs[2:2 + n],
                  lands=outs[2 + n:2 + 2 * n])
    return handle, outs[-1]


def _exchange_wait(name, handle, after):
    kind, axes, lens = handle["kind"], handle["axes"], handle["lens"]
    srcs, lands = handle["srcs"], handle["lands"]
    n = len(srcs)
    n_in = 2 * n + 2 + len(after)

    def body(*refs):
        ins = refs[:n]
        send, recv = refs[2 * n], refs[2 * n + 1]
        got = refs[n_in + n:n_in + 2 * n]
        me = _my_index()
        for d in range(1, N_DEV):
            frm = (me + N_DEV - d) % N_DEV
            for a in range(n):
                src, dst = _arrival_ends(kind, ins[a], got[a], axes[a], lens[a], me, frm)
                pltpu.make_async_remote_copy(src_ref=src, dst_ref=dst, send_sem=send.at[_pair(a, d)],
                                             recv_sem=recv.at[_pair(a, d)], device_id=_mesh_id(frm),
                                             device_id_type=pl.DeviceIdType.MESH).wait_recv()
        for d in range(1, N_DEV):
            to = (me + d) % N_DEV
            for a in range(n):
                src, dst = _push_ends(kind, ins[a], got[a], axes[a], lens[a], me, to)
                pltpu.make_async_remote_copy(src_ref=src, dst_ref=dst, send_sem=send.at[_pair(a, d)],
                                             recv_sem=recv.at[_pair(a, d)], device_id=_mesh_id(to),
                                             device_id_type=pl.DeviceIdType.MESH).wait_send()
        for a in range(n):
            pltpu.make_async_copy(*_push_ends(kind, ins[a], got[a], axes[a], lens[a], me, me),
                                  send.at[_pair(a, 0)]).wait()

    hbm = pl.BlockSpec(memory_space=pltpu.HBM)
    sem = pl.BlockSpec(memory_space=pltpu.SEMAPHORE)
    outs = pl.pallas_call(
        body, name=name,
        in_specs=[hbm] * (2 * n) + [sem, sem] + [_UNTOUCHED] * len(after),
        out_specs=[hbm] * (2 * n),
        out_shape=[pltpu.HBM(t.shape, t.dtype) for t in list(srcs) + list(lands)],
        input_output_aliases={i: i for i in range(2 * n)},
        compiler_params=pltpu.CompilerParams(has_side_effects=pltpu.SideEffectType.DATAFLOW_SIDE_EFFECTING),
    )(*srcs, *lands, handle["send"], handle["recv"], *after)
    return outs[n:]


def _all_reduce_small(packed, after):
    r, c = packed.shape

    def body(p_ref, *rest):
        o_ref, buf, send, recv = rest[len(after):]
        me = _my_index()
        buf[me] = p_ref[...]
        pushes = []
        for d in range(1, N_DEV):
            to = (me + d) % N_DEV
            cp = pltpu.make_async_remote_copy(src_ref=p_ref, dst_ref=buf.at[me], send_sem=send.at[d],
                                              recv_sem=recv.at[d], device_id=_mesh_id(to),
                                              device_id_type=pl.DeviceIdType.MESH)
            cp.start()
            pushes.append(cp)
        for d in range(1, N_DEV):
            frm = (me + N_DEV - d) % N_DEV
            pltpu.make_async_remote_copy(src_ref=p_ref, dst_ref=buf.at[frm], send_sem=send.at[d], recv_sem=recv.at[d],
                                         device_id=_mesh_id(frm), device_id_type=pl.DeviceIdType.MESH).wait_recv()
        for cp in pushes:
            cp.wait_send()
        acc = buf[0]
        for s in range(1, N_DEV):
            acc = acc + buf[s]
        o_ref[...] = acc

    vmem = pl.BlockSpec(memory_space=pltpu.VMEM)
    return pl.pallas_call(
        body, name="all_reduce_small", in_specs=[vmem] + [_UNTOUCHED] * len(after), out_specs=vmem,
        out_shape=jax.ShapeDtypeStruct((r, c), F32),
        scratch_shapes=[pltpu.VMEM((N_DEV, r, c), F32), pltpu.SemaphoreType.DMA((N_DEV,)),
                        pltpu.SemaphoreType.DMA((N_DEV,))],
        compiler_params=_params(),
    )(packed, *after)


def _layer_fwd_branches(x, h, mem, sm, wf):
    proj = _mm_nn("proj", h, wf["w_in"], F32)
    mem_n = _rmsnorm("rms_mem", mem, sm["g_mem"], mem)
    kv = _mm_nn("kv", mem_n, wf["w_kv"], BF16)
    a1, abo = _branch_fwd(proj, kv, wf["conv_a_w"], sm["conv_a_b"], sm["ln_a_g"], sm["ln_a_b"], wf["conv_b_w"])
    return dict(x=x, h=h, proj=proj, mem_n=mem_n, kv=kv, a1=a1, abo=abo)


def _layer_fwd_rest(sm, wf, sv, after, g_next):
    x = sv["x"]
    y, merged = _merge_fwd(sv["abo"], wf["w_cat"], sv["proj"], x.shape[1], after)
    z = _mm_nn("mix_out", merged, wf["w_o"], F32)
    x1, h2 = _res_norm("mix_out_norm", z, x, sm["g_mix_post"], sm["g_mlp_pre"])
    r, act = _mm_relu2("mlp_up", h2, wf["w_up"])
    f = _mm_nn("mlp_down", act, wf["w_down"], F32)
    x2, h_next = _res_norm("mlp_down_norm", f, x1, sm["g_mlp_post"], g_next)
    sv.update(y=y, merged=merged, z=z, x1=x1, h2=h2, r=r, act=act, f=f)
    return x2, h_next


def _pad_rows(t):
    return jnp.pad(t, ((0, (-t.shape[0]) % 8), (0, 0)))


def _layer_bwd(dx2, df, mem, sm, wf, sv, send_off, below):
    c = sv["a1"].shape[1]
    d_up = _mm_relu2_bwd("mlp_down_bwd", df, wf["w_down"], sv["r"])
    gw_down = _mm_tn("gw_down", sv["act"], df, BF16)
    dh2 = _mm_nt("mlp_up_bwd", d_up, wf["w_up"], F32)
    gw_up = _mm_tn("gw_up", sv["h2"], d_up, BF16)
    sent = send_off(dict(w_up=gw_up, w_down=gw_down))
    dx1, dg_mlp_pre, dz, dg_mix_post = _rms_bwd("rms_bwd_mlp_pre", dh2, sv["x1"], sm["g_mlp_pre"], F32, sent,
                                                residual=dx2, below=(sv["z"], sm["g_mix_post"]))
    dy, dproj = _merge_bwd(dz, wf["w_o"], sv["y"], sv["proj"])
    gw_o = _mm_tn("gw_o", sv["merged"], dz, BF16)
    dabo = _mm_nt("branch_out_bwd", dy, wf["w_cat"], F32)
    gw_cat = _mm_tn("gw_branch_out", sv["abo"], dy, BF16)
    dproj, dkv, dconv_a_w, misc = _branch_bwd(sv["proj"], sv["a1"], dabo, sv["kv"], wf["conv_a_w"], sm["ln_a_g"],
                                              sm["ln_a_b"], wf["conv_b_w"], dproj)
    dkv = dkv.astype(BF16)
    gw_kv = _mm_tn("gw_kv", sv["mem_n"], dkv, BF16)
    dmem_n = _mm_nt("kv_bwd", dkv, wf["w_kv"], F32)
    _, dg_mem = _rms_bwd("rms_bwd_mem", dmem_n, mem, sm["g_mem"], BF16, dmem_n)
    gw_in = _mm_tn("gw_in", sv["h"], dproj, BF16)
    sent = send_off(dict(w_in=gw_in, w_kv=gw_kv, w_cat=gw_cat, w_o=gw_o))
    dh = _mm_nt("proj_bwd", dproj, wf["w_in"], F32, after=sent)
    dx, dg_mix_pre, *rest = _rms_bwd("rms_bwd_mix_pre", dh, sv["x"], sm["g_mix_pre"], F32, dh, residual=dx1,
                                     below=below)
    small = dict(g_mix_pre=dg_mix_pre, g_mem=dg_mem, g_mix_post=dg_mix_post, g_mlp_pre=dg_mlp_pre, misc=misc,
                 conv_a_w=dconv_a_w)
    return dx, small, rest


def _pack_small(small, c):
    rows = [_pad_rows(small[k].reshape(2, c)) for k in _SMALL_D_NAMES]
    return jnp.concatenate(rows + [small["misc"], small["conv_a_w"]], axis=0)


_SMALL_D_NAMES = ["g_mix_pre", "g_mem", "g_mix_post", "g_mlp_pre", "g_mlp_post"]
_SMALL_C_NAMES = ["conv_a_b", "ln_a_g", "ln_a_b"]
_SMALL_MISC_ROW = 8 * len(_SMALL_D_NAMES)
_SMALL_CONV_B_ROW = _SMALL_MISC_ROW + len(_SMALL_C_NAMES)
_SMALL_CONV_A_ROW = _SMALL_MISC_ROW + 8
_SMALL_ROWS = _SMALL_CONV_A_ROW + HALO_A


def kernel(x, mem, g_mix_pre, w_in, conv_a_w, conv_a_b, ln_a_g, ln_a_b, w_a_out, conv_b_w, w_b_out, g_mem, w_kv, w_x_out, w_o, g_mix_post, g_mlp_pre, w_up, w_down, g_mlp_post, loss_target, m_g_mix_pre, m_w_in, m_conv_a_w, m_conv_a_b, m_ln_a_g, m_ln_a_b, m_w_a_out, m_conv_b_w, m_w_b_out, m_g_mem, m_w_kv, m_w_x_out, m_w_o, m_g_mix_post, m_g_mlp_pre, m_w_up, m_w_down, m_g_mlp_post, v_g_mix_pre, v_w_in, v_conv_a_w, v_conv_a_b, v_ln_a_g, v_ln_a_b, v_w_a_out, v_conv_b_w, v_w_b_out, v_g_mem, v_w_kv, v_w_x_out, v_w_o, v_g_mix_post, v_g_mlp_pre, v_w_up, v_w_down, v_g_mlp_post):
    names = ["g_mix_pre", "w_in", "conv_a_w", "conv_a_b", "ln_a_g", "ln_a_b", "w_a_out", "conv_b_w", "w_b_out",
             "g_mem", "w_kv", "w_x_out", "w_o", "g_mix_post", "g_mlp_pre", "w_up", "w_down", "g_mlp_post"]
    w = dict(zip(names, [g_mix_pre, w_in, conv_a_w, conv_a_b, ln_a_g, ln_a_b, w_a_out, conv_b_w, w_b_out, g_mem,
                         w_kv, w_x_out, w_o, g_mix_post, g_mlp_pre, w_up, w_down, g_mlp_post]))
    mo = dict(zip(names, [m_g_mix_pre, m_w_in, m_conv_a_w, m_conv_a_b, m_ln_a_g, m_ln_a_b, m_w_a_out, m_conv_b_w,
                          m_w_b_out, m_g_mem, m_w_kv, m_w_x_out, m_w_o, m_g_mix_post, m_g_mlp_pre, m_w_up, m_w_down,
                          m_g_mlp_post]))
    vo = dict(zip(names, [v_g_mix_pre, v_w_in, v_conv_a_w, v_conv_a_b, v_ln_a_g, v_ln_a_b, v_w_a_out, v_conv_b_w,
                          v_w_b_out, v_g_mem, v_w_kv, v_w_x_out, v_w_o, v_g_mix_post, v_g_mlp_pre, v_w_up, v_w_down,
                          v_g_mlp_post]))
    depth = w_in.shape[0]
    c = conv_a_b.shape[1]
    cs = conv_a_w.shape[2]
    me = _my_index()
    xs, mems, tgt = x[0], mem[0], loss_target[0]

    big_names = ["w_in", "w_kv", "w_cat", "w_o", "w_up", "w_down"]
    big_axes = [1, 0, 2, 0, 1, 0]
    n_early = 2
    branch_names = ["w_a_out", "w_b_out", "w_x_out"]
    smalls = [{k: w[k][l][None, :] for k in _SMALL_D_NAMES + _SMALL_C_NAMES} for l in range(depth)]

    def weight_shards(l):
        return [w_in[l].astype(BF16), w_kv[l].astype(BF16), jnp.stack([w[k][l] for k in branch_names]).astype(BF16),
                w_o[l].astype(BF16), w_up[l].astype(BF16), w_down[l].astype(BF16)]

    def start_gather(l, after):
        shards = weight_shards(l)
        early, token = _exchange_start(f"gather_start_{l}_early", "gather", shards[:n_early], big_axes[:n_early],
                                       after)
        late, token = _exchange_start(f"gather_start_{l}_late", "gather", shards[n_early:], big_axes[n_early:], token)
        return early, late, token

    taps_a, taps_b = _all_gather(
        "gather_conv_taps",
        [jnp.pad(conv_a_w, ((0, 0), (0, HALO_A - CONV_A_K), (0, 0))),
         jnp.pad(conv_b_w, ((0, 0), (0, HALO_B - CONV_B_K), (0, 0)))], [2, 2], [])

    shards = weight_shards(0)
    first = _all_gather("gather_0_early", shards[:n_early], big_axes[:n_early], [taps_a])
    late, token = _exchange_start("gather_start_0_late", "gather", shards[n_early:], big_axes[n_early:], first[0])
    fulls = [None] * depth
    saved = []
    xc = xs
    h = _rmsnorm("rms_mix_pre", xs, smalls[0]["g_mix_pre"], token)
    for l in range(depth):
        arrived = first if l == 0 else _exchange_wait(f"gather_wait_{l}_early", early, [xc])
        fulls[l] = dict(zip(big_names[:n_early], arrived))
        fulls[l]["conv_a_w"], fulls[l]["conv_b_w"] = taps_a[l], taps_b[l]
        sv = _layer_fwd_branches(xc, h, mems, smalls[l], fulls[l])
        fulls[l].update(zip(big_names[n_early:], _exchange_wait(f"gather_wait_{l}_late", late, [sv["abo"]])))
        token = fulls[l]["w_o"]
        if l + 1 < depth:
            early, late, token = start_gather(l + 1, token)
        xc, h = _layer_fwd_rest(smalls[l], fulls[l], sv, token,
                                smalls[l + 1]["g_mix_pre"] if l + 1 < depth else None)
        saved.append(sv)
    loss_part, dx = _loss_head(xc, tgt)
    loss = lax.psum(loss_part, MESH_AXES)

    upd = {}

    def adamw_group(l, group, slots, after):
        for k, sl in zip(group, slots):
            if k == "w_cat":
                for b, nm in enumerate(branch_names):
                    upd[nm] = _adamw("adamw_" + nm, sl, b, w[nm], mo[nm], vo[nm], l, upd.get(nm), after)
            else:
                upd[k] = _adamw("adamw_" + k, sl, None, w[k], mo[k], vo[k], l, upd.get(k), after)

    in_flight = [None]

    def take_in(after):
        handle, l, group = in_flight[0]
        return _exchange_wait(f"scatter_wait_{l}_{group[0]}", handle, after), l, group

    def send_off_layer(l):
        def send_off(grads):
            group = [k for k in big_names if k in grads]
            arrays = [grads[k] for k in group]
            arrived = None if in_flight[0] is None else take_in([arrays[0]])
            handle, token = _exchange_start(f"scatter_start_{l}_{group[0]}", "scatter", arrays,
                                            [big_axes[big_names.index(k)] for k in group],
                                            arrays[0] if arrived is None else arrived[0][0])
            in_flight[0] = (handle, l, group)
            if arrived is not None:
                adamw_group(arrived[1], arrived[2], arrived[0], token)
            return token
        return send_off

    small_g = [dict() for _ in range(depth)]
    df, small_g[depth - 1]["g_mlp_post"] = _rms_bwd("rms_bwd_top", dx, saved[depth - 1]["f"],
                                                    smalls[depth - 1]["g_mlp_post"], BF16, dx)
    for l in reversed(range(depth)):
        below = (saved[l - 1]["f"], smalls[l - 1]["g_mlp_post"]) if l > 0 else None
        dx, small, rest = _layer_bwd(dx, df, mems, smalls[l], fulls[l], saved[l], send_off_layer(l), below)
        small_g[l].update(small)
        if l > 0:
            df, small_g[l - 1]["g_mlp_post"] = rest
    small_parts = [_pack_small(small_g[l], c) for l in range(depth)]

    slots, l, group = take_in([u[0] for u in upd.values()] + [dx])
    adamw_group(l, group, slots, dx)

    tot = _all_reduce_small(jnp.concatenate(small_parts, axis=0), [upd[k][0] for k in group if k in upd])
    tot = tot.reshape(depth, _SMALL_ROWS, c)
    g_small = {}
    for n_, nm in enumerate(_SMALL_D_NAMES):
        g_small[nm] = tot[:, 8 * n_:8 * n_ + 2, :].reshape(depth, 2 * c)
    for n_, nm in enumerate(_SMALL_C_NAMES):
        g_small[nm] = tot[:, _SMALL_MISC_ROW + n_, :]
    g_taps = {"conv_a_w": tot[:, _SMALL_CONV_A_ROW:_SMALL_CONV_A_ROW + CONV_A_K, :],
              "conv_b_w": tot[:, _SMALL_CONV_B_ROW:_SMALL_CONV_B_ROW + CONV_B_K, :]}

    def pack(t):
        return jnp.concatenate([_pad_rows(t[nm].reshape(-1, c)) for nm in _SMALL_D_NAMES + _SMALL_C_NAMES], axis=0)

    res = _adamw("adamw_small", pack(g_small)[None], None, pack(w)[None], pack(mo)[None], pack(vo)[None], 0, None, tot)
    row = 0
    for nm in _SMALL_D_NAMES + _SMALL_C_NAMES:
        n_rows = w[nm].size // c
        upd[nm] = [t[0, row:row + n_rows].reshape(w[nm].shape) for t in res]
        row += n_rows + (-n_rows) % 8
    for nm, g in g_taps.items():
        shp = w[nm].shape
        mine = lax.dynamic_slice_in_dim(g, me * cs, cs, axis=2).reshape(1, -1, cs)
        res = _adamw("adamw_" + nm, mine, None, w[nm].reshape(1, -1, cs), mo[nm].reshape(1, -1, cs),
                     vo[nm].reshape(1, -1, cs), 0, None, tot)
        upd[nm] = [t.reshape(shp) for t in res]

    return (loss, dx[None], *[upd[nm][0] for nm in names], *[upd[nm][1] for nm in names],
            *[upd[nm][2] for nm in names], *[upd[nm][3] for nm in names])
```

```python
import jax
import jax.numpy as jnp
from jax import lax
from jax.experimental import pallas as pl
from jax.experimental.pallas import tpu as pltpu

F32 = jnp.float32
BF16 = jnp.bfloat16

EPS = 1e-6
N_HEADS = 4
N_BRANCH = 3
CONV_A_K = 31
CONV_B_K = 3
HALO_A = 32
HALO_B = 8
N_DEV = 8
MESH_AXES = ("x", "y", "c")

ADAM_LR = 0.001
ADAM_B1 = 0.9
ADAM_B2 = 0.999
ADAM_EPS = 1e-08
ADAM_WD = 0.01
ADAM_STEP = 10

V7X_VMEM_BYTES = 64 * 1024 * 1024
VMEM_LIMIT = V7X_VMEM_BYTES - 8 * 1024 * 1024

TILE_M = 1024
TILE_N = 1024
TILE_K = 4096
TILE_N_SHALLOW = 2048
TILE_K_SHALLOW = 2048
TILE_ROWS_WIDE = 512
TILE_ROWS_BRANCH = 256
TILE_ROWS_BRANCH_BWD = 128


def _params():
    return pltpu.CompilerParams(vmem_limit_bytes=VMEM_LIMIT)


def _tile(n, t):
    t = min(n, t)
    assert n % t == 0, (n, t)
    return t


def _my_index():
    return 4 * lax.axis_index("x") + 2 * lax.axis_index("y") + lax.axis_index("c")


def _mesh_id(p):
    return (p // 4, (p // 2) % 2, p % 2)


def _mm(name, grid, a, a_spec, b, b_spec, *, ta, tb, acc_shape, extras, outs, epilogue):
    nk = grid[-1]
    lead = len(grid) - 3
    ne, no = len(extras), len(outs)
    dn = (((0,) if ta else (1,), (1,) if tb else (0,)), ((), ()))

    def body(a_ref, b_ref, *rest):
        extra_refs = rest[:ne]
        out_refs = rest[ne:ne + no]
        i, j, k = pl.program_id(lead), pl.program_id(lead + 1), pl.program_id(lead + 2)
        prod = lax.dot_general(a_ref[...], b_ref[...], dn, preferred_element_type=F32)
        if nk == 1:
            epilogue(prod, extra_refs, out_refs, i, j)
        else:
            acc_ref = rest[ne + no]

            @pl.when(k == 0)
            def _():
                acc_ref[...] = prod

            @pl.when(k > 0)
            def _():
                acc_ref[...] += prod

            @pl.when(k == nk - 1)
            def _():
                epilogue(acc_ref[...], extra_refs, out_refs, i, j)

    return pl.pallas_call(
        body, name=name, grid=grid,
        in_specs=[a_spec, b_spec] + [s for _, s in extras],
        out_specs=[s for _, s in outs],
        out_shape=[o for o, _ in outs],
        scratch_shapes=[pltpu.VMEM(acc_shape, F32)] if nk > 1 else [],
        compiler_params=_params(),
    )(a, b, *[e for e, _ in extras])


def _tile_n(n, kd):
    return _tile(n, TILE_N_SHALLOW if kd <= TILE_K_SHALLOW else TILE_N)


def _store_epilogue(res, extra_refs, out_refs, i, j):
    out_refs[0][...] = res.astype(out_refs[0].dtype)


def _mm_nn(name, a, b, out_dtype):
    m, kd = a.shape
    n = b.shape[1]
    bm, bn, bk = _tile(m, TILE_M), _tile_n(n, kd), _tile(kd, TILE_K)
    return _mm(name, (m // bm, n // bn, kd // bk), a, pl.BlockSpec((bm, bk), lambda i, j, k: (i, k)),
               b, pl.BlockSpec((bk, bn), lambda i, j, k: (k, j)), ta=False, tb=False, acc_shape=(bm, bn), extras=[],
               outs=[(jax.ShapeDtypeStruct((m, n), out_dtype), pl.BlockSpec((bm, bn), lambda i, j, k: (i, j)))],
               epilogue=_store_epilogue)[0]


def _mm_nt(name, a, b, out_dtype, after=None):
    m, kd = a.shape[-2:]
    n = b.shape[-2]
    bm, bn = _tile(m, TILE_M), _tile(n, TILE_N)
    if a.ndim == 2:
        bk = _tile(kd, TILE_K)
        return _mm(name, (m // bm, n // bn, kd // bk), a, pl.BlockSpec((bm, bk), lambda i, j, k: (i, k)),
                   b, pl.BlockSpec((bn, bk), lambda i, j, k: (j, k)), ta=False, tb=True, acc_shape=(bm, bn),
                   extras=[] if after is None else [(after, _UNTOUCHED)],
                   outs=[(jax.ShapeDtypeStruct((m, n), out_dtype), pl.BlockSpec((bm, bn), lambda i, j, k: (i, j)))],
                   epilogue=_store_epilogue)[0]
    nb = a.shape[0]
    return _mm(name, (nb, m // bm, n // bn, 1), a, pl.BlockSpec((None, bm, kd), lambda s, i, j, k: (s, i, 0)),
               b, pl.BlockSpec((None, bn, kd), lambda s, i, j, k: (s, j, 0)), ta=False, tb=True, acc_shape=(bm, bn),
               extras=[],
               outs=[(jax.ShapeDtypeStruct((nb, m, n), out_dtype),
                      pl.BlockSpec((None, bm, bn), lambda s, i, j, k: (s, i, j)))],
               epilogue=_store_epilogue)[0]


def _mm_tn(name, a, b, out_dtype):
    r, m = a.shape[-2:]
    n = b.shape[-1]
    bm, bn, bk = _tile(m, TILE_M), _tile(n, TILE_N), _tile(r, TILE_K)
    if a.ndim == 2:
        return _mm(name, (m // bm, n // bn, r // bk), a, pl.BlockSpec((bk, bm), lambda i, j, k: (k, i)),
                   b, pl.BlockSpec((bk, bn), lambda i, j, k: (k, j)), ta=True, tb=False, acc_shape=(bm, bn), extras=[],
                   outs=[(jax.ShapeDtypeStruct((m, n), out_dtype), pl.BlockSpec((bm, bn), lambda i, j, k: (i, j)))],
                   epilogue=_store_epilogue)[0]
    nb = a.shape[0]
    return _mm(name, (nb, m // bm, n // bn, r // bk), a, pl.BlockSpec((None, bk, bm), lambda s, i, j, k: (s, k, i)),
               b, pl.BlockSpec((None, bk, bn), lambda s, i, j, k: (s, k, j)), ta=True, tb=False, acc_shape=(bm, bn),
               extras=[],
               outs=[(jax.ShapeDtypeStruct((nb, m, n), out_dtype),
                      pl.BlockSpec((None, bm, bn), lambda s, i, j, k: (s, i, j)))],
               epilogue=_store_epilogue)[0]


def _mm_relu2(name, h, w):
    m, kd = h.shape
    n = w.shape[1]
    bm, bn = _tile(m, TILE_M), _tile_n(n, kd)

    def epilogue(res, extra_refs, out_refs, i, j):
        r = jnp.maximum(res, 0.0)
        out_refs[0][...] = r.astype(BF16)
        out_refs[1][...] = (r * r).astype(BF16)

    o = jax.ShapeDtypeStruct((m, n), BF16)
    spec = pl.BlockSpec((bm, bn), lambda i, j, k: (i, j))
    return _mm(name, (m // bm, n // bn, 1), h, pl.BlockSpec((bm, kd), lambda i, j, k: (i, 0)),
               w, pl.BlockSpec((kd, bn), lambda i, j, k: (0, j)), ta=False, tb=False, acc_shape=(bm, bn), extras=[],
               outs=[(o, spec), (o, spec)], epilogue=epilogue)


def _mm_relu2_bwd(name, df, w_down, r):
    m, kd = df.shape
    n = w_down.shape[0]
    bm, bn = _tile(m, TILE_M), _tile_n(n, kd)

    def epilogue(res, extra_refs, out_refs, i, j):
        out_refs[0][...] = (res * (2.0 * extra_refs[0][...].astype(F32))).astype(BF16)

    spec = pl.BlockSpec((bm, bn), lambda i, j, k: (i, j))
    return _mm(name, (m // bm, n // bn, 1), df, pl.BlockSpec((bm, kd), lambda i, j, k: (i, 0)),
               w_down, pl.BlockSpec((bn, kd), lambda i, j, k: (j, 0)), ta=False, tb=True, acc_shape=(bm, bn),
               extras=[(r, spec)], outs=[(jax.ShapeDtypeStruct((m, n), BF16), spec)], epilogue=epilogue)[0]


def _rms_bwd_math(dy, x, g):
    rr = lax.rsqrt(jnp.mean(x * x, axis=-1, keepdims=True) + EPS)
    gy = dy * g
    dx = rr * gy - x * (rr * rr * rr * jnp.mean(x * gy, axis=-1, keepdims=True))
    dg_rows = dy * x * rr
    return dx, dg_rows


_UNTOUCHED = pl.BlockSpec(memory_space=pl.ANY)


def _rmsnorm(name, x, g, after):
    r, d = x.shape
    br = _tile(r, TILE_ROWS_WIDE)

    def body(x_ref, g_ref, after_ref, o_ref):
        xv = x_ref[...]
        rr = lax.rsqrt(jnp.mean(xv * xv, axis=-1, keepdims=True) + EPS)
        o_ref[...] = ((xv * rr) * g_ref[...]).astype(BF16)

    row = pl.BlockSpec((br, d), lambda i: (i, 0))
    return pl.pallas_call(body, name=name, grid=(r // br,),
                          in_specs=[row, pl.BlockSpec((1, d), lambda i: (0, 0)), _UNTOUCHED],
                          out_specs=row, out_shape=jax.ShapeDtypeStruct((r, d), BF16),
                          compiler_params=_params())(x, g, after)


def _res_norm(name, z, x, g, g_next=None):
    r, d = x.shape
    br = _tile(r, TILE_ROWS_WIDE)
    n_next = 0 if g_next is None else 1

    def body(z_ref, x_ref, g_ref, *rest):
        zv = z_ref[...]
        rr = lax.rsqrt(jnp.mean(zv * zv, axis=-1, keepdims=True) + EPS)
        xn = x_ref[...] + (zv * rr) * g_ref[...]
        rest[n_next][...] = xn
        if n_next:
            r2 = lax.rsqrt(jnp.mean(xn * xn, axis=-1, keepdims=True) + EPS)
            rest[2][...] = ((xn * r2) * rest[0][...]).astype(BF16)

    row = pl.BlockSpec((br, d), lambda i: (i, 0))
    vec = pl.BlockSpec((1, d), lambda i: (0, 0))
    outs = pl.pallas_call(body, name=name, grid=(r // br,), in_specs=[row, row, vec] + [vec] * n_next,
                          out_specs=[row] * (1 + n_next),
                          out_shape=[jax.ShapeDtypeStruct((r, d), F32)] + [jax.ShapeDtypeStruct((r, d), BF16)] * n_next,
                          compiler_params=_params())(z, x, g, *([] if g_next is None else [g_next]))
    return outs if n_next else (outs[0], None)


def _rms_bwd(name, dy, x, g, out_dtype, after, residual=None, below=None):
    r, d = x.shape
    br = _tile(r, TILE_ROWS_WIDE if below is None else TILE_ROWS_WIDE // 2)
    n_res = 0 if residual is None else 1
    n_below = 0 if below is None else 1

    def body(dy_ref, x_ref, g_ref, after_ref, *rest):
        ins, outs = rest[:n_res + 2 * n_below], rest[n_res + 2 * n_below:]
        first = pl.program_id(0) == 0
        dx, dg_rows = _rms_bwd_math(dy_ref[...], x_ref[...], g_ref[...])
        if n_res:
            dx = dx + ins[0][...]
        outs[0][...] = dx.astype(out_dtype)

        @pl.when(first)
        def _():
            outs[1][...] = jnp.zeros_like(outs[1])

        outs[1][...] += jnp.sum(dg_rows, axis=0, keepdims=True)
        if n_below:
            dz, dgz_rows = _rms_bwd_math(dx, ins[n_res][...], ins[n_res + 1][...])
            outs[2][...] = dz.astype(BF16)

            @pl.when(first)
            def _():
                outs[3][...] = jnp.zeros_like(outs[3])

            outs[3][...] += jnp.sum(dgz_rows, axis=0, keepdims=True)

    row = pl.BlockSpec((br, d), lambda i: (i, 0))
    vec = pl.BlockSpec((1, d), lambda i: (0, 0))
    o_row, o_vec = jax.ShapeDtypeStruct((r, d), out_dtype), jax.ShapeDtypeStruct((1, d), F32)
    return pl.pallas_call(
        body, name=name, grid=(r // br,),
        in_specs=[row, row, vec, _UNTOUCHED] + [row] * n_res + [row, vec] * n_below,
        out_specs=[row, vec] + [row, vec] * n_below,
        out_shape=[o_row, o_vec] + [jax.ShapeDtypeStruct((r, d), BF16), o_vec] * n_below,
        compiler_params=_params(),
    )(dy, x, g, after, *([] if residual is None else [residual]), *([] if below is None else below))


def _loss_head(y, target):
    r, d = y.shape
    br = _tile(r, TILE_ROWS_WIDE)

    def body(y_ref, t_ref, dy_ref, l_ref):
        diff = y_ref[...] - t_ref[...]
        dy_ref[...] = diff * (1.0 / d)

        @pl.when(pl.program_id(0) == 0)
        def _():
            l_ref[...] = jnp.zeros_like(l_ref)

        l_ref[...] += 0.5 * jnp.sum(jnp.mean(diff * diff, axis=-1, keepdims=True))

    row = pl.BlockSpec((br, d), lambda i: (i, 0))
    one = pl.BlockSpec((8, 128), lambda i: (0, 0))
    dy, l = pl.pallas_call(body, name="loss_head", grid=(r // br,), in_specs=[row, row], out_specs=[row, one],
                           out_shape=[jax.ShapeDtypeStruct((r, d), F32), jax.ShapeDtypeStruct((8, 128), F32)],
                           compiler_params=_params())(y, target)
    return l[0, 0], dy


def _adamw(name, parts, branch, w, m, v, layer, prev, after):
    _, r, c = w.shape
    p = parts.shape[0]
    br = r
    for cand in (512, 256, 128, 64, 32, 16):
        if r % cand == 0 and cand * c * 4 <= 2 * 1024 * 1024:
            br = cand
            break
    bc1 = 1.0 - ADAM_B1 ** ADAM_STEP
    bc2 = 1.0 - ADAM_B2 ** ADAM_STEP
    if prev is None:
        prev = [lax.empty(w.shape, F32) for _ in range(4)]

    def body(p_ref, w_ref, m_ref, v_ref, after_ref, pg, pd, pm, pv, g_out, d_out, m_out, v_out):
        g = p_ref[0].astype(F32)
        for s in range(1, p):
            g = g + p_ref[s].astype(F32)
        m2 = ADAM_B1 * m_ref[...] + (1.0 - ADAM_B1) * g
        v2 = ADAM_B2 * v_ref[...] + (1.0 - ADAM_B2) * (g * g)
        m_hat = m2 / bc1
        v_hat = v2 / bc2
        g_out[...] = g
        d_out[...] = -ADAM_LR * (m_hat / (jnp.sqrt(v_hat) + ADAM_EPS) + ADAM_WD * w_ref[...])
        m_out[...] = m2
        v_out[...] = v2

    if branch is None:
        p_spec = pl.BlockSpec((p, br, c), lambda i: (0, i, 0))
    else:
        p_spec = pl.BlockSpec((p, None, br, c), lambda i: (0, branch, i, 0))
    slab = pl.BlockSpec((None, br, c), lambda i: (layer, i, 0))
    o = jax.ShapeDtypeStruct(w.shape, F32)
    return pl.pallas_call(body, name=name, grid=(r // br,),
                          in_specs=[p_spec, slab, slab, slab, _UNTOUCHED] + [_UNTOUCHED] * 4,
                          out_specs=[slab] * 4, out_shape=[o] * 4, input_output_aliases={5: 0, 6: 1, 7: 2, 8: 3},
                          compiler_params=_params())(parts, w, m, v, after, *prev)


def _layer_norm_parts(a1, ln_g, ln_b):
    mu = jnp.mean(a1, axis=-1, keepdims=True)
    xc = a1 - mu
    rstd = lax.rsqrt(jnp.mean(xc * xc, axis=-1, keepdims=True) + EPS)
    xhat = xc * rstd
    a2 = xhat * ln_g + ln_b
    return xhat, rstd, a2


def _softmax_rows(s):
    e = jnp.exp(s - jnp.max(s, axis=-1, keepdims=True))
    return e / jnp.sum(e, axis=-1, keepdims=True)


SUBLANES = 8
CONV_ROWS = 32


def _shift_copies(ext_ref, sh_ref, rows):
    for r in range(1, SUBLANES):
        sh_ref[r - 1, pl.ds(0, rows), :] = ext_ref[pl.ds(r, rows), :]


def _fill_tap_tiles(taps_ref, tile_ref):
    @pl.when(pl.program_id(0) == 0)
    def _():
        for k in range(CONV_A_K):
            tile_ref[k] = jnp.broadcast_to(taps_ref[pl.ds(k, 1), :], tile_ref.shape[1:])


def _rows_at(ext_ref, sh_ref, offset, ts):
    q, r = divmod(offset, SUBLANES)
    if r == 0:
        return ext_ref[pl.ds(SUBLANES * q, ts), :]
    return sh_ref[r - 1, pl.ds(SUBLANES * q, ts), :]


def _branch_specs(ts, c):
    def col(ci):
        return pl.BlockSpec((ts, c), lambda i: (i, ci))

    def prev(ci, h):
        return pl.BlockSpec((h, c), lambda i: (jnp.maximum(i * (ts // h) - 1, 0), ci))

    return col, prev


def _branch_fwd(proj, kv, conv_a_w, conv_a_b, ln_g, ln_b, conv_b_w):
    s = proj.shape[0]
    mlen, c2 = kv.shape
    c = c2 // 2
    hd = c // N_HEADS
    ts = _tile(s, TILE_ROWS_BRANCH)
    scale = hd ** -0.5
    col, prev = _branch_specs(ts, c)

    def body(av, ag, sb, sc, sx, q, hav, hag, hsc, hsx, kv_ref, caw, cab, lng, lnb, cbw, a1_ref, abo_ref, exta, extb,
             sha, wtile):
        not_first = (pl.program_id(0) > 0).astype(F32)
        _fill_tap_tiles(caw, wtile)
        exta[pl.ds(0, HALO_A), :] = hav[...] * jax.nn.sigmoid(hag[...]) * not_first
        exta[pl.ds(HALO_A, ts), :] = av[...] * jax.nn.sigmoid(ag[...])
        _shift_copies(exta, sha, ts + HALO_A - SUBLANES)
        chunk = (CONV_ROWS // SUBLANES, SUBLANES, c)
        for r0 in range(0, ts, CONV_ROWS):
            acc = jnp.broadcast_to(cab[...], chunk)
            for k in range(CONV_A_K):
                rows = _rows_at(exta, sha, HALO_A - (CONV_A_K - 1) + k + r0, CONV_ROWS)
                acc = acc + wtile[k][None] * rows.reshape(chunk)
            a1_ref[pl.ds(r0, CONV_ROWS), :] = acc.reshape(CONV_ROWS, c)
        _, _, a2 = _layer_norm_parts(a1_ref[...], lng[...], lnb[...])
        abo_ref[0] = (a2 * jax.nn.sigmoid(a2)).astype(BF16)
        extb[pl.ds(0, HALO_B), :] = hsc[...] * hsx[...] * not_first
        extb[pl.ds(HALO_B, ts), :] = sc[...] * sx[...]
        u = cbw[pl.ds(0, 1), :] * extb[pl.ds(HALO_B - (CONV_B_K - 1), ts), :]
        for k in range(1, CONV_B_K):
            u = u + cbw[pl.ds(k, 1), :] * extb[pl.ds(HALO_B - (CONV_B_K - 1) + k, ts), :]
        abo_ref[1] = (sb[...] * u).astype(BF16)
        for h in range(N_HEADS):
            qh = q[:, h * hd:(h + 1) * hd].astype(BF16)
            kh = kv_ref[:, h * hd:(h + 1) * hd]
            vh = kv_ref[:, c + h * hd:c + (h + 1) * hd]
            sc_ = lax.dot_general(qh, kh, (((1,), (1,)), ((), ())), preferred_element_type=F32) * scale
            p = _softmax_rows(sc_).astype(BF16)
            abo_ref[2, :, h * hd:(h + 1) * hd] = jnp.dot(p, vh, preferred_element_type=F32).astype(BF16)

    full = lambda shp: pl.BlockSpec(shp, lambda i: (0,) * len(shp))
    return pl.pallas_call(
        body, name="branch_fwd", grid=(s // ts,),
        in_specs=[col(0), col(1), col(2), col(3), col(4), col(5), prev(0, HALO_A), prev(1, HALO_A),
                  prev(3, HALO_B), prev(4, HALO_B), full((mlen, c2)), full(conv_a_w.shape), full((1, c)),
                  full((1, c)), full((1, c)), full(conv_b_w.shape)],
        out_specs=[pl.BlockSpec((ts, c), lambda i: (i, 0)), pl.BlockSpec((N_BRANCH, ts, c), lambda i: (0, i, 0))],
        out_shape=[jax.ShapeDtypeStruct((s, c), F32), jax.ShapeDtypeStruct((N_BRANCH, s, c), BF16)],
        scratch_shapes=[pltpu.VMEM((HALO_A + ts, c), F32), pltpu.VMEM((HALO_B + ts, c), F32),
                        pltpu.VMEM((SUBLANES - 1, ts + HALO_A - SUBLANES, c), F32),
                        pltpu.VMEM((CONV_A_K, SUBLANES, c), F32)],
        compiler_params=_params(),
    )(proj, proj, proj, proj, proj, proj, proj, proj, proj, proj, kv, conv_a_w, conv_a_b, ln_g, ln_b, conv_b_w)


def _branch_bwd(proj, a1, dabo, kv, conv_a_w, ln_g, ln_b, conv_b_w, dproj):
    s = proj.shape[0]
    mlen, c2 = kv.shape
    c = c2 // 2
    hd = c // N_HEADS
    ts = _tile(s, TILE_ROWS_BRANCH_BWD)
    nt = s // ts
    scale = hd ** -0.5
    col, prev = _branch_specs(ts, c)

    def nxt(h, lead=None, ci=0):
        if lead is None:
            return pl.BlockSpec((h, c), lambda i: (jnp.minimum((i + 1) * (ts // h), s // h - 1), ci))
        return pl.BlockSpec((None, h, c), lambda i: (lead, jnp.minimum((i + 1) * (ts // h), s // h - 1), 0))

    def body(av, ag, sb, sc, sx, q, hav, hag, hsc, hsx, nsb, a1_ref, na1, dabo_ref, nda, ndb, kv_ref, caw, lng, lnb,
             cbw, dproj_in, dp_ref, dkv_ref, dwa_ref, misc_ref, exta, extda, extb, extdu, sha, shda, wtile):
        i = pl.program_id(0)
        _fill_tap_tiles(caw, wtile)
        not_first = (i > 0).astype(F32)
        not_last = (i < nt - 1).astype(F32)

        @pl.when(i == 0)
        def _():
            dkv_ref[...] = jnp.zeros_like(dkv_ref)
            dwa_ref[...] = jnp.zeros_like(dwa_ref)
            misc_ref[...] = jnp.zeros_like(misc_ref)

        def rowsum(t):
            return jnp.sum(t, axis=0, keepdims=True)

        def da1_of(a1v, da3):
            xhat, rstd, a2 = _layer_norm_parts(a1v, lng[...], lnb[...])
            sg = jax.nn.sigmoid(a2)
            da2 = da3 * (sg * (1.0 + a2 * (1.0 - sg)))
            dxh = da2 * lng[...]
            da1 = rstd * (dxh - jnp.mean(dxh, axis=-1, keepdims=True)
                          - xhat * jnp.mean(dxh * xhat, axis=-1, keepdims=True))
            return da1, da2, xhat

        da1, da2, xhat = da1_of(a1_ref[...], dabo_ref[0])
        misc_ref[pl.ds(0, 1), :] += rowsum(da1)
        misc_ref[pl.ds(1, 1), :] += rowsum(da2 * xhat)
        misc_ref[pl.ds(2, 1), :] += rowsum(da2)
        extda[pl.ds(0, ts), :] = da1
        extda[pl.ds(ts, HALO_A), :] = da1_of(na1[...], nda[...])[0] * not_last
        sga = jax.nn.sigmoid(ag[...])
        exta[pl.ds(0, HALO_A), :] = hav[...] * jax.nn.sigmoid(hag[...]) * not_first
        exta[pl.ds(HALO_A, ts), :] = av[...] * sga
        _shift_copies(exta, sha, ts + HALO_A - SUBLANES)
        _shift_copies(extda, shda, ts + HALO_A - SUBLANES)
        for r0 in range(0, ts, CONV_ROWS):
            rows = pl.ds(r0, CONV_ROWS)
            da1_rows = extda[rows, :]
            chunk = (CONV_ROWS // SUBLANES, SUBLANES, c)
            da0 = jnp.zeros(chunk, F32)
            for k in range(CONV_A_K):
                tap = da1_rows * _rows_at(exta, sha, HALO_A - (CONV_A_K - 1) + k + r0, CONV_ROWS)
                dwa_ref[pl.ds(SUBLANES * k, SUBLANES), :] += tap.reshape(chunk).sum(axis=0)
                da0 = da0 + wtile[CONV_A_K - 1 - k][None] * _rows_at(extda, shda, k + r0, CONV_ROWS).reshape(chunk)
            da0 = da0.reshape(CONV_ROWS, c)
            sg = sga[r0:r0 + CONV_ROWS]
            dp_ref[rows, 0:c] = (da0 * sg).astype(BF16)
            dp_ref[rows, c:2 * c] = (da0 * av[rows, :] * sg * (1.0 - sg)).astype(BF16)

        extb[pl.ds(0, HALO_B), :] = hsc[...] * hsx[...] * not_first
        extb[pl.ds(HALO_B, ts), :] = sc[...] * sx[...]
        dbu = dabo_ref[1]
        du = dbu * sb[...]
        extdu[pl.ds(0, ts), :] = du
        extdu[pl.ds(ts, HALO_B), :] = ndb[...] * nsb[...] * not_last
        u = jnp.zeros((ts, c), F32)
        dpr = jnp.zeros((ts, c), F32)
        for k in range(CONV_B_K):
            shifted = extb[pl.ds(HALO_B - (CONV_B_K - 1) + k, ts), :]
            u = u + cbw[pl.ds(k, 1), :] * shifted
            misc_ref[pl.ds(3 + k, 1), :] += rowsum(du * shifted)
            dpr = dpr + cbw[pl.ds(CONV_B_K - 1 - k, 1), :] * extdu[pl.ds(k, ts), :]
        dp_ref[:, 2 * c:3 * c] = (dbu * u).astype(BF16)
        dp_ref[:, 3 * c:4 * c] = (dpr * sx[...]).astype(BF16)
        dp_ref[:, 4 * c:5 * c] = (dpr * sc[...]).astype(BF16)

        nt_dims = (((1,), (1,)), ((), ()))
        tn_dims = (((0,), (0,)), ((), ()))
        for h in range(N_HEADS):
            lo, hi = h * hd, (h + 1) * hd
            qh = q[:, lo:hi].astype(BF16)
            kh = kv_ref[:, lo:hi]
            vh = kv_ref[:, c + lo:c + hi]
            p = _softmax_rows(lax.dot_general(qh, kh, nt_dims, preferred_element_type=F32) * scale)
            pb = p.astype(BF16)
            doh = dabo_ref[2, :, lo:hi].astype(BF16)
            dpm = lax.dot_general(doh, vh, nt_dims, preferred_element_type=F32)
            ds = (p * (dpm - jnp.sum(dpm * p, axis=-1, keepdims=True)) * scale).astype(BF16)
            dp_ref[:, 5 * c + lo:5 * c + hi] = jnp.dot(ds, kh, preferred_element_type=F32).astype(BF16)
            dkv_ref[:, lo:hi] += lax.dot_general(ds, qh, tn_dims, preferred_element_type=F32)
            dkv_ref[:, c + lo:c + hi] += lax.dot_general(pb, doh, tn_dims, preferred_element_type=F32)

    full = lambda shp: pl.BlockSpec(shp, lambda i: (0,) * len(shp))
    n_in_before_dproj = 21
    shifted = pltpu.VMEM((SUBLANES - 1, ts + HALO_A - SUBLANES, c), F32)
    dp, dkv, dwa, misc = pl.pallas_call(
        body, name="branch_bwd", grid=(nt,),
        in_specs=[col(0), col(1), col(2), col(3), col(4), col(5), prev(0, HALO_A), prev(1, HALO_A),
                  prev(3, HALO_B), prev(4, HALO_B), nxt(HALO_B, ci=2),
                  pl.BlockSpec((ts, c), lambda i: (i, 0)), nxt(HALO_A),
                  pl.BlockSpec((N_BRANCH, ts, c), lambda i: (0, i, 0)), nxt(HALO_A, lead=0), nxt(HALO_B, lead=1),
                  full((mlen, c2)), full(conv_a_w.shape), full((1, c)), full((1, c)), full(conv_b_w.shape),
                  pl.BlockSpec(memory_space=pl.ANY)],
        out_specs=[pl.BlockSpec((ts, 6 * c), lambda i: (i, 0)), full((mlen, c2)), full((HALO_A * SUBLANES, c)),
                   full((8, c))],
        out_shape=[jax.ShapeDtypeStruct(dproj.shape, BF16), jax.ShapeDtypeStruct((mlen, c2), F32),
                   jax.ShapeDtypeStruct((HALO_A * SUBLANES, c), F32), jax.ShapeDtypeStruct((8, c), F32)],
        scratch_shapes=[pltpu.VMEM((HALO_A + ts, c), F32), pltpu.VMEM((ts + HALO_A, c), F32),
                        pltpu.VMEM((HALO_B + ts, c), F32), pltpu.VMEM((ts + HALO_B, c), F32), shifted, shifted,
                        pltpu.VMEM((CONV_A_K, SUBLANES, c), F32)],
        input_output_aliases={n_in_before_dproj: 0},
        compiler_params=_params(),
    )(proj, proj, proj, proj, proj, proj, proj, proj, proj, proj, proj, a1, a1, dabo, dabo, dabo, kv, conv_a_w,
      ln_g, ln_b, conv_b_w, dproj)
    return dp, dkv, dwa.reshape(HALO_A, SUBLANES, c).sum(axis=1), misc


def _merge_fwd(abo, w_cat, proj, d, after):
    nb, s, c = abo.shape
    bm, bn = _tile(s, TILE_M), _tile(d, TILE_N)
    gate_col0 = (proj.shape[1] - nb * d) // bn

    def body(a_ref, w_ref, g_ref, after_ref, y_ref, m_ref, acc_ref):
        k = pl.program_id(2)
        y = jnp.dot(a_ref[...], w_ref[...], preferred_element_type=F32)
        y_ref[...] = y.astype(BF16)
        contrib = jax.nn.sigmoid(g_ref[...]) * y

        @pl.when(k == 0)
        def _():
            acc_ref[...] = contrib

        @pl.when(k > 0)
        def _():
            acc_ref[...] += contrib

        @pl.when(k == nb - 1)
        def _():
            m_ref[...] = acc_ref[...].astype(BF16)

    return pl.pallas_call(
        body, name="merge_fwd", grid=(s // bm, d // bn, nb),
        in_specs=[pl.BlockSpec((None, bm, c), lambda i, j, k: (k, i, 0)),
                  pl.BlockSpec((None, c, bn), lambda i, j, k: (k, 0, j)),
                  pl.BlockSpec((bm, bn), lambda i, j, k: (i, gate_col0 + k * (d // bn) + j)), _UNTOUCHED],
        out_specs=[pl.BlockSpec((None, bm, bn), lambda i, j, k: (k, i, j)),
                   pl.BlockSpec((bm, bn), lambda i, j, k: (i, j))],
        out_shape=[jax.ShapeDtypeStruct((nb, s, d), BF16), jax.ShapeDtypeStruct((s, d), BF16)],
        scratch_shapes=[pltpu.VMEM((bm, bn), F32)],
        compiler_params=_params(),
    )(abo, w_cat, proj, after)


def _merge_bwd(dz, w_o, y, proj):
    s, d = dz.shape
    nb = y.shape[0]
    nin = proj.shape[1]
    bm, bn = _tile(s, TILE_M), _tile(d, TILE_N)
    gate_col0 = (nin - nb * d) // bn

    def body(dz_ref, w_ref, y_ref, g_ref, dy_ref, dg_ref, acc_ref):
        @pl.when(pl.program_id(2) == 0)
        def _():
            acc_ref[...] = lax.dot_general(dz_ref[...], w_ref[...], (((1,), (1,)), ((), ())),
                                           preferred_element_type=F32)

        dm = acc_ref[...]
        gt = jax.nn.sigmoid(g_ref[...])
        dy_ref[...] = (dm * gt).astype(BF16)
        dg_ref[...] = (dm * y_ref[...].astype(F32) * gt * (1.0 - gt)).astype(BF16)

    gate = lambda i, j, k: (i, gate_col0 + k * (d // bn) + j)
    return pl.pallas_call(
        body, name="merge_bwd", grid=(s // bm, d // bn, nb),
        in_specs=[pl.BlockSpec((bm, d), lambda i, j, k: (i, 0)), pl.BlockSpec((bn, d), lambda i, j, k: (j, 0)),
                  pl.BlockSpec((None, bm, bn), lambda i, j, k: (k, i, j)), pl.BlockSpec((bm, bn), gate)],
        out_specs=[pl.BlockSpec((None, bm, bn), lambda i, j, k: (k, i, j)), pl.BlockSpec((bm, bn), gate)],
        out_shape=[jax.ShapeDtypeStruct((nb, s, d), BF16), jax.ShapeDtypeStruct((s, nin), BF16)],
        scratch_shapes=[pltpu.VMEM((bm, bn), F32)],
        compiler_params=_params(),
    )(dz, w_o, y, proj)


def _window(ref, axis, who, length):
    idx = [slice(None)] * len(ref.shape)
    idx[axis] = pl.ds(pl.multiple_of(who * length, length), length)
    return ref.at[tuple(idx)]


def _all_gather(name, shards, axes, after):
    n = len(shards)
    n_copies = 7
    out_shapes = []
    for sh, ax in zip(shards, axes):
        shp = list(sh.shape)
        shp[ax] *= N_DEV
        out_shapes.append(jax.ShapeDtypeStruct(tuple(shp), sh.dtype))

    def body(*refs):
        ins = refs[:n]
        outs = refs[n + len(after):2 * n + len(after)]
        send, recv, local = refs[2 * n + len(after):]
        x, y, c = lax.axis_index("x"), lax.axis_index("y"), lax.axis_index("c")
        sibling = (x, y, 1 - c)
        chips = [(1 - x, y), (x, 1 - y), (1 - x, 1 - y)]
        lens = [ins[a].shape[axes[a]] for a in range(n)]

        def block(a, px, py, pc):
            return _window(outs[a], axes[a], 4 * px + 2 * py + pc, lens[a])

        def copy(a, k, src, dst, to):
            return pltpu.make_async_remote_copy(src_ref=src, dst_ref=dst, send_sem=send.at[a, k],
                                                recv_sem=recv.at[a, k], device_id=to,
                                                device_id_type=pl.DeviceIdType.MESH)

        mine = [pltpu.make_async_copy(ins[a], block(a, x, y, c), local.at[a]) for a in range(n)]
        for cp in mine:
            cp.start()
        started = []
        for j, chip in enumerate(chips):
            for a in range(n):
                started.append(copy(a, 1 + j, ins[a], block(a, x, y, c), (*chip, c)))
                started[-1].start()
        for a in range(n):
            started.append(copy(a, 0, ins[a], block(a, x, y, c), sibling))
            started[-1].start()
        for j, chip in enumerate(chips):
            for a in range(n):
                landed = block(a, *chip, c)
                copy(a, 1 + j, landed, landed, (*chip, c)).wait_recv()
                started.append(copy(a, 4 + j, landed, landed, sibling))
                started[-1].start()
        for a in range(n):
            copy(a, 0, ins[a], block(a, x, y, 1 - c), sibling).wait_recv()
            for j, chip in enumerate(chips):
                copy(a, 4 + j, ins[a], block(a, *chip, 1 - c), sibling).wait_recv()
        for cp in started:
            cp.wait_send()
        for cp in mine:
            cp.wait()

    hbm = pl.BlockSpec(memory_space=pl.ANY)
    return pl.pallas_call(
        body, name=name, in_specs=[hbm] * (n + len(after)), out_specs=[hbm] * n, out_shape=out_shapes,
        scratch_shapes=[pltpu.SemaphoreType.DMA((n, n_copies)), pltpu.SemaphoreType.DMA((n, n_copies)),
                        pltpu.SemaphoreType.DMA((n,))],
    )(*shards, *after)


def _pair(a, d):
    return a * N_DEV + d


def _push_ends(kind, src, land, axis, length, me, to):
    if kind == "gather":
        return src, _window(land, axis, me, length)
    return _window(src, axis, to, length), land.at[me]


def _arrival_ends(kind, src, land, axis, length, me, frm):
    if kind == "gather":
        return src, _window(land, axis, frm, length)
    return _window(src, axis, me, length), land.at[frm]


def _exchange_start(name, kind, srcs, axes, after):
    n = len(srcs)
    if kind == "gather":
        lens = [s.shape[ax] for s, ax in zip(srcs, axes)]
        land_shapes = [s.shape[:ax] + (s.shape[ax] * N_DEV,) + s.shape[ax + 1:] for s, ax in zip(srcs, axes)]
    else:
        lens = [s.shape[ax] // N_DEV for s, ax in zip(srcs, axes)]
        land_shapes = [(N_DEV,) + s.shape[:ax] + (ln,) + s.shape[ax + 1:] for s, ax, ln in zip(srcs, axes, lens)]
    lands = [lax.empty(shp, s.dtype) for shp, s in zip(land_shapes, srcs)]

    def body(*refs):
        ins = refs[:n]
        send, recv = refs[2 * n + 1], refs[2 * n + 2]
        lnd = refs[2 * n + 3 + n:2 * n + 3 + 2 * n]
        token = refs[-1]
        me = _my_index()
        for a in range(n):
            for d in range(1, N_DEV):
                to = (me + d) % N_DEV
                src, dst = _push_ends(kind, ins[a], lnd[a], axes[a], lens[a], me, to)
                pltpu.make_async_remote_copy(src_ref=src, dst_ref=dst, send_sem=send.at[_pair(a, d)],
                                             recv_sem=recv.at[_pair(a, d)], device_id=_mesh_id(to),
                                             device_id_type=pl.DeviceIdType.MESH).start()
        for a in range(n):
            pltpu.make_async_copy(*_push_ends(kind, ins[a], lnd[a], axes[a], lens[a], me, me),
                                  send.at[_pair(a, 0)]).start()
        token[...] = jnp.zeros_like(token)

    hbm = pl.BlockSpec(memory_space=pltpu.HBM)
    sem = pl.BlockSpec(memory_space=pltpu.SEMAPHORE)
    held = [pltpu.with_memory_space_constraint(t, pltpu.HBM) for t in list(srcs) + lands]
    outs = pl.pallas_call(
        body, name=name,
        in_specs=[hbm] * (2 * n) + [_UNTOUCHED],
        out_specs=[sem, sem] + [hbm] * (2 * n) + [pl.BlockSpec(memory_space=pltpu.VMEM)],
        out_shape=[pltpu.SemaphoreType.DMA((n * N_DEV,)), pltpu.SemaphoreType.DMA((n * N_DEV,))]
        + [pltpu.HBM(t.shape, t.dtype) for t in held] + [jax.ShapeDtypeStruct((8, 128), F32)],
        input_output_aliases={i: 2 + i for i in range(2 * n)},
        compiler_params=pltpu.CompilerParams(has_side_effects=pltpu.SideEffectType.DATAFLOW_SIDE_EFFECTING),
    )(*held, after)
    handle = dict(kind=kind, axes=axes, lens=lens, send=outs[0], recv=outs[1], srcs=outs[2:2 + n],
                  lands=outs[2 + n:2 + 2 * n])
    return handle, outs[-1]


def _exchange_wait(name, handle, after):
    kind, axes, lens = handle["kind"], handle["axes"], handle["lens"]
    srcs, lands = handle["srcs"], handle["lands"]
    n = len(srcs)
    n_in = 2 * n + 2 + len(after)

    def body(*refs):
        ins = refs[:n]
        send, recv = refs[2 * n], refs[2 * n + 1]
        got = refs[n_in + n:n_in + 2 * n]
        me = _my_index()
        for d in range(1, N_DEV):
            frm = (me + N_DEV - d) % N_DEV
            for a in range(n):
                src, dst = _arrival_ends(kind, ins[a], got[a], axes[a], lens[a], me, frm)
                pltpu.make_async_remote_copy(src_ref=src, dst_ref=dst, send_sem=send.at[_pair(a, d)],
                                             recv_sem=recv.at[_pair(a, d)], device_id=_mesh_id(frm),
                                             device_id_type=pl.DeviceIdType.MESH).wait_recv()
        for d in range(1, N_DEV):
            to = (me + d) % N_DEV
            for a in range(n):
                src, dst = _push_ends(kind, ins[a], got[a], axes[a], lens[a], me, to)
                pltpu.make_async_remote_copy(src_ref=src, dst_ref=dst, send_sem=send.at[_pair(a, d)],
                                             recv_sem=recv.at[_pair(a, d)], device_id=_mesh_id(to),
                                             device_id_type=pl.DeviceIdType.MESH).wait_send()
        for a in range(n):
            pltpu.make_async_copy(*_push_ends(kind, ins[a], got[a], axes[a], lens[a], me, me),
                                  send.at[_pair(a, 0)]).wait()

    hbm = pl.BlockSpec(memory_space=pltpu.HBM)
    sem = pl.BlockSpec(memory_space=pltpu.SEMAPHORE)
    outs = pl.pallas_call(
        body, name=name,
        in_specs=[hbm] * (2 * n) + [sem, sem] + [_UNTOUCHED] * len(after),
        out_specs=[hbm] * (2 * n),
        out_shape=[pltpu.HBM(t.shape, t.dtype) for t in list(srcs) + list(lands)],
        input_output_aliases={i: i for i in range(2 * n)},
        compiler_params=pltpu.CompilerParams(has_side_effects=pltpu.SideEffectType.DATAFLOW_SIDE_EFFECTING),
    )(*srcs, *lands, handle["send"], handle["recv"], *after)
    return outs[n:]


def _all_reduce_small(packed, after):
    r, c = packed.shape

    def body(p_ref, *rest):
        o_ref, buf, send, recv = rest[len(after):]
        me = _my_index()
        buf[me] = p_ref[...]
        pushes = []
        for d in range(1, N_DEV):
            to = (me + d) % N_DEV
            cp = pltpu.make_async_remote_copy(src_ref=p_ref, dst_ref=buf.at[me], send_sem=send.at[d],
                                              recv_sem=recv.at[d], device_id=_mesh_id(to),
                                              device_id_type=pl.DeviceIdType.MESH)
            cp.start()
            pushes.append(cp)
        for d in range(1, N_DEV):
            frm = (me + N_DEV - d) % N_DEV
            pltpu.make_async_remote_copy(src_ref=p_ref, dst_ref=buf.at[frm], send_sem=send.at[d], recv_sem=recv.at[d],
                                         device_id=_mesh_id(frm), device_id_type=pl.DeviceIdType.MESH).wait_recv()
        for cp in pushes:
            cp.wait_send()
        acc = buf[0]
        for s in range(1, N_DEV):
            acc = acc + buf[s]
        o_ref[...] = acc

    vmem = pl.BlockSpec(memory_space=pltpu.VMEM)
    return pl.pallas_call(
        body, name="all_reduce_small", in_specs=[vmem] + [_UNTOUCHED] * len(after), out_specs=vmem,
        out_shape=jax.ShapeDtypeStruct((r, c), F32),
        scratch_shapes=[pltpu.VMEM((N_DEV, r, c), F32), pltpu.SemaphoreType.DMA((N_DEV,)),
                        pltpu.SemaphoreType.DMA((N_DEV,))],
        compiler_params=_params(),
    )(packed, *after)


def _layer_fwd_branches(x, h, mem, sm, wf):
    proj = _mm_nn("proj", h, wf["w_in"], F32)
    mem_n = _rmsnorm("rms_mem", mem, sm["g_mem"], mem)
    kv = _mm_nn("kv", mem_n, wf["w_kv"], BF16)
    a1, abo = _branch_fwd(proj, kv, wf["conv_a_w"], sm["conv_a_b"], sm["ln_a_g"], sm["ln_a_b"], wf["conv_b_w"])
    return dict(x=x, h=h, proj=proj, mem_n=mem_n, kv=kv, a1=a1, abo=abo)


def _layer_fwd_rest(sm, wf, sv, after, g_next):
    x = sv["x"]
    y, merged = _merge_fwd(sv["abo"], wf["w_cat"], sv["proj"], x.shape[1], after)
    z = _mm_nn("mix_out", merged, wf["w_o"], F32)
    x1, h2 = _res_norm("mix_out_norm", z, x, sm["g_mix_post"], sm["g_mlp_pre"])
    r, act = _mm_relu2("mlp_up", h2, wf["w_up"])
    f = _mm_nn("mlp_down", act, wf["w_down"], F32)
    x2, h_next = _res_norm("mlp_down_norm", f, x1, sm["g_mlp_post"], g_next)
    sv.update(y=y, merged=merged, z=z, x1=x1, h2=h2, r=r, act=act, f=f)
    return x2, h_next


def _pad_rows(t):
    return jnp.pad(t, ((0, (-t.shape[0]) % 8), (0, 0)))


def _layer_bwd(dx2, df, mem, sm, wf, sv, send_off, below):
    c = sv["a1"].shape[1]
    d_up = _mm_relu2_bwd("mlp_down_bwd", df, wf["w_down"], sv["r"])
    gw_down = _mm_tn("gw_down", sv["act"], df, BF16)
    dh2 = _mm_nt("mlp_up_bwd", d_up, wf["w_up"], F32)
    gw_up = _mm_tn("gw_up", sv["h2"], d_up, BF16)
    sent = send_off(dict(w_up=gw_up, w_down=gw_down))
    dx1, dg_mlp_pre, dz, dg_mix_post = _rms_bwd("rms_bwd_mlp_pre", dh2, sv["x1"], sm["g_mlp_pre"], F32, sent,
                                                residual=dx2, below=(sv["z"], sm["g_mix_post"]))
    dy, dproj = _merge_bwd(dz, wf["w_o"], sv["y"], sv["proj"])
    gw_o = _mm_tn("gw_o", sv["merged"], dz, BF16)
    dabo = _mm_nt("branch_out_bwd", dy, wf["w_cat"], F32)
    gw_cat = _mm_tn("gw_branch_out", sv["abo"], dy, BF16)
    dproj, dkv, dconv_a_w, misc = _branch_bwd(sv["proj"], sv["a1"], dabo, sv["kv"], wf["conv_a_w"], sm["ln_a_g"],
                                              sm["ln_a_b"], wf["conv_b_w"], dproj)
    dkv = dkv.astype(BF16)
    gw_kv = _mm_tn("gw_kv", sv["mem_n"], dkv, BF16)
    dmem_n = _mm_nt("kv_bwd", dkv, wf["w_kv"], F32)
    _, dg_mem = _rms_bwd("rms_bwd_mem", dmem_n, mem, sm["g_mem"], BF16, dmem_n)
    gw_in = _mm_tn("gw_in", sv["h"], dproj, BF16)
    sent = send_off(dict(w_in=gw_in, w_kv=gw_kv, w_cat=gw_cat, w_o=gw_o))
    dh = _mm_nt("proj_bwd", dproj, wf["w_in"], F32, after=sent)
    dx, dg_mix_pre, *rest = _rms_bwd("rms_bwd_mix_pre", dh, sv["x"], sm["g_mix_pre"], F32, dh, residual=dx1,
                                     below=below)
    small = dict(g_mix_pre=dg_mix_pre, g_mem=dg_mem, g_mix_post=dg_mix_post, g_mlp_pre=dg_mlp_pre, misc=misc,
                 conv_a_w=dconv_a_w)
    return dx, small, rest


def _pack_small(small, c):
    rows = [_pad_rows(small[k].reshape(2, c)) for k in _SMALL_D_NAMES]
    return jnp.concatenate(rows + [small["misc"], small["conv_a_w"]], axis=0)


_SMALL_D_NAMES = ["g_mix_pre", "g_mem", "g_mix_post", "g_mlp_pre", "g_mlp_post"]
_SMALL_C_NAMES = ["conv_a_b", "ln_a_g", "ln_a_b"]
_SMALL_MISC_ROW = 8 * len(_SMALL_D_NAMES)
_SMALL_CONV_B_ROW = _SMALL_MISC_ROW + len(_SMALL_C_NAMES)
_SMALL_CONV_A_ROW = _SMALL_MISC_ROW + 8
_SMALL_ROWS = _SMALL_CONV_A_ROW + HALO_A


def kernel(x, mem, g_mix_pre, w_in, conv_a_w, conv_a_b, ln_a_g, ln_a_b, w_a_out, conv_b_w, w_b_out, g_mem, w_kv, w_x_out, w_o, g_mix_post, g_mlp_pre, w_up, w_down, g_mlp_post, loss_target, m_g_mix_pre, m_w_in, m_conv_a_w, m_conv_a_b, m_ln_a_g, m_ln_a_b, m_w_a_out, m_conv_b_w, m_w_b_out, m_g_mem, m_w_kv, m_w_x_out, m_w_o, m_g_mix_post, m_g_mlp_pre, m_w_up, m_w_down, m_g_mlp_post, v_g_mix_pre, v_w_in, v_conv_a_w, v_conv_a_b, v_ln_a_g, v_ln_a_b, v_w_a_out, v_conv_b_w, v_w_b_out, v_g_mem, v_w_kv, v_w_x_out, v_w_o, v_g_mix_post, v_g_mlp_pre, v_w_up, v_w_down, v_g_mlp_post):
    names = ["g_mix_pre", "w_in", "conv_a_w", "conv_a_b", "ln_a_g", "ln_a_b", "w_a_out", "conv_b_w", "w_b_out",
             "g_mem", "w_kv", "w_x_out", "w_o", "g_mix_post", "g_mlp_pre", "w_up", "w_down", "g_mlp_post"]
    w = dict(zip(names, [g_mix_pre, w_in, conv_a_w, conv_a_b, ln_a_g, ln_a_b, w_a_out, conv_b_w, w_b_out, g_mem,
                         w_kv, w_x_out, w_o, g_mix_post, g_mlp_pre, w_up, w_down, g_mlp_post]))
    mo = dict(zip(names, [m_g_mix_pre, m_w_in, m_conv_a_w, m_conv_a_b, m_ln_a_g, m_ln_a_b, m_w_a_out, m_conv_b_w,
                          m_w_b_out, m_g_mem, m_w_kv, m_w_x_out, m_w_o, m_g_mix_post, m_g_mlp_pre, m_w_up, m_w_down,
                          m_g_mlp_post]))
    vo = dict(zip(names, [v_g_mix_pre, v_w_in, v_conv_a_w, v_conv_a_b, v_ln_a_g, v_ln_a_b, v_w_a_out, v_conv_b_w,
                          v_w_b_out, v_g_mem, v_w_kv, v_w_x_out, v_w_o, v_g_mix_post, v_g_mlp_pre, v_w_up, v_w_down,
                          v_g_mlp_post]))
    depth = w_in.shape[0]
    c = conv_a_b.shape[1]
    cs = conv_a_w.shape[2]
    me = _my_index()
    xs, mems, tgt = x[0], mem[0], loss_target[0]

    big_names = ["w_in", "w_kv", "w_cat", "w_o", "w_up", "w_down"]
    big_axes = [1, 0, 2, 0, 1, 0]
    n_early = 2
    branch_names = ["w_a_out", "w_b_out", "w_x_out"]
    smalls = [{k: w[k][l][None, :] for k in _SMALL_D_NAMES + _SMALL_C_NAMES} for l in range(depth)]

    def weight_shards(l):
        return [w_in[l].astype(BF16), w_kv[l].astype(BF16), jnp.stack([w[k][l] for k in branch_names]).astype(BF16),
                w_o[l].astype(BF16), w_up[l].astype(BF16), w_down[l].astype(BF16)]

    def start_gather(l, after):
        shards = weight_shards(l)
        early, token = _exchange_start(f"gather_start_{l}_early", "gather", shards[:n_early], big_axes[:n_early],
                                       after)
        late, token = _exchange_start(f"gather_start_{l}_late", "gather", shards[n_early:], big_axes[n_early:], token)
        return early, late, token

    taps_a, taps_b = _all_gather(
        "gather_conv_taps",
        [jnp.pad(conv_a_w, ((0, 0), (0, HALO_A - CONV_A_K), (0, 0))),
         jnp.pad(conv_b_w, ((0, 0), (0, HALO_B - CONV_B_K), (0, 0)))], [2, 2], [])

    shards = weight_shards(0)
    n_first = 4
    first = _all_gather("gather_0_early", shards[:n_first], big_axes[:n_first], [taps_a])
    late, token = _exchange_start("gather_start_0_late", "gather", shards[n_first:], big_axes[n_first:], first[0])
    fulls = [None] * depth
    saved = []
    xc = xs
    h = _rmsnorm("rms_mix_pre", xs, smalls[0]["g_mix_pre"], token)
    for l in range(depth):
        n_here = n_first if l == 0 else n_early
        arrived = first if l == 0 else _exchange_wait(f"gather_wait_{l}_early", early, [xc])
        fulls[l] = dict(zip(big_names[:n_here], arrived))
        fulls[l]["conv_a_w"], fulls[l]["conv_b_w"] = taps_a[l], taps_b[l]
        sv = _layer_fwd_branches(xc, h, mems, smalls[l], fulls[l])
        fulls[l].update(zip(big_names[n_here:], _exchange_wait(f"gather_wait_{l}_late", late, [sv["abo"]])))
        token = fulls[l]["w_o"]
        if l + 1 < depth:
            early, late, token = start_gather(l + 1, token)
        xc, h = _layer_fwd_rest(smalls[l], fulls[l], sv, token,
                                smalls[l + 1]["g_mix_pre"] if l + 1 < depth else None)
        saved.append(sv)
    loss_part, dx = _loss_head(xc, tgt)
    loss = lax.psum(loss_part, MESH_AXES)

    upd = {}

    def adamw_group(l, group, slots, after):
        for k, sl in zip(group, slots):
            if k == "w_cat":
                for b, nm in enumerate(branch_names):
                    upd[nm] = _adamw("adamw_" + nm, sl, b, w[nm], mo[nm], vo[nm], l, upd.get(nm), after)
            else:
                upd[k] = _adamw("adamw_" + k, sl, None, w[k], mo[k], vo[k], l, upd.get(k), after)

    in_flight = [None]

    def take_in(after):
        handle, l, group = in_flight[0]
        return _exchange_wait(f"scatter_wait_{l}_{group[0]}", handle, after), l, group

    def send_off_layer(l):
        def send_off(grads):
            group = [k for k in big_names if k in grads]
            arrays = [grads[k] for k in group]
            arrived = None if in_flight[0] is None else take_in([arrays[0]])
            handle, token = _exchange_start(f"scatter_start_{l}_{group[0]}", "scatter", arrays,
                                            [big_axes[big_names.index(k)] for k in group],
                                            arrays[0] if arrived is None else arrived[0][0])
            in_flight[0] = (handle, l, group)
            if arrived is not None:
                adamw_group(arrived[1], arrived[2], arrived[0], token)
            return token
        return send_off

    small_g = [dict() for _ in range(depth)]
    df, small_g[depth - 1]["g_mlp_post"] = _rms_bwd("rms_bwd_top", dx, saved[depth - 1]["f"],
                                                    smalls[depth - 1]["g_mlp_post"], BF16, dx)
    for l in reversed(range(depth)):
        below = (saved[l - 1]["f"], smalls[l - 1]["g_mlp_post"]) if l > 0 else None
        dx, small, rest = _layer_bwd(dx, df, mems, smalls[l], fulls[l], saved[l], send_off_layer(l), below)
        small_g[l].update(small)
        if l > 0:
            df, small_g[l - 1]["g_mlp_post"] = rest
    small_parts = [_pack_small(small_g[l], c) for l in range(depth)]

    slots, l, group = take_in([u[0] for u in upd.values()] + [dx])
    adamw_group(l, group, slots, dx)

    tot = _all_reduce_small(jnp.concatenate(small_parts, axis=0), [upd[k][0] for k in group if k in upd])
    tot = tot.reshape(depth, _SMALL_ROWS, c)
    g_small = {}
    for n_, nm in enumerate(_SMALL_D_NAMES):
        g_small[nm] = tot[:, 8 * n_:8 * n_ + 2, :].reshape(depth, 2 * c)
    for n_, nm in enumerate(_SMALL_C_NAMES):
        g_small[nm] = tot[:, _SMALL_MISC_ROW + n_, :]
    g_taps = {"conv_a_w": tot[:, _SMALL_CONV_A_ROW:_SMALL_CONV_A_ROW + CONV_A_K, :],
              "conv_b_w": tot[:, _SMALL_CONV_B_ROW:_SMALL_CONV_B_ROW + CONV_B_K, :]}

    def pack(t):
        return jnp.concatenate([_pad_rows(t[nm].reshape(-1, c)) for nm in _SMALL_D_NAMES + _SMALL_C_NAMES], axis=0)

    res = _adamw("adamw_small", pack(g_small)[None], None, pack(w)[None], pack(mo)[None], pack(vo)[None], 0, None, tot)
    row = 0
    for nm in _SMALL_D_NAMES + _SMALL_C_NAMES:
        n_rows = w[nm].size // c
        upd[nm] = [t[0, row:row + n_rows].reshape(w[nm].shape) for t in res]
        row += n_rows + (-n_rows) % 8
    for nm, g in g_taps.items():
        shp = w[nm].shape
        mine = lax.dynamic_slice_in_dim(g, me * cs, cs, axis=2).reshape(1, -1, cs)
        res = _adamw("adamw_" + nm, mine, None, w[nm].reshape(1, -1, cs), mo[nm].reshape(1, -1, cs),
                     vo[nm].reshape(1, -1, cs), 0, None, tot)
        upd[nm] = [t.reshape(shp) for t in res]

    return (loss, dx[None], *[upd[nm][0] for nm in names], *[upd[nm][1] for nm in names],
            *[upd[nm][2] for nm in names], *[upd[nm][3] for nm in names])
```
